```python
import jax, jax.numpy as jnp
from jax import lax
import numpy as np

D_MODEL = 1024
BATCH = 1
SEQ = 16384
DEPTH = 1

N_META = 16
BLOCK = 128
MIX_WIDTH = D_MODEL
HG_HEADS = 4
HG_DK = 128
HG_DV = (MIX_WIDTH // 2) // HG_HEADS
SB_HEADS = 8
SB_DH = (MIX_WIDTH // 2) // SB_HEADS
HG_KEY = HG_HEADS * HG_DK
HG_VAL = HG_HEADS * HG_DV
SB_W = SB_HEADS * SB_DH
IN_COLS = 2 * HG_KEY + 2 * HG_VAL + 3 * SB_W
D_FF = ((8 * D_MODEL // 3 + 127) // 128) * 128
RMS_EPS = 1e-6

kernel_name = "hymba_hgrn2_stickbreaking_macaron"


def rms_norm(x, gain):
    xf = x.astype(jnp.float32)
    y = xf * lax.rsqrt(jnp.mean(xf * xf, axis=-1, keepdims=True) + RMS_EPS)
    return (y * gain.astype(jnp.float32)).astype(x.dtype)


def swiglu(h, w_in, w_out):
    gu = h @ w_in
    g, u = jnp.split(gu, 2, axis=-1)
    return (jax.nn.silu(g) * u) @ w_out


def split_heads(t, n_heads):
    b, l, w = t.shape
    return t.reshape(b, l, n_heads, w // n_heads).transpose(0, 2, 1, 3)


def hgrn2(q_raw, f_raw, i_raw, g_raw, lb, out_gain, valid):
    b, L, _ = q_raw.shape
    n_chunks = L // BLOCK
    lbf = lb.astype(jnp.float32)
    z = f_raw.astype(jnp.float32)
    f = lbf + (1.0 - lbf) * jax.nn.sigmoid(z)
    vmask = valid[None, :, None]
    logf = jnp.where(vmask, jnp.log(f), 0.0)
    k = jnp.where(vmask, (1.0 - lbf) * jax.nn.sigmoid(-z), 0.0)
    q = jax.nn.silu(q_raw.astype(jnp.float32))
    v = i_raw.astype(jnp.float32)

    def to_chunks(t, n_heads):
        t = split_heads(t, n_heads)
        t = t.reshape(b, n_heads, n_chunks, BLOCK, t.shape[-1])
        return jnp.moveaxis(t, 2, 0)

    qc, kc, vc, lfc = (to_chunks(q, HG_HEADS), to_chunks(k, HG_HEADS),
                       to_chunks(v, HG_HEADS), to_chunks(logf, HG_HEADS))
    causal = jnp.tril(jnp.ones((BLOCK, BLOCK), dtype=bool))[:, :, None]

    def step(S, inp):
        qk, kk, vk, lf = inp
        bcum = jnp.cumsum(lf, axis=-2)
        o_inter = jnp.einsum('bhtd,bhdv->bhtv', qk * jnp.exp(bcum), S)
        diff = bcum[:, :, :, None, :] - bcum[:, :, None, :, :]
        decay = jnp.exp(jnp.where(causal, diff, -jnp.inf))
        attn = jnp.einsum('bhtd,bhsd,bhtsd->bhts', qk, kk, decay)
        o_intra = jnp.einsum('bhts,bhsv->bhtv', attn, vk)
        b_last = bcum[:, :, -1:, :]
        S_new = (jnp.exp(b_last[:, :, 0, :])[..., None] * S
                 + jnp.einsum('bhsd,bhsv->bhdv', kk * jnp.exp(b_last - bcum), vk))
        return S_new, o_inter + o_intra

    S0 = jnp.zeros((b, HG_HEADS, HG_DK, HG_DV), jnp.float32)
    _, ys = lax.scan(step, S0, (qc, kc, vc, lfc))
    o = jnp.moveaxis(ys, 0, 2).reshape(b, HG_HEADS, L, HG_DV).transpose(0, 2, 1, 3)
    o = rms_norm(o, out_gain.reshape(HG_HEADS, HG_DV)).reshape(b, L, HG_VAL)
    return o * jax.nn.silu(g_raw.astype(jnp.float32))


def stick_breaking(q_raw, k_raw, v_raw, q_gain, k_gain, valid):
    b, L, _ = q_raw.shape
    n_blocks = L // BLOCK
    q = rms_norm(split_heads(q_raw.astype(jnp.float32), SB_HEADS), q_gain)
    k = rms_norm(split_heads(k_raw.astype(jnp.float32), SB_HEADS), k_gain)
    v = split_heads(v_raw.astype(jnp.float32), SB_HEADS)
    scale = 1.0 / np.sqrt(SB_DH).astype(np.float32)
    kpos = jnp.arange(L, dtype=jnp.int32)
    qb = jnp.moveaxis(q.reshape(b, SB_HEADS, n_blocks, BLOCK, SB_DH), 2, 0)
    starts = jnp.arange(n_blocks, dtype=jnp.int32) * BLOCK

    def block(args):
        qblk, start = args
        qpos = start + jnp.arange(BLOCK, dtype=jnp.int32)
        z = jnp.einsum('bhqd,bhkd->bhqk', qblk, k) * scale
        allowed = (kpos[None, :] < qpos[:, None]) & valid[None, :]
        log_beta = jnp.where(allowed, jax.nn.log_sigmoid(z), -jnp.inf)
        log_keep = jnp.where(allowed, jax.nn.log_sigmoid(-z), 0.0)
        suffix = lax.cumsum(log_keep, axis=3, reverse=True) - log_keep
        w = jnp.exp(log_beta + suffix)
        return jnp.einsum('bhqk,bhkd->bhqd', w, v)

    out = lax.map(block, (qb, starts))
    out = jnp.moveaxis(out, 0, 2).reshape(b, SB_HEADS, L, SB_DH)
    return out.transpose(0, 2, 1, 3).reshape(b, L, SB_W)


def setup_inputs(seed: int = 0) -> dict:
    key = jax.random.key(seed)
    ks = jax.random.split(key, 16)
    f32 = jnp.float32
    nrm = lambda k, shape, s: jax.random.normal(k, shape, f32) * s
    gain = lambda k, shape: 1.0 + 0.02 * jax.random.normal(k, shape, f32)
    return {
        "x": jax.random.normal(ks[0], (BATCH, SEQ, D_MODEL), f32),
        "meta_tokens": nrm(ks[1], (N_META, D_MODEL), 1.0),
        "ffn1_norm": gain(ks[2], (DEPTH, D_MODEL)),
        "ffn1_w_in": nrm(ks[3], (DEPTH, D_MODEL, 2 * D_FF), D_MODEL ** -0.5),
        "ffn1_w_out": nrm(ks[4], (DEPTH, D_FF, D_MODEL), D_FF ** -0.5),
        "mix_norm": gain(ks[5], (DEPTH, D_MODEL)),
        "w_in": nrm(ks[6], (DEPTH, D_MODEL, IN_COLS), D_MODEL ** -0.5),
        "hgrn_lb_logits": nrm(ks[7], (DEPTH + 1, HG_KEY), 0.5),
        "hgrn_out_norm": gain(ks[8], (DEPTH, HG_VAL)),
        "sb_q_norm": gain(ks[9], (DEPTH, SB_DH)),
        "sb_k_norm": gain(ks[10], (DEPTH, SB_DH)),
        "w_out": nrm(ks[11], (DEPTH, MIX_WIDTH, D_MODEL), MIX_WIDTH ** -0.5),
        "ffn2_norm": gain(ks[12], (DEPTH, D_MODEL)),
        "ffn2_w_in": nrm(ks[13], (DEPTH, D_MODEL, 2 * D_FF), D_MODEL ** -0.5),
        "ffn2_w_out": nrm(ks[14], (DEPTH, D_FF, D_MODEL), D_FF ** -0.5),
    }


def reference(x, meta_tokens, ffn1_norm, ffn1_w_in, ffn1_w_out, mix_norm, w_in,
              hgrn_lb_logits, hgrn_out_norm, sb_q_norm, sb_k_norm, w_out,
              ffn2_norm, ffn2_w_in, ffn2_w_out):
    b = x.shape[0]
    pad = (-N_META) % BLOCK
    meta = jnp.broadcast_to(meta_tokens.astype(x.dtype)[None], (b, N_META, D_MODEL))
    h = jnp.concatenate([jnp.zeros((b, pad, D_MODEL), x.dtype), meta, x], axis=1)
    Lp = h.shape[1]
    valid = jnp.arange(Lp, dtype=jnp.int32) >= pad
    lb_all = jnp.cumsum(jax.nn.softmax(hgrn_lb_logits.astype(jnp.float32), axis=0), axis=0)

    for l in range(DEPTH):
        h = h + 0.5 * swiglu(rms_norm(h, ffn1_norm[l]), ffn1_w_in[l], ffn1_w_out[l])
        u = rms_norm(h, mix_norm[l]) @ w_in[l]
        hq, hf, hi, hg, sq, sk, sv = jnp.split(
            u, np.cumsum([HG_KEY, HG_KEY, HG_VAL, HG_VAL, SB_W, SB_W])[:6].tolist(), axis=-1)
        o_hg = hgrn2(hq, hf, hi, hg, lb_all[l], hgrn_out_norm[l], valid)
        o_sb = stick_breaking(sq, sk, sv, sb_q_norm[l], sb_k_norm[l], valid)
        o = jnp.concatenate([o_hg, o_sb], axis=-1).astype(h.dtype)
        h = h + o @ w_out[l]
        h = h + 0.5 * swiglu(rms_norm(h, ffn2_norm[l]), ffn2_w_in[l], ffn2_w_out[l])

    return h[:, pad + N_META:, :]
```

```python
import functools

import numpy as np
import jax
import jax.numpy as jnp
from jax import lax
from jax.experimental import pallas as pl
from jax.experimental.pallas import tpu as pltpu

F32 = jnp.float32
BF16 = jnp.bfloat16

D_MODEL = 1024
N_META = 16
BLOCK = 128
PAD = (-N_META) % BLOCK
HG_HEADS = 4
HG_DK = 128
HG_DV = 128
HG_W = HG_HEADS * HG_DK
SB_HEADS = 8
SB_DH = 64
SB_W = SB_HEADS * SB_DH
SB_PAIRS = SB_HEADS // 2
D_FF = 2816
RMS_EPS = 1e-6
FF_CHUNK = 256
N_LEVELS = 7
EXP_ZERO_BELOW = -104.0
VMEM_LIMIT = 56 * 1024 * 1024


def _dot(a, b):
    return jnp.dot(a, b, preferred_element_type=F32)


def _rms(x, gain):
    ms = jnp.mean(x * x, axis=-1, keepdims=True)
    return x * lax.rsqrt(ms + RMS_EPS) * gain


def _split3(x):
    hi = x.astype(BF16)
    r1 = x - hi.astype(F32)
    mid = r1.astype(BF16)
    lo = (r1 - mid.astype(F32)).astype(BF16)
    return hi, mid, lo


def _swiglu(xn, w_in_ref, w_out_ref, act_ref):
    for c in range(D_FF // FF_CHUNK):
        lo, hi = c * FF_CHUNK, (c + 1) * FF_CHUNK
        g = _dot(xn, w_in_ref[:, lo:hi])
        u = _dot(xn, w_in_ref[:, D_FF + lo:D_FF + hi])
        act_ref[:, lo:hi] = (g * jax.nn.sigmoid(g) * u).astype(BF16)
    return _dot(act_ref[...], w_out_ref[...])


def _ffn_in_kernel(h_ref, g1_ref, w1i_ref, w1o_ref, gm_ref, whg_ref, wq_ref,
                   wkt_ref, wv_ref, gk_ref,
                   h1_ref, uhg_ref, q_ref, kt_ref, v_ref, act_ref):
    tm = h_ref.shape[0]
    h = h_ref[...]
    xn = _rms(h, g1_ref[...]).astype(BF16)
    h1 = h + 0.5 * _swiglu(xn, w1i_ref, w1o_ref, act_ref)
    h1_ref[...] = h1
    xm = _rms(h1, gm_ref[...]).astype(BF16)
    uhg_ref[...] = _dot(xm, whg_ref[...])
    q_ref[...] = _dot(xm, wq_ref[...])
    v_ref[...] = _dot(xm, wv_ref[...]).astype(BF16)
    kt = lax.dot_general(wkt_ref[...], xm, (((1,), (1,)), ((), ())),
                         preferred_element_type=F32)
    k3 = kt.reshape(SB_HEADS, SB_DH, tm)
    ms = jnp.mean(k3 * k3, axis=1, keepdims=True)
    kn = (k3 * lax.rsqrt(ms + RMS_EPS)).reshape(SB_W, tm) * gk_ref[...]
    kn = kn.astype(BF16)
    for t in range(tm // BLOCK):
        kt_ref[t] = kn[:, t * BLOCK:(t + 1) * BLOCK]


def _ffn_in(hp, g1, w1i, w1o, gm, whg, wq, wkt, wv, gk, tm):
    lp = hp.shape[0]
    n_blk = lp // BLOCK
    const = lambda shape: pl.BlockSpec(shape, lambda i: (0,) * len(shape),
                                       pipeline_mode=pl.Buffered(1))
    rows = lambda w: pl.BlockSpec((tm, w), lambda i: (i, 0))
    return pl.pallas_call(
        _ffn_in_kernel,
        grid=(pl.cdiv(lp, tm),),
        in_specs=[rows(D_MODEL), const((1, D_MODEL)),
                  const((D_MODEL, 2 * D_FF)), const((D_FF, D_MODEL)),
                  const((1, D_MODEL)), const((D_MODEL, 4 * HG_W)),
                  const((D_MODEL, SB_W)), const((SB_W, D_MODEL)),
                  const((D_MODEL, SB_W)), const((SB_W, 1))],
        out_specs=[rows(D_MODEL), rows(4 * HG_W), rows(SB_W),
                   pl.BlockSpec((tm // BLOCK, SB_W, BLOCK), lambda i: (i, 0, 0)),
                   rows(SB_W)],
        out_shape=[jax.ShapeDtypeStruct((lp, D_MODEL), F32),
                   jax.ShapeDtypeStruct((lp, 4 * HG_W), F32),
                   jax.ShapeDtypeStruct((lp, SB_W), F32),
                   jax.ShapeDtypeStruct((n_blk, SB_W, BLOCK), BF16),
                   jax.ShapeDtypeStruct((lp, SB_W), BF16)],
        scratch_shapes=[pltpu.VMEM((tm, D_FF), BF16)],
        compiler_params=pltpu.CompilerParams(
            dimension_semantics=("arbitrary",), vmem_limit_bytes=VMEM_LIMIT),
        name="ffn_in",
    )(hp, g1, w1i, w1o, gm, whg, wq, wkt, wv, gk)


def _hgrn_sum_matrix():
    t = np.arange(BLOCK)[:, None]
    j = np.arange(BLOCK)[None, :]
    mats = [(j <= t)]
    for lvl in range(N_LEVELS):
        c = 1 << lvl
        m = (t // (2 * c)) * (2 * c) + c
        upper = (t >= m) & (j >= m) & (j <= t)
        lower = (t < m) & (j > t) & (j <= m - 1)
        mats.append(upper | lower)
    return np.concatenate(mats, axis=0).astype(np.float32)


def _hgrn_kernel(q_ref, f_ref, i_ref, g_ref, lbl_ref, gain_ref, m_ref,
                 o_ref, st_ref):
    c = pl.program_id(1)

    @pl.when(c == 0)
    def _():
        st_ref[...] = jnp.zeros_like(st_ref)

    lg = lbl_ref[...]
    e = jnp.exp(lg - jnp.max(lg, axis=0, keepdims=True))
    lb = e[0:1] / jnp.sum(e, axis=0, keepdims=True)

    z = f_ref[...]
    row = lax.broadcasted_iota(jnp.int32, (BLOCK, HG_DK), 0)
    col = lax.broadcasted_iota(jnp.int32, (BLOCK, HG_DK), 1)
    valid = jnp.logical_or(c > 0, row >= PAD)
    f = lb + (1.0 - lb) * jax.nn.sigmoid(z)
    logf = jnp.where(valid, jnp.log(f), 0.0)
    k = jnp.where(valid, (1.0 - lb) * jax.nn.sigmoid(-z), 0.0)
    qr = q_ref[...]
    q = qr * jax.nn.sigmoid(qr)
    v = i_ref[...].astype(BF16)

    hi, mid, lo = _split3(logf)
    m = m_ref[...]
    x = _dot(m, hi) + _dot(m, mid) + _dot(m, lo)
    bcum = x[0:BLOCK]
    b_last = bcum[BLOCK - 1:BLOCK]

    nt = (((1,), (1,)), ((), ()))
    attn = jnp.where(row == col,
                     lax.dot_general(q.astype(BF16), k.astype(BF16), nt,
                                     preferred_element_type=F32), 0.0)
    for lvl in range(N_LEVELS):
        el = jnp.exp(x[(lvl + 1) * BLOCK:(lvl + 2) * BLOCK])
        is_q = ((row >> lvl) & 1) == 1
        ql = jnp.where(is_q, q * el, 0.0).astype(BF16)
        kl = jnp.where(is_q, 0.0, k * el).astype(BF16)
        al = lax.dot_general(ql, kl, nt, preferred_element_type=F32)
        if lvl + 1 < N_LEVELS:
            al = jnp.where((row >> (lvl + 1)) == (col >> (lvl + 1)), al, 0.0)
        attn = attn + al
    o = _dot(attn.astype(BF16), v)

    st = st_ref[...]
    o = o + lax.dot_general((q * jnp.exp(bcum)).astype(BF16), st.astype(BF16),
                            nt, preferred_element_type=F32)
    kd = (k * jnp.exp(b_last - bcum)).astype(BF16)
    st_ref[...] = st * jnp.exp(b_last) + lax.dot_general(
        v, kd, (((0,), (0,)), ((), ())), preferred_element_type=F32)

    gr = g_ref[...]
    o = _rms(o, gain_ref[...]) * (gr * jax.nn.sigmoid(gr))
    o_ref[...] = o.astype(BF16)


def _hgrn(uhg, lb_logits, out_gain, n_real_blk):
    lp = uhg.shape[0]
    n_blk = lp // BLOCK
    phys = lambda c: (c + n_real_blk) % n_blk
    part = lambda p: pl.BlockSpec((BLOCK, HG_DK),
                                  lambda h, c: (phys(c), p * HG_HEADS + h))
    msum = jnp.asarray(_hgrn_sum_matrix(), dtype=BF16)
    return pl.pallas_call(
        _hgrn_kernel,
        grid=(HG_HEADS, n_blk),
        in_specs=[part(0), part(1), part(2), part(3),
                  pl.BlockSpec((2, HG_DK), lambda h, c: (0, h)),
                  pl.BlockSpec((1, HG_DV), lambda h, c: (0, h)),
                  pl.BlockSpec(msum.shape, lambda h, c: (0, 0))],
        out_specs=pl.BlockSpec((BLOCK, HG_DV), lambda h, c: (phys(c), h)),
        out_shape=jax.ShapeDtypeStruct((lp, HG_HEADS * HG_DV), BF16),
        scratch_shapes=[pltpu.VMEM((HG_DV, HG_DK), F32)],
        compiler_params=pltpu.CompilerParams(
            dimension_semantics=("arbitrary", "arbitrary")),
        name="hgrn2",
    )(uhg, uhg, uhg, uhg, lb_logits, out_gain, msum)


def _sb_sum_matrix():
    j = np.arange(BLOCK)[:, None]
    s = np.arange(BLOCK)[None, :]
    return np.concatenate([(j >= s), np.ones((BLOCK, BLOCK), bool)],
                          axis=1).astype(np.float32)


def _sb_kernel(q_ref, kt_ref, v_ref, gq_ref, w_ref, o_ref,
               qn_ref, acc_ref, crep_ref, *, n_real_blk):
    n_blk = kt_ref.shape[0]
    c = pl.program_id(0)
    row = lax.broadcasted_iota(jnp.int32, (BLOCK, BLOCK), 0)
    col = lax.broadcasted_iota(jnp.int32, (BLOCK, BLOCK), 1)
    low = col < SB_DH
    scale = 1.0 / np.sqrt(np.float32(SB_DH))

    for p in range(SB_PAIRS):
        q = q_ref[:, p * BLOCK:(p + 1) * BLOCK]
        gq = gq_ref[...]
        for a in range(2):
            own = low if a == 0 else jnp.logical_not(low)
            ms = jnp.sum(jnp.where(own, q * q, 0.0), axis=-1,
                         keepdims=True) * (1.0 / SB_DH)
            qn = jnp.where(own, q * lax.rsqrt(ms + RMS_EPS) * gq * scale, 0.0)
            qn_ref[2 * p + a] = qn.astype(BF16)
    acc_ref[...] = jnp.zeros_like(acc_ref)
    crep_ref[...] = jnp.zeros_like(crep_ref)
    wsum = w_ref[...]

    def tile(j):
        pj = lax.rem(j + n_real_blk, n_blk)
        kpos = j * BLOCK + col
        allowed = jnp.logical_and(kpos < c * BLOCK + row, kpos >= PAD)
        cmax = jnp.full((1, 1), -jnp.inf, F32)
        for p in range(SB_PAIRS):
            kt = kt_ref[pj, p * BLOCK:(p + 1) * BLOCK, :]
            vt = v_ref[pl.ds(pl.multiple_of(pj * BLOCK, BLOCK), BLOCK),
                       p * BLOCK:(p + 1) * BLOCK]
            pv = []
            for a in range(2):
                hd = 2 * p + a
                z = _dot(qn_ref[hd], kt)
                sp = jnp.maximum(z, 0.0) + jnp.log(1.0 + jnp.exp(-jnp.abs(z)))
                sp = jnp.where(allowed, sp, 0.0)
                hi = sp.astype(BF16)
                lo = (sp - hi.astype(F32)).astype(BF16)
                r = _dot(hi, wsum) + _dot(lo, wsum)
                crep = crep_ref[hd]
                w = jnp.where(allowed, jnp.exp(z - r[:, :BLOCK] + crep), 0.0)
                pv.append(_dot(w.astype(BF16), vt))
                crep = crep - r[:, BLOCK:]
                crep_ref[hd] = crep
                cmax = jnp.maximum(cmax, jnp.max(crep, keepdims=True))
            acc_ref[p] += jnp.where(low, pv[0], pv[1])
        return cmax[0, 0]

    cmax0 = tile(c)

    def cond(carry):
        j, cmax = carry
        return jnp.logical_and(j >= 0, cmax > EXP_ZERO_BELOW)

    def body(carry):
        j, _ = carry
        return j - 1, tile(j)

    lax.while_loop(cond, body, (c - 1, cmax0))

    for p in range(SB_PAIRS):
        o_ref[:, p * BLOCK:(p + 1) * BLOCK] = acc_ref[p].astype(BF16)


def _stickbreak(q, kt3, v, gq, n_real_blk):
    lp = q.shape[0]
    n_blk = lp // BLOCK
    phys = lambda c: (c + n_real_blk) % n_blk
    wsum = jnp.asarray(_sb_sum_matrix(), dtype=BF16)
    return pl.pallas_call(
        functools.partial(_sb_kernel, n_real_blk=n_real_blk),
        grid=(n_blk,),
        in_specs=[pl.BlockSpec((BLOCK, SB_W), lambda c: (phys(c), 0)),
                  pl.BlockSpec(kt3.shape, lambda c: (0, 0, 0),
                               pipeline_mode=pl.Buffered(1)),
                  pl.BlockSpec(v.shape, lambda c: (0, 0),
                               pipeline_mode=pl.Buffered(1)),
                  pl.BlockSpec((1, BLOCK), lambda c: (0, 0)),
                  pl.BlockSpec(wsum.shape, lambda c: (0, 0))],
        out_specs=pl.BlockSpec((BLOCK, SB_W), lambda c: (phys(c), 0)),
        out_shape=jax.ShapeDtypeStruct((lp, SB_W), BF16),
        scratch_shapes=[pltpu.VMEM((SB_HEADS, BLOCK, BLOCK), BF16),
                        pltpu.VMEM((SB_PAIRS, BLOCK, BLOCK), F32),
                        pltpu.VMEM((SB_HEADS, BLOCK, BLOCK), F32)],
        compiler_params=pltpu.CompilerParams(
            dimension_semantics=("arbitrary",), vmem_limit_bytes=VMEM_LIMIT),
        name="stickbrk",
    )(q, kt3, v, gq, wsum)


def _ffn_out_kernel(h1_ref, ohg_ref, osb_ref, woa_ref, wob_ref, g2_ref,
                    w2i_ref, w2o_ref, out_ref, act_ref):
    h2 = (h1_ref[...] + _dot(ohg_ref[...], woa_ref[...])
          + _dot(osb_ref[...], wob_ref[...]))
    xn = _rms(h2, g2_ref[...]).astype(BF16)
    out_ref[...] = h2 + 0.5 * _swiglu(xn, w2i_ref, w2o_ref, act_ref)


def _ffn_out(h1, ohg, osb, woa, wob, g2, w2i, w2o, n_rows, tm):
    const = lambda shape: pl.BlockSpec(shape, lambda i: (0,) * len(shape),
                                       pipeline_mode=pl.Buffered(1))
    rows = lambda w: pl.BlockSpec((tm, w), lambda i: (i, 0))
    return pl.pallas_call(
        _ffn_out_kernel,
        grid=(n_rows // tm,),
        in_specs=[rows(D_MODEL), rows(HG_W), rows(SB_W),
                  const((HG_W, D_MODEL)), const((SB_W, D_MODEL)),
                  const((1, D_MODEL)), const((D_MODEL, 2 * D_FF)),
                  const((D_FF, D_MODEL))],
        out_specs=rows(D_MODEL),
        out_shape=jax.ShapeDtypeStruct((n_rows, D_MODEL), F32),
        scratch_shapes=[pltpu.VMEM((tm, D_FF), BF16)],
        compiler_params=pltpu.CompilerParams(
            dimension_semantics=("arbitrary",), vmem_limit_bytes=VMEM_LIMIT),
        name="ffn_out",
    )(h1, ohg, osb, woa, wob, g2, w2i, w2o)


def kernel(x, meta_tokens, ffn1_norm, ffn1_w_in, ffn1_w_out, mix_norm, w_in,
           hgrn_lb_logits, hgrn_out_norm, sb_q_norm, sb_k_norm, w_out,
           ffn2_norm, ffn2_w_in, ffn2_w_out):
    b, seq, _ = x.shape
    assert b == 1 and seq % BLOCK == 0
    assert ffn1_norm.shape[0] == 1, "single layer"
    n_real_blk = seq // BLOCK
    tm = 512
    assert seq % tm == 0

    meta_blk = jnp.concatenate(
        [jnp.zeros((PAD, D_MODEL), x.dtype), meta_tokens.astype(x.dtype)], axis=0)
    hp = jnp.concatenate([x[0], meta_blk], axis=0)

    win = w_in[0].astype(BF16)
    o_hg, o_sq, o_sk, o_sv = 4 * HG_W, 4 * HG_W + SB_W, 4 * HG_W + 2 * SB_W, 4 * HG_W + 3 * SB_W
    whg, wq = win[:, :o_hg], win[:, o_hg:o_sq]
    wkt, wv = win[:, o_sq:o_sk].T, win[:, o_sk:o_sv]
    gk = jnp.tile(sb_k_norm[0], SB_HEADS).reshape(SB_W, 1)
    gq = jnp.tile(sb_q_norm[0], 2).reshape(1, BLOCK)

    h1, uhg, q, kt3, v = _ffn_in(
        hp, ffn1_norm, ffn1_w_in[0].astype(BF16), ffn1_w_out[0].astype(BF16),
        mix_norm, whg, wq, wkt, wv, gk, tm)
    ohg = _hgrn(uhg, hgrn_lb_logits, hgrn_out_norm, n_real_blk)
    osb = _stickbreak(q, kt3, v, gq, n_real_blk)
    wo = w_out[0].astype(BF16)
    out = _ffn_out(h1, ohg, osb, wo[:HG_W], wo[HG_W:], ffn2_norm,
                   ffn2_w_in[0].astype(BF16), ffn2_w_out[0].astype(BF16), seq, tm)
    return out[None]
```

```python
import functools

import numpy as np
import jax
import jax.numpy as jnp
from jax import lax
from jax.experimental import pallas as pl
from jax.experimental.pallas import tpu as pltpu

F32 = jnp.float32
BF16 = jnp.bfloat16

D_MODEL = 1024
N_META = 16
BLOCK = 128
PAD = (-N_META) % BLOCK
HG_HEADS = 4
HG_DK = 128
HG_DV = 128
HG_W = HG_HEADS * HG_DK
SB_HEADS = 8
SB_DH = 64
SB_W = SB_HEADS * SB_DH
SB_PAIRS = SB_HEADS // 2
D_FF = 2816
RMS_EPS = 1e-6
FF_CHUNK = 256
N_LEVELS = 7
EXP_ZERO_BELOW = -104.0
VMEM_LIMIT = 56 * 1024 * 1024
NT_DIMS = (((1,), (1,)), ((), ()))
TN_DIMS = (((0,), (0,)), ((), ()))


def _dot(a, b):
    return jnp.dot(a, b, preferred_element_type=F32)


def _dot_nt(a, b):
    return lax.dot_general(a, b, NT_DIMS, preferred_element_type=F32)


def _rms(x, gain):
    ms = jnp.mean(x * x, axis=-1, keepdims=True)
    return x * lax.rsqrt(ms + RMS_EPS) * gain


def _split3(x):
    hi = x.astype(BF16)
    r1 = x - hi.astype(F32)
    mid = r1.astype(BF16)
    lo = (r1 - mid.astype(F32)).astype(BF16)
    return hi, mid, lo


def _swiglu(xn, w_in_ref, w_out_ref, act_ref):
    for c in range(D_FF // FF_CHUNK):
        lo, hi = c * FF_CHUNK, (c + 1) * FF_CHUNK
        g = _dot(xn, w_in_ref[:, lo:hi])
        u = _dot(xn, w_in_ref[:, D_FF + lo:D_FF + hi])
        act_ref[:, lo:hi] = (g * jax.nn.sigmoid(g) * u).astype(BF16)
    return _dot(act_ref[...], w_out_ref[...])


def _ffn_in_kernel(h_ref, g1_ref, w1i_ref, w1o_ref, gm_ref, whg_ref, wq_ref,
                   wkt_ref, wv_ref, gk_ref,
                   h1_ref, uhg_ref, q_ref, kt_ref, v_ref, act_ref):
    tm = h_ref.shape[0]
    h = h_ref[...]
    xn = _rms(h, g1_ref[...]).astype(BF16)
    h1 = h + 0.5 * _swiglu(xn, w1i_ref, w1o_ref, act_ref)
    h1_ref[...] = h1
    xm = _rms(h1, gm_ref[...]).astype(BF16)
    uhg_ref[...] = _dot(xm, whg_ref[...])
    q_ref[...] = _dot(xm, wq_ref[...])
    v_ref[...] = _dot(xm, wv_ref[...]).astype(BF16)
    kt = _dot_nt(wkt_ref[...], xm)
    k3 = kt.reshape(SB_HEADS, SB_DH, tm)
    ms = jnp.mean(k3 * k3, axis=1, keepdims=True)
    kn = (k3 * lax.rsqrt(ms + RMS_EPS)).reshape(SB_W, tm) * gk_ref[...]
    kn = kn.astype(BF16)
    for t in range(tm // BLOCK):
        kt_ref[t] = kn[:, t * BLOCK:(t + 1) * BLOCK]


def _ffn_in(hp, g1, w1i, w1o, gm, whg, wq, wkt, wv, gk, tm):
    lp = hp.shape[0]
    n_blk = lp // BLOCK
    const = lambda shape: pl.BlockSpec(shape, lambda i: (0,) * len(shape),
                                       pipeline_mode=pl.Buffered(1))
    rows = lambda w: pl.BlockSpec((tm, w), lambda i: (i, 0))
    return pl.pallas_call(
        _ffn_in_kernel,
        grid=(pl.cdiv(lp, tm),),
        in_specs=[rows(D_MODEL), const((1, D_MODEL)),
                  const((D_MODEL, 2 * D_FF)), const((D_FF, D_MODEL)),
                  const((1, D_MODEL)), const((D_MODEL, 4 * HG_W)),
                  const((D_MODEL, SB_W)), const((SB_W, D_MODEL)),
                  const((D_MODEL, SB_W)), const((SB_W, 1))],
        out_specs=[rows(D_MODEL), rows(4 * HG_W), rows(SB_W),
                   pl.BlockSpec((tm // BLOCK, SB_W, BLOCK), lambda i: (i, 0, 0)),
                   rows(SB_W)],
        out_shape=[jax.ShapeDtypeStruct((lp, D_MODEL), F32),
                   jax.ShapeDtypeStruct((lp, 4 * HG_W), F32),
                   jax.ShapeDtypeStruct((lp, SB_W), F32),
                   jax.ShapeDtypeStruct((n_blk, SB_W, BLOCK), BF16),
                   jax.ShapeDtypeStruct((lp, SB_W), BF16)],
        scratch_shapes=[pltpu.VMEM((tm, D_FF), BF16)],
        compiler_params=pltpu.CompilerParams(
            dimension_semantics=("arbitrary",), vmem_limit_bytes=VMEM_LIMIT),
        name="ffn_in",
    )(hp, g1, w1i, w1o, gm, whg, wq, wkt, wv, gk)


def _hgrn_sum_matrix():
    t = np.arange(BLOCK)[:, None]
    j = np.arange(BLOCK)[None, :]
    mats = [(j <= t)]
    for lvl in range(N_LEVELS):
        c = 1 << lvl
        m = (t // (2 * c)) * (2 * c) + c
        upper = (t >= m) & (j >= m) & (j <= t)
        lower = (t < m) & (j > t) & (j <= m - 1)
        mats.append(upper | lower)
    return np.concatenate(mats, axis=0).astype(np.float32)


def _hgrn_kernel(q_ref, f_ref, i_ref, g_ref, lbl_ref, gain_ref, m_ref,
                 o_ref, st_ref):
    c = pl.program_id(0)
    heads = range(HG_HEADS)
    hs = lambda a, h: a[:, h * HG_DK:(h + 1) * HG_DK]

    @pl.when(c == 0)
    def _():
        st_ref[...] = jnp.zeros_like(st_ref)

    lg = lbl_ref[...]
    e = jnp.exp(lg - jnp.max(lg, axis=0, keepdims=True))
    lb = e[0:1] / jnp.sum(e, axis=0, keepdims=True)

    z = f_ref[...]
    valid = jnp.logical_or(
        c > 0, lax.broadcasted_iota(jnp.int32, z.shape, 0) >= PAD)
    f = lb + (1.0 - lb) * jax.nn.sigmoid(z)
    logf = jnp.where(valid, jnp.log(f), 0.0)
    k = jnp.where(valid, (1.0 - lb) * jax.nn.sigmoid(-z), 0.0)
    qr = q_ref[...]
    q = qr * jax.nn.sigmoid(qr)
    v = i_ref[...].astype(BF16)

    hi, mid, lo = _split3(logf)
    m = m_ref[...]
    x = _dot(m, hi) + _dot(m, mid) + _dot(m, lo)
    bcum = x[0:BLOCK]
    b_last = bcum[BLOCK - 1:BLOCK]
    qe = (q * jnp.exp(bcum)).astype(BF16)
    kd = (k * jnp.exp(b_last - bcum)).astype(BF16)
    st_decay = jnp.exp(b_last)

    row = lax.broadcasted_iota(jnp.int32, (BLOCK, BLOCK), 0)
    col = lax.broadcasted_iota(jnp.int32, (BLOCK, BLOCK), 1)
    qb, kb = q.astype(BF16), k.astype(BF16)
    diag = row == col
    attn = [jnp.where(diag, _dot_nt(hs(qb, h), hs(kb, h)), 0.0) for h in heads]
    rowf = lax.broadcasted_iota(jnp.int32, z.shape, 0)
    for lvl in range(N_LEVELS):
        el = jnp.exp(x[(lvl + 1) * BLOCK:(lvl + 2) * BLOCK])
        is_q = ((rowf >> lvl) & 1) == 1
        ql = jnp.where(is_q, q * el, 0.0).astype(BF16)
        kl = jnp.where(is_q, 0.0, k * el).astype(BF16)
        same = (row >> (lvl + 1)) == (col >> (lvl + 1))
        for h in heads:
            al = _dot_nt(hs(ql, h), hs(kl, h))
            if lvl + 1 < N_LEVELS:
                al = jnp.where(same, al, 0.0)
            attn[h] = attn[h] + al

    o = []
    for h in heads:
        st = st_ref[h]
        oh = _dot(attn[h].astype(BF16), hs(v, h))
        oh = oh + _dot_nt(hs(qe, h), st.astype(BF16))
        st_ref[h] = st * hs(st_decay, h) + lax.dot_general(
            hs(v, h), hs(kd, h), TN_DIMS, preferred_element_type=F32)
        o.append(_rms(oh, hs(gain_ref[...], h)))
    gr = g_ref[...]
    o_ref[...] = (jnp.concatenate(o, axis=1) * (gr * jax.nn.sigmoid(gr))).astype(BF16)


def _hgrn(uhg, lb_logits, out_gain, n_real_blk):
    lp = uhg.shape[0]
    n_blk = lp // BLOCK
    phys = lambda c: (c + n_real_blk) % n_blk
    part = lambda p: pl.BlockSpec((BLOCK, HG_W), lambda c: (phys(c), p))
    msum = jnp.asarray(_hgrn_sum_matrix(), dtype=BF16)
    return pl.pallas_call(
        _hgrn_kernel,
        grid=(n_blk,),
        in_specs=[part(0), part(1), part(2), part(3),
                  pl.BlockSpec((2, HG_W), lambda c: (0, 0)),
                  pl.BlockSpec((1, HG_W), lambda c: (0, 0)),
                  pl.BlockSpec(msum.shape, lambda c: (0, 0))],
        out_specs=pl.BlockSpec((BLOCK, HG_W), lambda c: (phys(c), 0)),
        out_shape=jax.ShapeDtypeStruct((lp, HG_W), BF16),
        scratch_shapes=[pltpu.VMEM((HG_HEADS, HG_DV, HG_DK), F32)],
        compiler_params=pltpu.CompilerParams(
            dimension_semantics=("arbitrary",)),
        name="hgrn2",
    )(uhg, uhg, uhg, uhg, lb_logits, out_gain, msum)


def _sb_sum_matrix():
    j = np.arange(BLOCK)[:, None]
    s = np.arange(BLOCK)[None, :]
    return np.concatenate([(j >= s), np.ones((BLOCK, BLOCK), bool)],
                          axis=1).astype(np.float32)


def _sb_kernel(q_ref, kt_ref, v_ref, gq_ref, w_ref, o_ref,
               qn_ref, acc_ref, crep_ref, *, n_real_blk):
    n_blk = kt_ref.shape[0]
    c = pl.program_id(0)
    row = lax.broadcasted_iota(jnp.int32, (BLOCK, BLOCK), 0)
    col = lax.broadcasted_iota(jnp.int32, (BLOCK, BLOCK), 1)
    low = col < SB_DH
    scale = 1.0 / np.sqrt(np.float32(SB_DH))
    pairs = range(SB_PAIRS)
    heads = range(SB_HEADS)

    gq = gq_ref[...]
    for p in pairs:
        q = q_ref[:, p * BLOCK:(p + 1) * BLOCK]
        for a in range(2):
            own = low if a == 0 else jnp.logical_not(low)
            ms = jnp.sum(jnp.where(own, q * q, 0.0), axis=-1,
                         keepdims=True) * (1.0 / SB_DH)
            qn = jnp.where(own, q * lax.rsqrt(ms + RMS_EPS) * gq * scale, 0.0)
            qn_ref[p, a * BLOCK:(a + 1) * BLOCK, :] = qn.astype(BF16)
    acc_ref[...] = jnp.zeros_like(acc_ref)
    crep_ref[...] = jnp.zeros_like(crep_ref)

    def tile(j, masked):
        pj = lax.rem(j + n_real_blk, n_blk)
        if masked:
            kpos = j * BLOCK + col
            allowed = jnp.logical_and(kpos < c * BLOCK + row, kpos >= PAD)
        z = []
        for p in pairs:
            kt = kt_ref[pj, p * BLOCK:(p + 1) * BLOCK, :]
            zz = _dot(qn_ref[p], kt)
            z += [zz[:BLOCK], zz[BLOCK:]]
        sp = [jnp.maximum(zh, 0.0) + jnp.log(1.0 + jnp.exp(-jnp.abs(zh)))
              for zh in z]
        if masked:
            sp = [jnp.where(allowed, s, 0.0) for s in sp]
        hi = [s.astype(BF16) for s in sp]
        lo = [(s - h.astype(F32)).astype(BF16) for s, h in zip(sp, hi)]
        r = _dot(jnp.concatenate(hi + lo, axis=0), w_ref[...])
        nh = SB_HEADS * BLOCK
        cmax = None
        w = []
        for hd in heads:
            rh = r[hd * BLOCK:(hd + 1) * BLOCK] + r[nh + hd * BLOCK:nh + (hd + 1) * BLOCK]
            crep = crep_ref[hd]
            wh = jnp.exp(z[hd] - rh[:, :BLOCK] + crep)
            if masked:
                wh = jnp.where(allowed, wh, 0.0)
            w.append(wh.astype(BF16))
            crep = crep - rh[:, BLOCK:]
            crep_ref[hd] = crep
            cmax = crep if cmax is None else jnp.maximum(cmax, crep)
        for p in pairs:
            vt = v_ref[pl.ds(pl.multiple_of(pj * BLOCK, BLOCK), BLOCK),
                       p * BLOCK:(p + 1) * BLOCK]
            pv = _dot(jnp.concatenate([w[2 * p], w[2 * p + 1]], axis=0), vt)
            acc_ref[p] += jnp.where(low, pv[:BLOCK], pv[BLOCK:])
        return jnp.max(cmax)

    cmax0 = tile(c, True)

    def cond(carry):
        j, cmax = carry
        return jnp.logical_and(j >= 1, cmax > EXP_ZERO_BELOW)

    def body(carry):
        j, _ = carry
        return j - 1, tile(j, False)

    j_end, cmax_end = lax.while_loop(cond, body, (c - 1, cmax0))

    @pl.when(jnp.logical_and(j_end == 0, cmax_end > EXP_ZERO_BELOW))
    def _():
        tile(0, True)

    for p in pairs:
        o_ref[:, p * BLOCK:(p + 1) * BLOCK] = acc_ref[p].astype(BF16)


def _stickbreak(q, kt3, v, gq, n_real_blk):
    lp = q.shape[0]
    n_blk = lp // BLOCK
    phys = lambda c: (c + n_real_blk) % n_blk
    wsum = jnp.asarray(_sb_sum_matrix(), dtype=BF16)
    return pl.pallas_call(
        functools.partial(_sb_kernel, n_real_blk=n_real_blk),
        grid=(n_blk,),
        in_specs=[pl.BlockSpec((BLOCK, SB_W), lambda c: (phys(c), 0)),
                  pl.BlockSpec(kt3.shape, lambda c: (0, 0, 0),
                               pipeline_mode=pl.Buffered(1)),
                  pl.BlockSpec(v.shape, lambda c: (0, 0),
                               pipeline_mode=pl.Buffered(1)),
                  pl.BlockSpec((1, BLOCK), lambda c: (0, 0)),
                  pl.BlockSpec(wsum.shape, lambda c: (0, 0))],
        out_specs=pl.BlockSpec((BLOCK, SB_W), lambda c: (phys(c), 0)),
        out_shape=jax.ShapeDtypeStruct((lp, SB_W), BF16),
        scratch_shapes=[pltpu.VMEM((SB_PAIRS, 2 * BLOCK, BLOCK), BF16),
                        pltpu.VMEM((SB_PAIRS, BLOCK, BLOCK), F32),
                        pltpu.VMEM((SB_HEADS, BLOCK, BLOCK), F32)],
        compiler_params=pltpu.CompilerParams(
            dimension_semantics=("arbitrary",), vmem_limit_bytes=VMEM_LIMIT),
        name="stickbrk",
    )(q, kt3, v, gq, wsum)


def _ffn_out_kernel(h1_ref, ohg_ref, osb_ref, woa_ref, wob_ref, g2_ref,
                    w2i_ref, w2o_ref, out_ref, act_ref):
    h2 = (h1_ref[...] + _dot(ohg_ref[...], woa_ref[...])
          + _dot(osb_ref[...], wob_ref[...]))
    xn = _rms(h2, g2_ref[...]).astype(BF16)
    out_ref[...] = h2 + 0.5 * _swiglu(xn, w2i_ref, w2o_ref, act_ref)


def _ffn_out(h1, ohg, osb, woa, wob, g2, w2i, w2o, n_rows, tm):
    const = lambda shape: pl.BlockSpec(shape, lambda i: (0,) * len(shape),
                                       pipeline_mode=pl.Buffered(1))
    rows = lambda w: pl.BlockSpec((tm, w), lambda i: (i, 0))
    return pl.pallas_call(
        _ffn_out_kernel,
        grid=(n_rows // tm,),
        in_specs=[rows(D_MODEL), rows(HG_W), rows(SB_W),
                  const((HG_W, D_MODEL)), const((SB_W, D_MODEL)),
                  const((1, D_MODEL)), const((D_MODEL, 2 * D_FF)),
                  const((D_FF, D_MODEL))],
        out_specs=rows(D_MODEL),
        out_shape=jax.ShapeDtypeStruct((n_rows, D_MODEL), F32),
        scratch_shapes=[pltpu.VMEM((tm, D_FF), BF16)],
        compiler_params=pltpu.CompilerParams(
            dimension_semantics=("arbitrary",), vmem_limit_bytes=VMEM_LIMIT),
        name="ffn_out",
    )(h1, ohg, osb, woa, wob, g2, w2i, w2o)


def kernel(x, meta_tokens, ffn1_norm, ffn1_w_in, ffn1_w_out, mix_norm, w_in,
           hgrn_lb_logits, hgrn_out_norm, sb_q_norm, sb_k_norm, w_out,
           ffn2_norm, ffn2_w_in, ffn2_w_out):
    b, seq, _ = x.shape
    assert b == 1 and seq % BLOCK == 0
    assert ffn1_norm.shape[0] == 1, "single layer"
    n_real_blk = seq // BLOCK
    tm = 512
    assert seq % tm == 0

    meta_blk = jnp.concatenate(
        [jnp.zeros((PAD, D_MODEL), x.dtype), meta_tokens.astype(x.dtype)], axis=0)
    hp = jnp.concatenate([x[0], meta_blk], axis=0)

    win = w_in[0].astype(BF16)
    o_hg, o_sq, o_sk, o_sv = 4 * HG_W, 4 * HG_W + SB_W, 4 * HG_W + 2 * SB_W, 4 * HG_W + 3 * SB_W
    whg, wq = win[:, :o_hg], win[:, o_hg:o_sq]
    wkt, wv = win[:, o_sq:o_sk].T, win[:, o_sk:o_sv]
    gk = jnp.tile(sb_k_norm[0], SB_HEADS).reshape(SB_W, 1)
    gq = jnp.tile(sb_q_norm[0], 2).reshape(1, BLOCK)

    h1, uhg, q, kt3, v = _ffn_in(
        hp, ffn1_norm, ffn1_w_in[0].astype(BF16), ffn1_w_out[0].astype(BF16),
        mix_norm, whg, wq, wkt, wv, gk, tm)
    ohg = _hgrn(uhg, hgrn_lb_logits, hgrn_out_norm, n_real_blk)
    osb = _stickbreak(q, kt3, v, gq, n_real_blk)
    wo = w_out[0].astype(BF16)
    out = _ffn_out(h1, ohg, osb, wo[:HG_W], wo[HG_W:], ffn2_norm,
                   ffn2_w_in[0].astype(BF16), ffn2_w_out[0].astype(BF16), seq, tm)
    return out[None]
```

```python
import functools

import numpy as np
import jax
import jax.numpy as jnp
from jax import lax
from jax.experimental import pallas as pl
from jax.experimental.pallas import tpu as pltpu

F32 = jnp.float32
BF16 = jnp.bfloat16

D_MODEL = 1024
N_META = 16
BLOCK = 128
PAD = (-N_META) % BLOCK
HG_HEADS = 4
HG_DK = 128
HG_DV = 128
HG_W = HG_HEADS * HG_DK
SB_HEADS = 8
SB_DH = 64
SB_W = SB_HEADS * SB_DH
SB_PAIRS = SB_HEADS // 2
D_FF = 2816
RMS_EPS = 1e-6
FF_CHUNK = 256
N_LEVELS = 7
EXP_ZERO_BELOW = -104.0
VMEM_LIMIT = 56 * 1024 * 1024
NT_DIMS = (((1,), (1,)), ((), ()))
TN_DIMS = (((0,), (0,)), ((), ()))


def _dot(a, b):
    return jnp.dot(a, b, preferred_element_type=F32)


def _dot_nt(a, b):
    return lax.dot_general(a, b, NT_DIMS, preferred_element_type=F32)


def _rms(x, gain):
    ms = jnp.mean(x * x, axis=-1, keepdims=True)
    return x * lax.rsqrt(ms + RMS_EPS) * gain


def _split2(x):
    hi = x.astype(BF16)
    lo = (x - hi.astype(F32)).astype(BF16)
    return hi, lo


def _swiglu(xn, w_in_ref, w_out_ref, act_ref):
    for c in range(D_FF // FF_CHUNK):
        lo, hi = c * FF_CHUNK, (c + 1) * FF_CHUNK
        g = _dot(xn, w_in_ref[:, lo:hi])
        u = _dot(xn, w_in_ref[:, D_FF + lo:D_FF + hi])
        act_ref[:, lo:hi] = (g * jax.nn.sigmoid(g) * u).astype(BF16)
    return _dot(act_ref[...], w_out_ref[...])


def _ffn_in_kernel(x_ref, meta_ref, g1_ref, w1i_ref, w1o_ref, gm_ref, whg_ref,
                   wq_ref, wkt_ref, wv_ref, gk_ref,
                   h1_ref, uhg_ref, q_ref, kt_ref, v_ref, act_ref):
    tm = x_ref.shape[0]
    is_meta = pl.program_id(0) == pl.num_programs(0) - 1
    h = jnp.where(is_meta, meta_ref[...], x_ref[...])
    xn = _rms(h, g1_ref[...]).astype(BF16)
    h1 = h + 0.5 * _swiglu(xn, w1i_ref, w1o_ref, act_ref)
    h1_ref[...] = h1
    xm = _rms(h1, gm_ref[...]).astype(BF16)
    uhg_ref[...] = _dot(xm, whg_ref[...])
    q_ref[...] = _dot(xm, wq_ref[...])
    v_ref[...] = _dot(xm, wv_ref[...]).astype(BF16)
    kt = _dot_nt(wkt_ref[...], xm)
    k3 = kt.reshape(SB_HEADS, SB_DH, tm)
    ms = jnp.mean(k3 * k3, axis=1, keepdims=True)
    kn = (k3 * lax.rsqrt(ms + RMS_EPS)).reshape(SB_W, tm) * gk_ref[...]
    kn = kn.astype(BF16)
    for t in range(tm // BLOCK):
        kt_ref[t] = kn[:, t * BLOCK:(t + 1) * BLOCK]


def _ffn_in(x2d, meta_tile, g1, w1i, w1o, gm, whg, wq, wkt, wv, gk, tm):
    n_real_tiles = x2d.shape[0] // tm
    lp = x2d.shape[0] + BLOCK
    n_blk = lp // BLOCK
    const = lambda shape: pl.BlockSpec(shape, lambda i: (0,) * len(shape),
                                       pipeline_mode=pl.Buffered(1))
    rows = lambda w: pl.BlockSpec((tm, w), lambda i: (i, 0))
    return pl.pallas_call(
        _ffn_in_kernel,
        grid=(n_real_tiles + 1,),
        in_specs=[pl.BlockSpec((tm, D_MODEL),
                               lambda i: (jnp.minimum(i, n_real_tiles - 1), 0)),
                  const((tm, D_MODEL)), const((1, D_MODEL)),
                  const((D_MODEL, 2 * D_FF)), const((D_FF, D_MODEL)),
                  const((1, D_MODEL)), const((D_MODEL, 4 * HG_W)),
                  const((D_MODEL, SB_W)), const((SB_W, D_MODEL)),
                  const((D_MODEL, SB_W)), const((SB_W, 1))],
        out_specs=[rows(D_MODEL), rows(4 * HG_W), rows(SB_W),
                   pl.BlockSpec((tm // BLOCK, SB_W, BLOCK), lambda i: (i, 0, 0)),
                   rows(SB_W)],
        out_shape=[jax.ShapeDtypeStruct((lp, D_MODEL), F32),
                   jax.ShapeDtypeStruct((lp, 4 * HG_W), F32),
                   jax.ShapeDtypeStruct((lp, SB_W), F32),
                   jax.ShapeDtypeStruct((n_blk, SB_W, BLOCK), BF16),
                   jax.ShapeDtypeStruct((lp, SB_W), BF16)],
        scratch_shapes=[pltpu.VMEM((tm, D_FF), BF16)],
        compiler_params=pltpu.CompilerParams(
            dimension_semantics=("arbitrary",), vmem_limit_bytes=VMEM_LIMIT),
        name="ffn_in",
    )(x2d, meta_tile, g1, w1i, w1o, gm, whg, wq, wkt, wv, gk)


def _hgrn_sum_matrix():
    t = np.arange(BLOCK)[:, None]
    j = np.arange(BLOCK)[None, :]
    mats = [(j <= t)]
    for lvl in range(N_LEVELS):
        c = 1 << lvl
        m = (t // (2 * c)) * (2 * c) + c
        upper = (t >= m) & (j >= m) & (j <= t)
        lower = (t < m) & (j > t) & (j <= m - 1)
        mats.append(upper | lower)
    m = np.concatenate(mats, axis=0).astype(np.float32)
    return np.concatenate([m, m], axis=1)


def _hgrn_kernel(q_ref, f_ref, i_ref, g_ref, lbl_ref, gain_ref, m_ref,
                 o_ref, st_ref):
    c = pl.program_id(0)
    heads = range(HG_HEADS)
    hs = lambda a, h: a[:, h * HG_DK:(h + 1) * HG_DK]

    @pl.when(c == 0)
    def _():
        st_ref[...] = jnp.zeros_like(st_ref)

    lg = lbl_ref[...]
    e = jnp.exp(lg - jnp.max(lg, axis=0, keepdims=True))
    lb = e[0:1] / jnp.sum(e, axis=0, keepdims=True)

    z = f_ref[...]
    valid = jnp.logical_or(
        c > 0, lax.broadcasted_iota(jnp.int32, z.shape, 0) >= PAD)
    ez = jnp.exp(-jnp.abs(z))
    rz = 1.0 / (1.0 + ez)
    erz = ez * rz
    pos = z >= 0.0
    f = lb + (1.0 - lb) * jnp.where(pos, rz, erz)
    logf = jnp.where(valid, jnp.log(f), 0.0)
    k = jnp.where(valid, (1.0 - lb) * jnp.where(pos, erz, rz), 0.0)
    qr = q_ref[...]
    q = qr * jax.nn.sigmoid(qr)
    v = i_ref[...].astype(BF16)

    x = _dot(m_ref[...], jnp.concatenate(_split2(logf), axis=0))
    bcum = x[0:BLOCK]
    b_last = bcum[BLOCK - 1:BLOCK]
    qe = (q * jnp.exp(bcum)).astype(BF16)
    kd = (k * jnp.exp(b_last - bcum)).astype(BF16)
    st_decay = jnp.exp(b_last)

    row = lax.broadcasted_iota(jnp.int32, (BLOCK, BLOCK), 0)
    col = lax.broadcasted_iota(jnp.int32, (BLOCK, BLOCK), 1)
    qb, kb = q.astype(BF16), k.astype(BF16)
    diag = row == col
    attn = [jnp.where(diag, _dot_nt(hs(qb, h), hs(kb, h)), 0.0) for h in heads]
    rowf = lax.broadcasted_iota(jnp.int32, z.shape, 0)
    for lvl in range(N_LEVELS):
        el = jnp.exp(x[(lvl + 1) * BLOCK:(lvl + 2) * BLOCK])
        is_q = ((rowf >> lvl) & 1) == 1
        ql = jnp.where(is_q, q * el, 0.0).astype(BF16)
        kl = jnp.where(is_q, 0.0, k * el).astype(BF16)
        same = (row >> (lvl + 1)) == (col >> (lvl + 1))
        for h in heads:
            al = _dot_nt(hs(ql, h), hs(kl, h))
            if lvl + 1 < N_LEVELS:
                al = jnp.where(same, al, 0.0)
            attn[h] = attn[h] + al

    o = []
    for h in heads:
        st = st_ref[h]
        oh = _dot(attn[h].astype(BF16), hs(v, h))
        oh = oh + _dot_nt(hs(qe, h), st.astype(BF16))
        st_ref[h] = st * hs(st_decay, h) + lax.dot_general(
            hs(v, h), hs(kd, h), TN_DIMS, preferred_element_type=F32)
        o.append(_rms(oh, hs(gain_ref[...], h)))
    gr = g_ref[...]
    o_ref[...] = (jnp.concatenate(o, axis=1) * (gr * jax.nn.sigmoid(gr))).astype(BF16)


def _hgrn(uhg, lb_logits, out_gain, n_real_blk):
    lp = uhg.shape[0]
    n_blk = lp // BLOCK
    phys = lambda c: (c + n_real_blk) % n_blk
    part = lambda p: pl.BlockSpec((BLOCK, HG_W), lambda c: (phys(c), p))
    msum = jnp.asarray(_hgrn_sum_matrix(), dtype=BF16)
    return pl.pallas_call(
        _hgrn_kernel,
        grid=(n_blk,),
        in_specs=[part(0), part(1), part(2), part(3),
                  pl.BlockSpec((2, HG_W), lambda c: (0, 0)),
                  pl.BlockSpec((1, HG_W), lambda c: (0, 0)),
                  pl.BlockSpec(msum.shape, lambda c: (0, 0))],
        out_specs=pl.BlockSpec((BLOCK, HG_W), lambda c: (phys(c), 0)),
        out_shape=jax.ShapeDtypeStruct((lp, HG_W), BF16),
        scratch_shapes=[pltpu.VMEM((HG_HEADS, HG_DV, HG_DK), F32)],
        compiler_params=pltpu.CompilerParams(
            dimension_semantics=("arbitrary",)),
        name="hgrn2",
    )(uhg, uhg, uhg, uhg, lb_logits, out_gain, msum)


def _sb_sum_matrix():
    j = np.arange(BLOCK)[:, None]
    s = np.arange(BLOCK)[None, :]
    w = np.concatenate([(j >= s), np.ones((BLOCK, BLOCK), bool)],
                       axis=1).astype(np.float32)
    return np.concatenate([w, w], axis=0)


def _softplus(z):
    return jnp.maximum(z, 0.0) + jnp.log(1.0 + jnp.exp(-jnp.abs(z)))


def _sb_kernel(q_ref, kt_ref, v_ref, gq_ref, w_ref, o_ref,
               qn_ref, acc_ref, crep_ref, *, n_real_blk):
    n_blk = kt_ref.shape[0]
    c = pl.program_id(0)
    row = lax.broadcasted_iota(jnp.int32, (BLOCK, BLOCK), 0)
    col = lax.broadcasted_iota(jnp.int32, (BLOCK, BLOCK), 1)
    low = col < SB_DH
    scale = 1.0 / np.sqrt(np.float32(SB_DH))
    pairs = range(SB_PAIRS)
    heads = range(SB_HEADS)
    blk = lambda i: slice(i * BLOCK, (i + 1) * BLOCK)

    gq = gq_ref[...]
    for p in pairs:
        q = q_ref[:, blk(p)]
        for a in range(2):
            own = low if a == 0 else jnp.logical_not(low)
            ms = jnp.sum(jnp.where(own, q * q, 0.0), axis=-1,
                         keepdims=True) * (1.0 / SB_DH)
            qn = jnp.where(own, q * lax.rsqrt(ms + RMS_EPS) * gq * scale, 0.0)
            qn_ref[p, blk(a), :] = qn.astype(BF16)

    def fold(j_top, kinds, fresh):
        n = len(kinds)
        pjs = [lax.rem(j_top - i + n_real_blk, n_blk) for i in range(n)]
        masks = []
        for i, kind in enumerate(kinds):
            if kind == "diag":
                masks.append(col < row)
            elif kind == "general":
                kpos = (j_top - i) * BLOCK + col
                masks.append(jnp.logical_and(kpos < c * BLOCK + row, kpos >= PAD))
            else:
                masks.append(None)
        z = {}
        for p in pairs:
            kt = jnp.concatenate([kt_ref[pjs[i], blk(p), :] for i in range(n)],
                                 axis=1)
            zz = _dot(qn_ref[p], kt)
            for a in range(2):
                for i in range(n):
                    z[2 * p + a, i] = zz[blk(a), blk(i)]
        order = [(hd, i) for hd in heads for i in range(n)]
        packed = []
        for hd, i in order:
            sp = _softplus(z[hd, i])
            if masks[i] is not None:
                sp = jnp.where(masks[i], sp, 0.0)
            packed.append(jnp.concatenate(_split2(sp), axis=1))
        r = _dot(jnp.concatenate(packed, axis=0), w_ref[...])
        w = {}
        cmax = None
        for hd in heads:
            crep = None if fresh else crep_ref[hd]
            for i in range(n):
                rh = r[blk(hd * n + i)]
                arg = z[hd, i] - rh[:, :BLOCK]
                wh = jnp.exp(arg if crep is None else arg + crep)
                if masks[i] is not None:
                    wh = jnp.where(masks[i], wh, 0.0)
                w[hd, i] = wh.astype(BF16)
                crep = -rh[:, BLOCK:] if crep is None else crep - rh[:, BLOCK:]
            crep_ref[hd] = crep
            cmax = crep if cmax is None else jnp.maximum(cmax, crep)
        for p in pairs:
            wp = jnp.concatenate(
                [jnp.concatenate([w[2 * p + a, i] for i in range(n)], axis=1)
                 for a in range(2)], axis=0)
            vt = jnp.concatenate(
                [v_ref[pl.ds(pl.multiple_of(pjs[i] * BLOCK, BLOCK), BLOCK), blk(p)]
                 for i in range(n)], axis=0)
            pv = _dot(wp, vt)
            pv = jnp.where(low, pv[:BLOCK], pv[BLOCK:])
            acc_ref[p] = pv if fresh else acc_ref[p] + pv
        return jnp.max(cmax)

    n_fast = 3
    fast = c >= n_fast
    cmax0 = lax.cond(fast,
                     lambda: fold(c, ("diag",) + (None,) * (n_fast - 1), True),
                     lambda: fold(c, ("general",), True))
    j0 = jnp.where(fast, c - n_fast, c - 1)

    def cond(carry):
        j, cmax = carry
        return jnp.logical_and(j >= 1, cmax > EXP_ZERO_BELOW)

    def body(carry):
        j, _ = carry
        return j - 1, fold(j, (None,), False)

    j_end, cmax_end = lax.while_loop(cond, body, (j0, cmax0))

    @pl.when(jnp.logical_and(j_end == 0, cmax_end > EXP_ZERO_BELOW))
    def _():
        fold(0, ("general",), False)

    for p in pairs:
        o_ref[:, blk(p)] = acc_ref[p].astype(BF16)


def _stickbreak(q, kt3, v, gq, n_real_blk):
    lp = q.shape[0]
    n_blk = lp // BLOCK
    phys = lambda c: (c + n_real_blk) % n_blk
    wsum = jnp.asarray(_sb_sum_matrix(), dtype=BF16)
    return pl.pallas_call(
        functools.partial(_sb_kernel, n_real_blk=n_real_blk),
        grid=(n_blk,),
        in_specs=[pl.BlockSpec((BLOCK, SB_W), lambda c: (phys(c), 0)),
                  pl.BlockSpec(kt3.shape, lambda c: (0, 0, 0),
                               pipeline_mode=pl.Buffered(1)),
                  pl.BlockSpec(v.shape, lambda c: (0, 0),
                               pipeline_mode=pl.Buffered(1)),
                  pl.BlockSpec((1, BLOCK), lambda c: (0, 0)),
                  pl.BlockSpec(wsum.shape, lambda c: (0, 0))],
        out_specs=pl.BlockSpec((BLOCK, SB_W), lambda c: (phys(c), 0)),
        out_shape=jax.ShapeDtypeStruct((lp, SB_W), BF16),
        scratch_shapes=[pltpu.VMEM((SB_PAIRS, 2 * BLOCK, BLOCK), BF16),
                        pltpu.VMEM((SB_PAIRS, BLOCK, BLOCK), F32),
                        pltpu.VMEM((SB_HEADS, BLOCK, BLOCK), F32)],
        compiler_params=pltpu.CompilerParams(
            dimension_semantics=("arbitrary",), vmem_limit_bytes=VMEM_LIMIT),
        name="stickbrk",
    )(q, kt3, v, gq, wsum)


def _ffn_out_kernel(h1_ref, ohg_ref, osb_ref, woa_ref, wob_ref, g2_ref,
                    w2i_ref, w2o_ref, out_ref, act_ref):
    h2 = (h1_ref[...] + _dot(ohg_ref[...], woa_ref[...])
          + _dot(osb_ref[...], wob_ref[...]))
    xn = _rms(h2, g2_ref[...]).astype(BF16)
    out_ref[...] = h2 + 0.5 * _swiglu(xn, w2i_ref, w2o_ref, act_ref)


def _ffn_out(h1, ohg, osb, woa, wob, g2, w2i, w2o, n_rows, tm):
    const = lambda shape: pl.BlockSpec(shape, lambda i: (0,) * len(shape),
                                       pipeline_mode=pl.Buffered(1))
    rows = lambda w: pl.BlockSpec((tm, w), lambda i: (i, 0))
    return pl.pallas_call(
        _ffn_out_kernel,
        grid=(n_rows // tm,),
        in_specs=[rows(D_MODEL), rows(HG_W), rows(SB_W),
                  const((HG_W, D_MODEL)), const((SB_W, D_MODEL)),
                  const((1, D_MODEL)), const((D_MODEL, 2 * D_FF)),
                  const((D_FF, D_MODEL))],
        out_specs=rows(D_MODEL),
        out_shape=jax.ShapeDtypeStruct((n_rows, D_MODEL), F32),
        scratch_shapes=[pltpu.VMEM((tm, D_FF), BF16)],
        compiler_params=pltpu.CompilerParams(
            dimension_semantics=("arbitrary",), vmem_limit_bytes=VMEM_LIMIT),
        name="ffn_out",
    )(h1, ohg, osb, woa, wob, g2, w2i, w2o)


def kernel(x, meta_tokens, ffn1_norm, ffn1_w_in, ffn1_w_out, mix_norm, w_in,
           hgrn_lb_logits, hgrn_out_norm, sb_q_norm, sb_k_norm, w_out,
           ffn2_norm, ffn2_w_in, ffn2_w_out):
    b, seq, _ = x.shape
    assert b == 1 and seq % BLOCK == 0
    assert ffn1_norm.shape[0] == 1, "single layer"
    n_real_blk = seq // BLOCK
    tm = 512
    assert seq % tm == 0

    meta_tile = jnp.zeros((tm, D_MODEL), x.dtype).at[PAD:BLOCK].set(
        meta_tokens.astype(x.dtype))

    win = w_in[0].astype(BF16)
    o_hg, o_sq, o_sk, o_sv = 4 * HG_W, 4 * HG_W + SB_W, 4 * HG_W + 2 * SB_W, 4 * HG_W + 3 * SB_W
    whg, wq = win[:, :o_hg], win[:, o_hg:o_sq]
    wkt, wv = win[:, o_sq:o_sk].T, win[:, o_sk:o_sv]
    gk = jnp.tile(sb_k_norm[0], SB_HEADS).reshape(SB_W, 1)
    gq = jnp.tile(sb_q_norm[0], 2).reshape(1, BLOCK)

    h1, uhg, q, kt3, v = _ffn_in(
        x[0], meta_tile, ffn1_norm, ffn1_w_in[0].astype(BF16), ffn1_w_out[0].astype(BF16),
        mix_norm, whg, wq, wkt, wv, gk, tm)
    ohg = _hgrn(uhg, hgrn_lb_logits, hgrn_out_norm, n_real_blk)
    osb = _stickbreak(q, kt3, v, gq, n_real_blk)
    wo = w_out[0].astype(BF16)
    out = _ffn_out(h1, ohg, osb, wo[:HG_W], wo[HG_W:], ffn2_norm,
                   ffn2_w_in[0].astype(BF16), ffn2_w_out[0].astype(BF16), seq, tm)
    return out[None]
```

```python
import functools

import numpy as np
import jax
import jax.numpy as jnp
from jax import lax
from jax.experimental import pallas as pl
from jax.experimental.pallas import tpu as pltpu

F32 = jnp.float32
BF16 = jnp.bfloat16

D_MODEL = 1024
N_META = 16
BLOCK = 128
PAD = (-N_META) % BLOCK
HG_HEADS = 4
HG_DK = 128
HG_DV = 128
HG_W = HG_HEADS * HG_DK
SB_HEADS = 8
SB_DH = 64
SB_W = SB_HEADS * SB_DH
SB_PAIRS = SB_HEADS // 2
D_FF = 2816
RMS_EPS = 1e-6
FF_CHUNK = 256
N_LEVELS = 7
SUBLANES = 8
LOG2E = 1.4426950408889634
EXP2_ZERO_BELOW = -150.0
VMEM_LIMIT = 56 * 1024 * 1024
NT_DIMS = (((1,), (1,)), ((), ()))
TN_DIMS = (((0,), (0,)), ((), ()))


def _dot(a, b):
    return jnp.dot(a, b, preferred_element_type=F32)


def _dot_nt(a, b):
    return lax.dot_general(a, b, NT_DIMS, preferred_element_type=F32)


def _rms(x, gain):
    ms = jnp.mean(x * x, axis=-1, keepdims=True)
    return x * lax.rsqrt(ms + RMS_EPS) * gain


def _split2(x):
    hi = x.astype(BF16)
    lo = (x - hi.astype(F32)).astype(BF16)
    return hi, lo


def _neg_abs(x):
    bits = lax.bitcast_convert_type(x, jnp.uint32) | jnp.uint32(0x80000000)
    return lax.bitcast_convert_type(bits, F32)


def _silu(x):
    return x * jax.nn.sigmoid(x)


def _blk(i):
    return slice(i * BLOCK, (i + 1) * BLOCK)


def _swiglu(xn, w_in_ref, w_out_ref, act_ref):
    for c in range(D_FF // FF_CHUNK):
        lo, hi = c * FF_CHUNK, (c + 1) * FF_CHUNK
        g = _dot(xn, w_in_ref[:, lo:hi])
        u = _dot(xn, w_in_ref[:, D_FF + lo:D_FF + hi])
        act_ref[:, lo:hi] = (_silu(g) * u).astype(BF16)
    return _dot(act_ref[...], w_out_ref[...])


def _ffn_in_kernel(x_ref, meta_ref, g1_ref, w1i_ref, w1o_ref, gm_ref, whg_ref,
                   wq_ref, wkt_ref, wv_ref, gk_ref, gq_ref, lbl_ref,
                   h1_ref, hq_ref, hlf_ref, hk_ref, hv_ref, hg_ref,
                   qn_ref, kt_ref, v_ref, act_ref):
    tm = x_ref.shape[0]
    is_meta = pl.program_id(0) == pl.num_programs(0) - 1
    h = jnp.where(is_meta, meta_ref[...], x_ref[...])
    xn = _rms(h, g1_ref[...]).astype(BF16)
    h1 = h + 0.5 * _swiglu(xn, w1i_ref, w1o_ref, act_ref)
    h1_ref[...] = h1
    xm = _rms(h1, gm_ref[...]).astype(BF16)

    uhg = _dot(xm, whg_ref[...])
    hq_ref[...] = _silu(uhg[:, 0:HG_W])
    hv_ref[...] = uhg[:, 2 * HG_W:3 * HG_W].astype(BF16)
    hg_ref[...] = _silu(uhg[:, 3 * HG_W:4 * HG_W])
    lg = lbl_ref[...]
    e = jnp.exp(lg - jnp.max(lg, axis=0, keepdims=True))
    lb = e[0:1] / jnp.sum(e, axis=0, keepdims=True)
    z = uhg[:, HG_W:2 * HG_W]
    ez = jnp.exp(_neg_abs(z))
    rz = 1.0 / (1.0 + ez)
    erz = ez * rz
    pos = z >= 0.0
    f = lb + (1.0 - lb) * jnp.where(pos, rz, erz)
    valid = jnp.logical_or(
        jnp.logical_not(is_meta),
        lax.broadcasted_iota(jnp.int32, z.shape, 0) >= PAD)
    hlf_ref[...] = jnp.where(valid, jnp.log2(f), 0.0)
    hk_ref[...] = jnp.where(valid, (1.0 - lb) * jnp.where(pos, erz, rz), 0.0)

    q = _dot(xm, wq_ref[...])
    low = lax.broadcasted_iota(jnp.int32, (tm, BLOCK), 1) < SB_DH
    qscale = gq_ref[...] * (LOG2E / np.sqrt(np.float32(SB_DH)))
    for p in range(SB_PAIRS):
        qp = q[:, _blk(p)]
        for a in range(2):
            own = low if a == 0 else jnp.logical_not(low)
            ms = jnp.sum(jnp.where(own, qp * qp, 0.0), axis=-1,
                         keepdims=True) * (1.0 / SB_DH)
            qn = jnp.where(own, qp * lax.rsqrt(ms + RMS_EPS) * qscale,
                           0.0).astype(BF16)
            for t in range(tm // BLOCK):
                qn_ref[t, p, _blk(a), :] = qn[_blk(t)]
    v_ref[...] = _dot(xm, wv_ref[...]).astype(BF16)
    kt = _dot_nt(wkt_ref[...], xm)
    k3 = kt.reshape(SB_HEADS, SB_DH, tm)
    ms = jnp.mean(k3 * k3, axis=1, keepdims=True)
    kn = (k3 * lax.rsqrt(ms + RMS_EPS)).reshape(SB_W, tm) * gk_ref[...]
    kn = kn.astype(BF16)
    for t in range(tm // BLOCK):
        kt_ref[t] = kn[:, _blk(t)]


def _ffn_in(x2d, meta_tile, g1, w1i, w1o, gm, whg, wq, wkt, wv, gk, gq, lbl, tm):
    n_real_tiles = x2d.shape[0] // tm
    lp = x2d.shape[0] + BLOCK
    n_blk = lp // BLOCK
    tb = tm // BLOCK
    const = lambda shape: pl.BlockSpec(shape, lambda i: (0,) * len(shape),
                                       pipeline_mode=pl.Buffered(1))
    rows = lambda w: pl.BlockSpec((tm, w), lambda i: (i, 0))
    seq = lambda w, dt: jax.ShapeDtypeStruct((lp, w), dt)
    return pl.pallas_call(
        _ffn_in_kernel,
        grid=(n_real_tiles + 1,),
        in_specs=[pl.BlockSpec((tm, D_MODEL),
                               lambda i: (jnp.minimum(i, n_real_tiles - 1), 0)),
                  const((tm, D_MODEL)), const((1, D_MODEL)),
                  const((D_MODEL, 2 * D_FF)), const((D_FF, D_MODEL)),
                  const((1, D_MODEL)), const((D_MODEL, 4 * HG_W)),
                  const((D_MODEL, SB_W)), const((SB_W, D_MODEL)),
                  const((D_MODEL, SB_W)), const((SB_W, 1)),
                  const((1, BLOCK)), const((2, HG_W))],
        out_specs=[rows(D_MODEL), rows(HG_W), rows(HG_W), rows(HG_W),
                   rows(HG_W), rows(HG_W),
                   pl.BlockSpec((tb, SB_PAIRS, 2 * BLOCK, BLOCK),
                                lambda i: (i, 0, 0, 0)),
                   pl.BlockSpec((tb, SB_W, BLOCK), lambda i: (i, 0, 0)),
                   rows(SB_W)],
        out_shape=[seq(D_MODEL, F32), seq(HG_W, F32), seq(HG_W, F32),
                   seq(HG_W, F32), seq(HG_W, BF16), seq(HG_W, F32),
                   jax.ShapeDtypeStruct((n_blk, SB_PAIRS, 2 * BLOCK, BLOCK), BF16),
                   jax.ShapeDtypeStruct((n_blk, SB_W, BLOCK), BF16),
                   seq(SB_W, BF16)],
        scratch_shapes=[pltpu.VMEM((tm, D_FF), BF16)],
        compiler_params=pltpu.CompilerParams(
            dimension_semantics=("arbitrary",), vmem_limit_bytes=VMEM_LIMIT),
        name="ffn_in",
    )(x2d, meta_tile, g1, w1i, w1o, gm, whg, wq, wkt, wv, gk, gq, lbl)


def _hgrn_sum_matrix():
    t = np.arange(BLOCK)[:, None]
    j = np.arange(BLOCK)[None, :]
    mats = [(j <= t)]
    for lvl in range(N_LEVELS):
        c = 1 << lvl
        m = (t // (2 * c)) * (2 * c) + c
        upper = (t >= m) & (j >= m) & (j <= t)
        lower = (t < m) & (j > t) & (j <= m - 1)
        mats.append(upper | lower)
    m = np.concatenate(mats, axis=0).astype(np.float32)
    return np.concatenate([m, m], axis=1)


def _hgrn_kernel(q_ref, lf_ref, k_ref, v_ref, g_ref, gain_ref, m_ref,
                 o_ref, st_ref):
    c = pl.program_id(0)
    heads = range(HG_HEADS)
    hs = lambda a, h: a[:, h * HG_DK:(h + 1) * HG_DK]

    @pl.when(c == 0)
    def _():
        st_ref[...] = jnp.zeros_like(st_ref)

    q = q_ref[...]
    k = k_ref[...]
    v = v_ref[...]
    x = _dot(m_ref[...], jnp.concatenate(_split2(lf_ref[...]), axis=0))
    bcum = x[0:BLOCK]
    b_last = bcum[BLOCK - 1:BLOCK]
    qe = (q * jnp.exp2(bcum)).astype(BF16)
    kd = (k * jnp.exp2(b_last - bcum)).astype(BF16)
    st_decay = jnp.exp2(b_last)

    row = lax.broadcasted_iota(jnp.int32, (BLOCK, BLOCK), 0)
    col = lax.broadcasted_iota(jnp.int32, (BLOCK, BLOCK), 1)
    qb, kb = q.astype(BF16), k.astype(BF16)
    diag = row == col
    attn = [jnp.where(diag, _dot_nt(hs(qb, h), hs(kb, h)), 0.0) for h in heads]
    rowf = lax.broadcasted_iota(jnp.int32, q.shape, 0)
    for lvl in range(N_LEVELS):
        half = 1 << lvl
        el = jnp.exp2(x[(lvl + 1) * BLOCK:(lvl + 2) * BLOCK])
        if half >= SUBLANES:
            qparts, kparts = [], []
            for b in range(BLOCK // half):
                rows = slice(b * half, (b + 1) * half)
                zero = jnp.zeros((half, q.shape[1]), F32)
                if b % 2 == 1:
                    qparts.append(q[rows] * el[rows])
                    kparts.append(zero)
                else:
                    qparts.append(zero)
                    kparts.append(k[rows] * el[rows])
            ql = jnp.concatenate(qparts, axis=0).astype(BF16)
            kl = jnp.concatenate(kparts, axis=0).astype(BF16)
        else:
            is_q = ((rowf >> lvl) & 1) == 1
            ql = jnp.where(is_q, q * el, 0.0).astype(BF16)
            kl = jnp.where(is_q, 0.0, k * el).astype(BF16)
        same = (row >> (lvl + 1)) == (col >> (lvl + 1))
        for h in heads:
            al = _dot_nt(hs(ql, h), hs(kl, h))
            if lvl + 1 < N_LEVELS:
                al = jnp.where(same, al, 0.0)
            attn[h] = attn[h] + al

    o = []
    for h in heads:
        st = st_ref[h]
        oh = _dot(attn[h].astype(BF16), hs(v, h))
        oh = oh + _dot_nt(hs(qe, h), st.astype(BF16))
        st_ref[h] = st * hs(st_decay, h) + lax.dot_general(
            hs(v, h), hs(kd, h), TN_DIMS, preferred_element_type=F32)
        o.append(_rms(oh, hs(gain_ref[...], h)))
    o_ref[...] = (jnp.concatenate(o, axis=1) * g_ref[...]).astype(BF16)


def _hgrn(hq, hlf, hk, hv, hg, out_gain, n_real_blk):
    lp = hq.shape[0]
    n_blk = lp // BLOCK
    phys = lambda c: (c + n_real_blk) % n_blk
    slab = pl.BlockSpec((BLOCK, HG_W), lambda c: (phys(c), 0))
    msum = jnp.asarray(_hgrn_sum_matrix(), dtype=BF16)
    return pl.pallas_call(
        _hgrn_kernel,
        grid=(n_blk,),
        in_specs=[slab, slab, slab, slab, slab,
                  pl.BlockSpec((1, HG_W), lambda c: (0, 0)),
                  pl.BlockSpec(msum.shape, lambda c: (0, 0))],
        out_specs=slab,
        out_shape=jax.ShapeDtypeStruct((lp, HG_W), BF16),
        scratch_shapes=[pltpu.VMEM((HG_HEADS, HG_DV, HG_DK), F32)],
        compiler_params=pltpu.CompilerParams(
            dimension_semantics=("arbitrary",)),
        name="hgrn2",
    )(hq, hlf, hk, hv, hg, out_gain, msum)


def _sb_sum_matrix():
    j = np.arange(BLOCK)[:, None]
    s = np.arange(BLOCK)[None, :]
    w = np.concatenate([(j >= s), np.ones((BLOCK, BLOCK), bool)],
                       axis=1).astype(np.float32)
    return np.concatenate([w, w], axis=0)


def _softplus2(z):
    return jnp.maximum(z, 0.0) + jnp.log2(1.0 + jnp.exp2(_neg_abs(z)))


def _sb_kernel(qn_ref, kt_ref, v_ref, w_ref, o_ref, acc_ref, crep_ref,
               *, n_real_blk):
    n_blk = kt_ref.shape[0]
    c = pl.program_id(0)
    row = lax.broadcasted_iota(jnp.int32, (BLOCK, BLOCK), 0)
    col = lax.broadcasted_iota(jnp.int32, (BLOCK, BLOCK), 1)
    low = col < SB_DH
    pairs = range(SB_PAIRS)
    heads = range(SB_HEADS)

    def fold(j_top, kinds, fresh):
        n = len(kinds)
        pjs = [lax.rem(j_top - i + n_real_blk, n_blk) for i in range(n)]
        masks = []
        for i, kind in enumerate(kinds):
            if kind == "diag":
                masks.append(col < row)
            elif kind == "general":
                kpos = (j_top - i) * BLOCK + col
                masks.append(jnp.logical_and(kpos < c * BLOCK + row, kpos >= PAD))
            else:
                masks.append(None)
        z = {}
        for p in pairs:
            kt = jnp.concatenate([kt_ref[pjs[i], _blk(p), :] for i in range(n)],
                                 axis=1)
            zz = _dot(qn_ref[0, p], kt)
            for a in range(2):
                for i in range(n):
                    z[2 * p + a, i] = zz[_blk(a), _blk(i)]
        order = [(hd, i) for hd in heads for i in range(n)]
        packed = []
        for hd, i in order:
            sp = _softplus2(z[hd, i])
            if masks[i] is not None:
                sp = jnp.where(masks[i], sp, 0.0)
            packed.append(jnp.concatenate(_split2(sp), axis=1))
        r = _dot(jnp.concatenate(packed, axis=0), w_ref[...])
        w = {}
        cmax = None
        for hd in heads:
            crep = None if fresh else crep_ref[hd]
            for i in range(n):
                rh = r[_blk(hd * n + i)]
                arg = z[hd, i] - rh[:, :BLOCK]
                wh = jnp.exp2(arg if crep is None else arg + crep)
                if masks[i] is not None:
                    wh = jnp.where(masks[i], wh, 0.0)
                w[hd, i] = wh.astype(BF16)
                crep = -rh[:, BLOCK:] if crep is None else crep - rh[:, BLOCK:]
            crep_ref[hd] = crep
            cmax = crep if cmax is None else jnp.maximum(cmax, crep)
        for p in pairs:
            wp = jnp.concatenate(
                [jnp.concatenate([w[2 * p + a, i] for i in range(n)], axis=1)
                 for a in range(2)], axis=0)
            vt = jnp.concatenate(
                [v_ref[pl.ds(pl.multiple_of(pjs[i] * BLOCK, BLOCK), BLOCK), _blk(p)]
                 for i in range(n)], axis=0)
            pv = _dot(wp, vt)
            pv = jnp.where(low, pv[:BLOCK], pv[BLOCK:])
            acc_ref[p] = pv if fresh else acc_ref[p] + pv
        return jnp.max(cmax)

    n_fast = 3
    fast = c >= n_fast
    cmax0 = lax.cond(fast,
                     lambda: fold(c, ("diag",) + (None,) * (n_fast - 1), True),
                     lambda: fold(c, ("general",), True))
    j0 = jnp.where(fast, c - n_fast, c - 1)

    def cond(carry):
        j, cmax = carry
        return jnp.logical_and(j >= 1, cmax > EXP2_ZERO_BELOW)

    def body(carry):
        j, _ = carry
        return j - 1, fold(j, (None,), False)

    j_end, cmax_end = lax.while_loop(cond, body, (j0, cmax0))

    @pl.when(jnp.logical_and(j_end == 0, cmax_end > EXP2_ZERO_BELOW))
    def _():
        fold(0, ("general",), False)

    for p in pairs:
        o_ref[:, _blk(p)] = acc_ref[p].astype(BF16)


def _stickbreak(qn4, kt3, v, n_real_blk):
    lp = v.shape[0]
    n_blk = lp // BLOCK
    phys = lambda c: (c + n_real_blk) % n_blk
    wsum = jnp.asarray(_sb_sum_matrix(), dtype=BF16)
    return pl.pallas_call(
        functools.partial(_sb_kernel, n_real_blk=n_real_blk),
        grid=(n_blk,),
        in_specs=[pl.BlockSpec((1,) + qn4.shape[1:], lambda c: (phys(c), 0, 0, 0)),
                  pl.BlockSpec(kt3.shape, lambda c: (0, 0, 0),
                               pipeline_mode=pl.Buffered(1)),
                  pl.BlockSpec(v.shape, lambda c: (0, 0),
                               pipeline_mode=pl.Buffered(1)),
                  pl.BlockSpec(wsum.shape, lambda c: (0, 0))],
        out_specs=pl.BlockSpec((BLOCK, SB_W), lambda c: (phys(c), 0)),
        out_shape=jax.ShapeDtypeStruct((lp, SB_W), BF16),
        scratch_shapes=[pltpu.VMEM((SB_PAIRS, BLOCK, BLOCK), F32),
                        pltpu.VMEM((SB_HEADS, BLOCK, BLOCK), F32)],
        compiler_params=pltpu.CompilerParams(
            dimension_semantics=("arbitrary",), vmem_limit_bytes=VMEM_LIMIT),
        name="stickbrk",
    )(qn4, kt3, v, wsum)


def _ffn_out_kernel(h1_ref, ohg_ref, osb_ref, woa_ref, wob_ref, g2_ref,
                    w2i_ref, w2o_ref, out_ref, act_ref):
    h2 = (h1_ref[...] + _dot(ohg_ref[...], woa_ref[...])
          + _dot(osb_ref[...], wob_ref[...]))
    xn = _rms(h2, g2_ref[...]).astype(BF16)
    out_ref[...] = h2 + 0.5 * _swiglu(xn, w2i_ref, w2o_ref, act_ref)


def _ffn_out(h1, ohg, osb, woa, wob, g2, w2i, w2o, n_rows, tm):
    const = lambda shape: pl.BlockSpec(shape, lambda i: (0,) * len(shape),
                                       pipeline_mode=pl.Buffered(1))
    rows = lambda w: pl.BlockSpec((tm, w), lambda i: (i, 0))
    return pl.pallas_call(
        _ffn_out_kernel,
        grid=(n_rows // tm,),
        in_specs=[rows(D_MODEL), rows(HG_W), rows(SB_W),
                  const((HG_W, D_MODEL)), const((SB_W, D_MODEL)),
                  const((1, D_MODEL)), const((D_MODEL, 2 * D_FF)),
                  const((D_FF, D_MODEL))],
        out_specs=rows(D_MODEL),
        out_shape=jax.ShapeDtypeStruct((n_rows, D_MODEL), F32),
        scratch_shapes=[pltpu.VMEM((tm, D_FF), BF16)],
        compiler_params=pltpu.CompilerParams(
            dimension_semantics=("arbitrary",), vmem_limit_bytes=VMEM_LIMIT),
        name="ffn_out",
    )(h1, ohg, osb, woa, wob, g2, w2i, w2o)


def kernel(x, meta_tokens, ffn1_norm, ffn1_w_in, ffn1_w_out, mix_norm, w_in,
           hgrn_lb_logits, hgrn_out_norm, sb_q_norm, sb_k_norm, w_out,
           ffn2_norm, ffn2_w_in, ffn2_w_out):
    b, seq, _ = x.shape
    assert b == 1 and seq % BLOCK == 0
    assert ffn1_norm.shape[0] == 1, "single layer"
    n_real_blk = seq // BLOCK
    tm = 512
    assert seq % tm == 0

    meta_tile = jnp.zeros((tm, D_MODEL), x.dtype).at[PAD:BLOCK].set(
        meta_tokens.astype(x.dtype))

    win = w_in[0].astype(BF16)
    o_hg, o_sq, o_sk, o_sv = 4 * HG_W, 4 * HG_W + SB_W, 4 * HG_W + 2 * SB_W, 4 * HG_W + 3 * SB_W
    whg, wq = win[:, :o_hg], win[:, o_hg:o_sq]
    wkt, wv = win[:, o_sq:o_sk].T, win[:, o_sk:o_sv]
    gk = jnp.tile(sb_k_norm[0], SB_HEADS).reshape(SB_W, 1)
    gq = jnp.tile(sb_q_norm[0], 2).reshape(1, BLOCK)

    h1, hq, hlf, hk, hv, hg, qn4, kt3, v = _ffn_in(
        x[0], meta_tile, ffn1_norm, ffn1_w_in[0].astype(BF16),
        ffn1_w_out[0].astype(BF16), mix_norm, whg, wq, wkt, wv, gk, gq,
        hgrn_lb_logits, tm)
    ohg = _hgrn(hq, hlf, hk, hv, hg, hgrn_out_norm, n_real_blk)
    osb = _stickbreak(qn4, kt3, v, n_real_blk)
    wo = w_out[0].astype(BF16)
    out = _ffn_out(h1, ohg, osb, wo[:HG_W], wo[HG_W:], ffn2_norm,
                   ffn2_w_in[0].astype(BF16), ffn2_w_out[0].astype(BF16), seq, tm)
    return out[None]
```

```python
import functools

import numpy as np
import jax
import jax.numpy as jnp
from jax import lax
from jax.experimental import pallas as pl
from jax.experimental.pallas import tpu as pltpu

F32 = jnp.float32
BF16 = jnp.bfloat16

D_MODEL = 1024
N_META = 16
BLOCK = 128
PAD = (-N_META) % BLOCK
HG_HEADS = 4
HG_DK = 128
HG_DV = 128
HG_W = HG_HEADS * HG_DK
SB_HEADS = 8
SB_DH = 64
SB_W = SB_HEADS * SB_DH
SB_PAIRS = SB_HEADS // 2
D_FF = 2816
RMS_EPS = 1e-6
FF_CHUNK = 256
N_LEVELS = 7
SUBLANES = 8
LOG2E = 1.4426950408889634
EXP2_ZERO_BELOW = -150.0
FAR_ROWS = 48
VMEM_LIMIT = 56 * 1024 * 1024
NT_DIMS = (((1,), (1,)), ((), ()))
TN_DIMS = (((0,), (0,)), ((), ()))


def _dot(a, b):
    return jnp.dot(a, b, preferred_element_type=F32)


def _dot_nt(a, b):
    return lax.dot_general(a, b, NT_DIMS, preferred_element_type=F32)


def _rms(x, gain):
    ms = jnp.mean(x * x, axis=-1, keepdims=True)
    return x * lax.rsqrt(ms + RMS_EPS) * gain


def _split2(x):
    hi = x.astype(BF16)
    lo = (x - hi.astype(F32)).astype(BF16)
    return hi, lo


def _neg_abs(x):
    return -jnp.abs(x)


def _silu(x):
    return x * jax.nn.sigmoid(x)


def _blk(i):
    return slice(i * BLOCK, (i + 1) * BLOCK)


def _swiglu(xn, w_in_ref, w_out_ref, act_ref):
    for c in range(D_FF // FF_CHUNK):
        lo, hi = c * FF_CHUNK, (c + 1) * FF_CHUNK
        g = _dot(xn, w_in_ref[:, lo:hi])
        u = _dot(xn, w_in_ref[:, D_FF + lo:D_FF + hi])
        act_ref[:, lo:hi] = (_silu(g) * u).astype(BF16)
    return _dot(act_ref[...], w_out_ref[...])


def _ffn_in_kernel(x_ref, meta_ref, g1_ref, w1i_ref, w1o_ref, gm_ref, whg_ref,
                   wq_ref, wkt_ref, wv_ref, gk_ref, gq_ref, lbl_ref,
                   h1_ref, hq_ref, hlf_ref, hk_ref, hv_ref, hg_ref,
                   qn_ref, kt_ref, v_ref, act_ref):
    tm = x_ref.shape[0]
    is_meta = pl.program_id(0) == pl.num_programs(0) - 1
    h = jnp.where(is_meta, meta_ref[...], x_ref[...])
    xn = _rms(h, g1_ref[...]).astype(BF16)
    h1 = h + 0.5 * _swiglu(xn, w1i_ref, w1o_ref, act_ref)
    h1_ref[...] = h1
    xm = _rms(h1, gm_ref[...]).astype(BF16)

    uhg = _dot(xm, whg_ref[...])
    hq_ref[...] = _silu(uhg[:, 0:HG_W])
    hv_ref[...] = uhg[:, 2 * HG_W:3 * HG_W].astype(BF16)
    hg_ref[...] = _silu(uhg[:, 3 * HG_W:4 * HG_W])
    lg = lbl_ref[...]
    e = jnp.exp(lg - jnp.max(lg, axis=0, keepdims=True))
    lb = e[0:1] / jnp.sum(e, axis=0, keepdims=True)
    z = uhg[:, HG_W:2 * HG_W]
    ez = jnp.exp(_neg_abs(z))
    rz = 1.0 / (1.0 + ez)
    erz = ez * rz
    pos = z >= 0.0
    f = lb + (1.0 - lb) * jnp.where(pos, rz, erz)
    valid = jnp.logical_or(
        jnp.logical_not(is_meta),
        lax.broadcasted_iota(jnp.int32, z.shape, 0) >= PAD)
    hlf_ref[...] = jnp.where(valid, jnp.log2(f), 0.0)
    hk_ref[...] = jnp.where(valid, (1.0 - lb) * jnp.where(pos, erz, rz), 0.0)

    q = _dot(xm, wq_ref[...])
    low = lax.broadcasted_iota(jnp.int32, (tm, BLOCK), 1) < SB_DH
    qscale = gq_ref[...] * (LOG2E / np.sqrt(np.float32(SB_DH)))
    for p in range(SB_PAIRS):
        qp = q[:, _blk(p)]
        for a in range(2):
            own = low if a == 0 else jnp.logical_not(low)
            ms = jnp.sum(jnp.where(own, qp * qp, 0.0), axis=-1,
                         keepdims=True) * (1.0 / SB_DH)
            qn = jnp.where(own, qp * lax.rsqrt(ms + RMS_EPS) * qscale,
                           0.0).astype(BF16)
            for t in range(tm // BLOCK):
                qn_ref[t, p, _blk(a), :] = qn[_blk(t)]
    v_ref[...] = _dot(xm, wv_ref[...]).astype(BF16)
    kt = _dot_nt(wkt_ref[...], xm)
    k3 = kt.reshape(SB_HEADS, SB_DH, tm)
    ms = jnp.mean(k3 * k3, axis=1, keepdims=True)
    kn = (k3 * lax.rsqrt(ms + RMS_EPS)).reshape(SB_W, tm) * gk_ref[...]
    kn = kn.astype(BF16)
    for t in range(tm // BLOCK):
        kt_ref[t] = kn[:, _blk(t)]


def _ffn_in(x2d, meta_tile, g1, w1i, w1o, gm, whg, wq, wkt, wv, gk, gq, lbl, tm):
    n_real_tiles = x2d.shape[0] // tm
    lp = x2d.shape[0] + BLOCK
    n_blk = lp // BLOCK
    tb = tm // BLOCK
    const = lambda shape: pl.BlockSpec(shape, lambda i: (0,) * len(shape),
                                       pipeline_mode=pl.Buffered(1))
    rows = lambda w: pl.BlockSpec((tm, w), lambda i: (i, 0))
    seq = lambda w, dt: jax.ShapeDtypeStruct((lp, w), dt)
    return pl.pallas_call(
        _ffn_in_kernel,
        grid=(n_real_tiles + 1,),
        in_specs=[pl.BlockSpec((tm, D_MODEL),
                               lambda i: (jnp.minimum(i, n_real_tiles - 1), 0)),
                  const((tm, D_MODEL)), const((1, D_MODEL)),
                  const((D_MODEL, 2 * D_FF)), const((D_FF, D_MODEL)),
                  const((1, D_MODEL)), const((D_MODEL, 4 * HG_W)),
                  const((D_MODEL, SB_W)), const((SB_W, D_MODEL)),
                  const((D_MODEL, SB_W)), const((SB_W, 1)),
                  const((1, BLOCK)), const((2, HG_W))],
        out_specs=[rows(D_MODEL), rows(HG_W), rows(HG_W), rows(HG_W),
                   rows(HG_W), rows(HG_W),
                   pl.BlockSpec((tb, SB_PAIRS, 2 * BLOCK, BLOCK),
                                lambda i: (i, 0, 0, 0)),
                   pl.BlockSpec((tb, SB_W, BLOCK), lambda i: (i, 0, 0)),
                   rows(SB_W)],
        out_shape=[seq(D_MODEL, F32), seq(HG_W, F32), seq(HG_W, F32),
                   seq(HG_W, F32), seq(HG_W, BF16), seq(HG_W, F32),
                   jax.ShapeDtypeStruct((n_blk, SB_PAIRS, 2 * BLOCK, BLOCK), BF16),
                   jax.ShapeDtypeStruct((n_blk, SB_W, BLOCK), BF16),
                   seq(SB_W, BF16)],
        scratch_shapes=[pltpu.VMEM((tm, D_FF), BF16)],
        compiler_params=pltpu.CompilerParams(
            dimension_semantics=("arbitrary",), vmem_limit_bytes=VMEM_LIMIT),
        name="ffn_in",
    )(x2d, meta_tile, g1, w1i, w1o, gm, whg, wq, wkt, wv, gk, gq, lbl)


def _hgrn_sum_matrix():
    t = np.arange(BLOCK)[:, None]
    j = np.arange(BLOCK)[None, :]
    mats = [(j <= t)]
    for lvl in range(N_LEVELS):
        c = 1 << lvl
        m = (t // (2 * c)) * (2 * c) + c
        upper = (t >= m) & (j >= m) & (j <= t)
        lower = (t < m) & (j > t) & (j <= m - 1)
        mats.append(upper | lower)
    m = np.concatenate(mats, axis=0).astype(np.float32)
    return np.concatenate([m, m], axis=1)


def _hgrn_stages(q_ref, lf_ref, k_ref, v_ref, g_ref, gain_ref, m_ref,
                 o_ref, st_ref):
    heads = range(HG_HEADS)
    hs = lambda a, h: a[:, h * HG_DK:(h + 1) * HG_DK]

    q = q_ref[...]
    k = k_ref[...]
    v = v_ref[...]
    x = _dot(m_ref[...], jnp.concatenate(_split2(lf_ref[...]), axis=0))
    yield
    bcum = x[0:BLOCK]
    b_last = bcum[BLOCK - 1:BLOCK]
    qe = (q * jnp.exp2(bcum)).astype(BF16)
    kd = (k * jnp.exp2(b_last - bcum)).astype(BF16)
    st_decay = jnp.exp2(b_last)

    row = lax.broadcasted_iota(jnp.int32, (BLOCK, BLOCK), 0)
    col = lax.broadcasted_iota(jnp.int32, (BLOCK, BLOCK), 1)
    qb, kb = q.astype(BF16), k.astype(BF16)
    diag = row == col
    attn = [jnp.where(diag, _dot_nt(hs(qb, h), hs(kb, h)), 0.0) for h in heads]
    rowf = lax.broadcasted_iota(jnp.int32, q.shape, 0)
    for lvl in range(N_LEVELS):
        if lvl % 2 == 1:
            yield
        half = 1 << lvl
        el = jnp.exp2(x[(lvl + 1) * BLOCK:(lvl + 2) * BLOCK])
        if half >= SUBLANES:
            qparts, kparts = [], []
            for b in range(BLOCK // half):
                rows = slice(b * half, (b + 1) * half)
                zero = jnp.zeros((half, q.shape[1]), F32)
                if b % 2 == 1:
                    qparts.append(q[rows] * el[rows])
                    kparts.append(zero)
                else:
                    qparts.append(zero)
                    kparts.append(k[rows] * el[rows])
            ql = jnp.concatenate(qparts, axis=0).astype(BF16)
            kl = jnp.concatenate(kparts, axis=0).astype(BF16)
        else:
            is_q = ((rowf >> lvl) & 1) == 1
            ql = jnp.where(is_q, q * el, 0.0).astype(BF16)
            kl = jnp.where(is_q, 0.0, k * el).astype(BF16)
        same = (row >> (lvl + 1)) == (col >> (lvl + 1))
        for h in heads:
            al = _dot_nt(hs(ql, h), hs(kl, h))
            if lvl + 1 < N_LEVELS:
                al = jnp.where(same, al, 0.0)
            attn[h] = attn[h] + al
    yield

    o = []
    for h in heads:
        st = st_ref[h]
        oh = _dot(attn[h].astype(BF16), hs(v, h))
        oh = oh + _dot_nt(hs(qe, h), st.astype(BF16))
        st_ref[h] = st * hs(st_decay, h) + lax.dot_general(
            hs(v, h), hs(kd, h), TN_DIMS, preferred_element_type=F32)
        o.append(oh)
    yield
    o = jnp.concatenate([_rms(o[h], hs(gain_ref[...], h)) for h in heads], axis=1)
    o_ref[:, 0:HG_W] = (o * g_ref[...]).astype(BF16)


def _interleave(*stage_lists):
    results = [None] * len(stage_lists)
    active = dict(enumerate(stage_lists))
    while active:
        for i, g in list(active.items()):
            try:
                next(g)
            except StopIteration as stop:
                results[i] = stop.value
                del active[i]
    return results


def _sb_sum_matrix():
    j = np.arange(BLOCK)[:, None]
    s = np.arange(BLOCK)[None, :]
    w = np.concatenate([(j >= s), np.ones((BLOCK, BLOCK), bool)],
                       axis=1).astype(np.float32)
    return np.concatenate([w, w], axis=0)


def _softplus2(z):
    return jnp.maximum(z, 0.0) + jnp.log2(1.0 + jnp.exp2(_neg_abs(z)))


def _mixer_kernel(hq_ref, hlf_ref, hk_ref, hv_ref, hg_ref, gain_ref, m_ref,
                  qn_ref, kt_ref, v_ref, w_ref, o_ref,
                  st_ref, acc_ref, crep_ref, *, n_real_blk):
    n_blk = kt_ref.shape[0]
    c = pl.program_id(0)

    @pl.when(c == 0)
    def _():
        st_ref[...] = jnp.zeros_like(st_ref)

    def hgrn():
        return _hgrn_stages(hq_ref, hlf_ref, hk_ref, hv_ref, hg_ref, gain_ref,
                            m_ref, o_ref, st_ref)

    row = lax.broadcasted_iota(jnp.int32, (BLOCK, BLOCK), 0)
    col = lax.broadcasted_iota(jnp.int32, (BLOCK, BLOCK), 1)
    low = col < SB_DH
    pairs = range(SB_PAIRS)
    heads = range(SB_HEADS)

    def fold_stages(j_top, tiles, fresh):
        n = len(tiles)
        nrows = [t[1] for t in tiles]
        assert all(nr == BLOCK for nr in nrows[:-1])
        pjs = [lax.rem(j_top - i + n_real_blk, n_blk) for i in range(n)]
        masks = []
        for i, (kind, nr) in enumerate(tiles):
            assert kind is None or nr == BLOCK
            if kind == "diag":
                m = col < row
            elif kind == "general":
                kpos = (j_top - i) * BLOCK + col
                m = jnp.logical_and(kpos < c * BLOCK + row, kpos >= PAD)
            elif kind == "late_rows":
                m = row >= FAR_ROWS
            else:
                m = None
            masks.append(m)
        z = {}
        for p in pairs:
            kt = jnp.concatenate([kt_ref[pjs[i], _blk(p), :] for i in range(n)],
                                 axis=1)
            zz = _dot(qn_ref[0, p], kt)
            for a in range(2):
                for i in range(n):
                    z[2 * p + a, i] = zz[_blk(a), _blk(i)][:nrows[i]]
        yield
        order = [(hd, i) for hd in heads for i in range(n)]
        packed = []
        for hd, i in order:
            sp = _softplus2(z[hd, i])
            if masks[i] is not None:
                sp = jnp.where(masks[i], sp, 0.0)
            packed.append(jnp.concatenate(_split2(sp), axis=1))
        r = _dot(jnp.concatenate(packed, axis=0), w_ref[...])
        offs = np.cumsum([0] + [nrows[i] for _, i in order])
        yield
        w = {}
        cmax = cmax_rest = None
        for hd in heads:
            crep = None if fresh else crep_ref[hd]
            for i in range(n):
                nr = nrows[i]
                o0 = int(offs[hd * n + i])
                rh = r[o0:o0 + nr]
                arg = z[hd, i] - rh[:, :BLOCK]
                wh = jnp.exp2(arg if crep is None else arg + crep[:nr])
                if masks[i] is not None:
                    wh = jnp.where(masks[i], wh, 0.0)
                w[hd, i] = wh.astype(BF16)
                if crep is None:
                    crep = -rh[:, BLOCK:]
                elif nr == BLOCK:
                    crep = crep - rh[:, BLOCK:]
                else:
                    rest = crep[nr:]
                    cmax_rest = (rest if cmax_rest is None
                                 else jnp.maximum(cmax_rest, rest))
                    crep = jnp.concatenate([crep[:nr] - rh[:, BLOCK:], rest], axis=0)
            crep_ref[hd] = crep
            cmax = crep if cmax is None else jnp.maximum(cmax, crep)
        yield
        full = [i for i in range(n) if nrows[i] == BLOCK]
        for p in pairs:
            wp = jnp.concatenate(
                [jnp.concatenate([w[2 * p + a, i] for i in full], axis=1)
                 for a in range(2)], axis=0)
            vtile = lambda i: v_ref[
                pl.ds(pl.multiple_of(pjs[i] * BLOCK, BLOCK), BLOCK), _blk(p)]
            pv = _dot(wp, jnp.concatenate([vtile(i) for i in full], axis=0))
            pv = jnp.where(low, pv[:BLOCK], pv[BLOCK:])
            if nrows[-1] < BLOCK:
                nr = nrows[-1]
                ps = _dot(jnp.concatenate([w[2 * p, n - 1], w[2 * p + 1, n - 1]],
                                          axis=0), vtile(n - 1))
                low_nr = lax.broadcasted_iota(jnp.int32, (nr, BLOCK), 1) < SB_DH
                ps = jnp.where(low_nr, ps[:nr], ps[nr:])
                pv = jnp.concatenate([pv[:nr] + ps, pv[nr:]], axis=0)
            acc_ref[p] = pv if fresh else acc_ref[p] + pv
        return (jnp.max(cmax), jnp.float32(-jnp.inf) if cmax_rest is None
                else jnp.max(cmax_rest))

    n_fast = 3
    fast = c >= n_fast

    def first_fold(tiles):
        return lambda: _interleave(fold_stages(c, tiles, True), hgrn())[0]

    def fold(j_top, kind):
        return _interleave(fold_stages(j_top, ((kind, BLOCK),), False))[0][0]

    cmax0, cmax_rest = lax.cond(
        fast,
        first_fold((("diag", BLOCK), (None, BLOCK), (None, FAR_ROWS))),
        first_fold((("general", BLOCK),)))

    cmax0 = lax.cond(cmax_rest > EXP2_ZERO_BELOW,
                     lambda: fold(c - (n_fast - 1), "late_rows"),
                     lambda: cmax0)
    j0 = jnp.where(fast, c - n_fast, c - 1)

    def cond(carry):
        j, cmax = carry
        return jnp.logical_and(j >= 1, cmax > EXP2_ZERO_BELOW)

    def body(carry):
        j, _ = carry
        return j - 1, fold(j, None)

    j_end, cmax_end = lax.while_loop(cond, body, (j0, cmax0))

    @pl.when(jnp.logical_and(j_end == 0, cmax_end > EXP2_ZERO_BELOW))
    def _():
        fold(0, "general")

    for p in pairs:
        o_ref[:, HG_W + p * BLOCK:HG_W + (p + 1) * BLOCK] = acc_ref[p].astype(BF16)


def _mixer(hq, hlf, hk, hv, hg, out_gain, qn4, kt3, v, n_real_blk):
    lp = v.shape[0]
    n_blk = lp // BLOCK
    phys = lambda c: (c + n_real_blk) % n_blk
    slab = pl.BlockSpec((BLOCK, HG_W), lambda c: (phys(c), 0))
    whole = lambda a: pl.BlockSpec(a.shape, lambda c: (0,) * a.ndim)
    resident = lambda a: pl.BlockSpec(a.shape, lambda c: (0,) * a.ndim,
                                      pipeline_mode=pl.Buffered(1))
    msum = jnp.asarray(_hgrn_sum_matrix(), dtype=BF16)
    wsum = jnp.asarray(_sb_sum_matrix(), dtype=BF16)
    return pl.pallas_call(
        functools.partial(_mixer_kernel, n_real_blk=n_real_blk),
        grid=(n_blk,),
        in_specs=[slab, slab, slab, slab, slab, whole(out_gain), whole(msum),
                  pl.BlockSpec((1,) + qn4.shape[1:], lambda c: (phys(c), 0, 0, 0)),
                  resident(kt3), resident(v), whole(wsum)],
        out_specs=pl.BlockSpec((BLOCK, HG_W + SB_W), lambda c: (phys(c), 0)),
        out_shape=jax.ShapeDtypeStruct((lp, HG_W + SB_W), BF16),
        scratch_shapes=[pltpu.VMEM((HG_HEADS, HG_DV, HG_DK), F32),
                        pltpu.VMEM((SB_PAIRS, BLOCK, BLOCK), F32),
                        pltpu.VMEM((SB_HEADS, BLOCK, BLOCK), F32)],
        compiler_params=pltpu.CompilerParams(
            dimension_semantics=("arbitrary",), vmem_limit_bytes=VMEM_LIMIT),
        name="mixer",
    )(hq, hlf, hk, hv, hg, out_gain, msum, qn4, kt3, v, wsum)


def _ffn_out_kernel(h1_ref, o_ref, wo_ref, g2_ref, w2i_ref, w2o_ref, out_ref,
                    act_ref):
    h2 = h1_ref[...] + _dot(o_ref[...], wo_ref[...])
    xn = _rms(h2, g2_ref[...]).astype(BF16)
    out_ref[...] = h2 + 0.5 * _swiglu(xn, w2i_ref, w2o_ref, act_ref)


def _ffn_out(h1, o, wo, g2, w2i, w2o, n_rows, tm):
    const = lambda shape: pl.BlockSpec(shape, lambda i: (0,) * len(shape),
                                       pipeline_mode=pl.Buffered(1))
    rows = lambda w: pl.BlockSpec((tm, w), lambda i: (i, 0))
    return pl.pallas_call(
        _ffn_out_kernel,
        grid=(n_rows // tm,),
        in_specs=[rows(D_MODEL), rows(HG_W + SB_W),
                  const((HG_W + SB_W, D_MODEL)),
                  const((1, D_MODEL)), const((D_MODEL, 2 * D_FF)),
                  const((D_FF, D_MODEL))],
        out_specs=rows(D_MODEL),
        out_shape=jax.ShapeDtypeStruct((n_rows, D_MODEL), F32),
        scratch_shapes=[pltpu.VMEM((tm, D_FF), BF16)],
        compiler_params=pltpu.CompilerParams(
            dimension_semantics=("arbitrary",), vmem_limit_bytes=VMEM_LIMIT),
        name="ffn_out",
    )(h1, o, wo, g2, w2i, w2o)


def kernel(x, meta_tokens, ffn1_norm, ffn1_w_in, ffn1_w_out, mix_norm, w_in,
           hgrn_lb_logits, hgrn_out_norm, sb_q_norm, sb_k_norm, w_out,
           ffn2_norm, ffn2_w_in, ffn2_w_out):
    b, seq, _ = x.shape
    assert b == 1 and seq % BLOCK == 0
    assert ffn1_norm.shape[0] == 1, "single layer"
    n_real_blk = seq // BLOCK
    tm = 512
    assert seq % tm == 0

    meta_tile = jnp.zeros((tm, D_MODEL), x.dtype).at[PAD:BLOCK].set(
        meta_tokens.astype(x.dtype))

    win = w_in[0].astype(BF16)
    o_hg, o_sq, o_sk, o_sv = 4 * HG_W, 4 * HG_W + SB_W, 4 * HG_W + 2 * SB_W, 4 * HG_W + 3 * SB_W
    whg, wq = win[:, :o_hg], win[:, o_hg:o_sq]
    wkt, wv = win[:, o_sq:o_sk].T, win[:, o_sk:o_sv]
    gk = jnp.tile(sb_k_norm[0], SB_HEADS).reshape(SB_W, 1)
    gq = jnp.tile(sb_q_norm[0], 2).reshape(1, BLOCK)

    h1, hq, hlf, hk, hv, hg, qn4, kt3, v = _ffn_in(
        x[0], meta_tile, ffn1_norm, ffn1_w_in[0].astype(BF16),
        ffn1_w_out[0].astype(BF16), mix_norm, whg, wq, wkt, wv, gk, gq,
        hgrn_lb_logits, tm)
    o = _mixer(hq, hlf, hk, hv, hg, hgrn_out_norm, qn4, kt3, v, n_real_blk)
    out = _ffn_out(h1, o, w_out[0].astype(BF16), ffn2_norm,
                   ffn2_w_in[0].astype(BF16), ffn2_w_out[0].astype(BF16), seq, tm)
    return out[None]
```

```python
import functools

import numpy as np
import jax
import jax.numpy as jnp
from jax import lax
from jax.experimental import pallas as pl
from jax.experimental.pallas import tpu as pltpu

F32 = jnp.float32
BF16 = jnp.bfloat16

D_MODEL = 1024
N_META = 16
BLOCK = 128
PAD = (-N_META) % BLOCK
HG_HEADS = 4
HG_DK = 128
HG_DV = 128
HG_W = HG_HEADS * HG_DK
SB_HEADS = 8
SB_DH = 64
SB_W = SB_HEADS * SB_DH
SB_PAIRS = SB_HEADS // 2
D_FF = 2816
RMS_EPS = 1e-6
FF_CHUNK = 256
N_LEVELS = 7
SUBLANES = 8
LOG2E = 1.4426950408889634
EXP2_ZERO_BELOW = -150.0
FAR_ROWS = 48
VMEM_LIMIT = 56 * 1024 * 1024
NT_DIMS = (((1,), (1,)), ((), ()))
TN_DIMS = (((0,), (0,)), ((), ()))


def _dot(a, b):
    return jnp.dot(a, b, preferred_element_type=F32)


def _dot_nt(a, b):
    return lax.dot_general(a, b, NT_DIMS, preferred_element_type=F32)


def _rms(x, gain):
    ms = jnp.mean(x * x, axis=-1, keepdims=True)
    return x * lax.rsqrt(ms + RMS_EPS) * gain


def _split2(x):
    hi = x.astype(BF16)
    lo = (x - hi.astype(F32)).astype(BF16)
    return hi, lo


def _neg_abs(x):
    return -jnp.abs(x)


def _silu(x):
    return x * jax.nn.sigmoid(x)


def _blk(i):
    return slice(i * BLOCK, (i + 1) * BLOCK)


def _swiglu(xn, w_in_ref, w_out_ref, act_ref):
    for c in range(D_FF // FF_CHUNK):
        lo, hi = c * FF_CHUNK, (c + 1) * FF_CHUNK
        g = _dot(xn, w_in_ref[:, lo:hi])
        u = _dot(xn, w_in_ref[:, D_FF + lo:D_FF + hi])
        act_ref[:, lo:hi] = (_silu(g) * u).astype(BF16)
    return _dot(act_ref[...], w_out_ref[...])


def _ffn_in_kernel(x_ref, meta_ref, g1_ref, w1i_ref, w1o_ref, gm_ref, whg_ref,
                   wq_ref, wkt_ref, wv_ref, gk_ref, gq_ref, lbl_ref,
                   h1_ref, hq_ref, hlf_ref, hk_ref, hv_ref, hg_ref,
                   qn_ref, kt_ref, v_ref, act_ref):
    tm = x_ref.shape[0]
    is_meta = pl.program_id(0) == pl.num_programs(0) - 1
    h = jnp.where(is_meta, meta_ref[...], x_ref[...])
    xn = _rms(h, g1_ref[...]).astype(BF16)
    h1 = h + 0.5 * _swiglu(xn, w1i_ref, w1o_ref, act_ref)
    h1_ref[...] = h1
    xm = _rms(h1, gm_ref[...]).astype(BF16)

    uhg = _dot(xm, whg_ref[...])
    hq_ref[...] = _silu(uhg[:, 0:HG_W])
    hv_ref[...] = uhg[:, 2 * HG_W:3 * HG_W].astype(BF16)
    hg_ref[...] = _silu(uhg[:, 3 * HG_W:4 * HG_W])
    lg = lbl_ref[...]
    e = jnp.exp(lg - jnp.max(lg, axis=0, keepdims=True))
    lb = e[0:1] / jnp.sum(e, axis=0, keepdims=True)
    z = uhg[:, HG_W:2 * HG_W]
    ez = jnp.exp(_neg_abs(z))
    rz = 1.0 / (1.0 + ez)
    erz = ez * rz
    pos = z >= 0.0
    f = lb + (1.0 - lb) * jnp.where(pos, rz, erz)
    valid = jnp.logical_or(
        jnp.logical_not(is_meta),
        lax.broadcasted_iota(jnp.int32, z.shape, 0) >= PAD)
    hlf_ref[...] = jnp.where(valid, jnp.log2(f), 0.0)
    hk_ref[...] = jnp.where(valid, (1.0 - lb) * jnp.where(pos, erz, rz), 0.0)

    q = _dot(xm, wq_ref[...])
    low = lax.broadcasted_iota(jnp.int32, (tm, BLOCK), 1) < SB_DH
    qscale = gq_ref[...] * (LOG2E / np.sqrt(np.float32(SB_DH)))
    for p in range(SB_PAIRS):
        qp = q[:, _blk(p)]
        for a in range(2):
            own = low if a == 0 else jnp.logical_not(low)
            ms = jnp.sum(jnp.where(own, qp * qp, 0.0), axis=-1,
                         keepdims=True) * (1.0 / SB_DH)
            qn = jnp.where(own, qp * lax.rsqrt(ms + RMS_EPS) * qscale,
                           0.0).astype(BF16)
            for t in range(tm // BLOCK):
                qn_ref[t, p, _blk(a), :] = qn[_blk(t)]
    v_ref[...] = _dot(xm, wv_ref[...]).astype(BF16)
    kt = _dot_nt(wkt_ref[...], xm)
    k3 = kt.reshape(SB_HEADS, SB_DH, tm)
    ms = jnp.mean(k3 * k3, axis=1, keepdims=True)
    kn = (k3 * lax.rsqrt(ms + RMS_EPS)).reshape(SB_W, tm) * gk_ref[...]
    kn = kn.astype(BF16)
    for t in range(tm // BLOCK):
        kt_ref[t] = kn[:, _blk(t)]


def _ffn_in(x2d, meta_tile, g1, w1i, w1o, gm, whg, wq, wkt, wv, gk, gq, lbl, tm):
    n_real_tiles = x2d.shape[0] // tm
    lp = x2d.shape[0] + BLOCK
    n_blk = lp // BLOCK
    tb = tm // BLOCK
    const = lambda shape: pl.BlockSpec(shape, lambda i: (0,) * len(shape),
                                       pipeline_mode=pl.Buffered(1))
    rows = lambda w: pl.BlockSpec((tm, w), lambda i: (i, 0))
    seq = lambda w, dt: jax.ShapeDtypeStruct((lp, w), dt)
    return pl.pallas_call(
        _ffn_in_kernel,
        grid=(n_real_tiles + 1,),
        in_specs=[pl.BlockSpec((tm, D_MODEL),
                               lambda i: (jnp.minimum(i, n_real_tiles - 1), 0)),
                  const((tm, D_MODEL)), const((1, D_MODEL)),
                  const((D_MODEL, 2 * D_FF)), const((D_FF, D_MODEL)),
                  const((1, D_MODEL)), const((D_MODEL, 4 * HG_W)),
                  const((D_MODEL, SB_W)), const((SB_W, D_MODEL)),
                  const((D_MODEL, SB_W)), const((SB_W, 1)),
                  const((1, BLOCK)), const((2, HG_W))],
        out_specs=[rows(D_MODEL), rows(HG_W), rows(HG_W), rows(HG_W),
                   rows(HG_W), rows(HG_W),
                   pl.BlockSpec((tb, SB_PAIRS, 2 * BLOCK, BLOCK),
                                lambda i: (i, 0, 0, 0)),
                   pl.BlockSpec((tb, SB_W, BLOCK), lambda i: (i, 0, 0)),
                   rows(SB_W)],
        out_shape=[seq(D_MODEL, F32), seq(HG_W, F32), seq(HG_W, F32),
                   seq(HG_W, F32), seq(HG_W, BF16), seq(HG_W, F32),
                   jax.ShapeDtypeStruct((n_blk, SB_PAIRS, 2 * BLOCK, BLOCK), BF16),
                   jax.ShapeDtypeStruct((n_blk, SB_W, BLOCK), BF16),
                   seq(SB_W, BF16)],
        scratch_shapes=[pltpu.VMEM((tm, D_FF), BF16)],
        compiler_params=pltpu.CompilerParams(
            dimension_semantics=("arbitrary",), vmem_limit_bytes=VMEM_LIMIT),
        name="ffn_in",
    )(x2d, meta_tile, g1, w1i, w1o, gm, whg, wq, wkt, wv, gk, gq, lbl)


def _hgrn_sum_matrix():
    t = np.arange(BLOCK)[:, None]
    j = np.arange(BLOCK)[None, :]
    mats = [(j <= t)]
    for lvl in range(N_LEVELS):
        c = 1 << lvl
        m = (t // (2 * c)) * (2 * c) + c
        upper = (t >= m) & (j >= m) & (j <= t)
        lower = (t < m) & (j > t) & (j <= m - 1)
        mats.append(upper | lower)
    m = np.concatenate(mats, axis=0).astype(np.float32)
    return np.concatenate([m, m], axis=1)


def _hgrn_stages(q_ref, lf_ref, k_ref, v_ref, g_ref, gain_ref, m_ref,
                 o_ref, st_ref):
    heads = range(HG_HEADS)
    hs = lambda a, h: a[:, h * HG_DK:(h + 1) * HG_DK]

    q = q_ref[...]
    k = k_ref[...]
    v = v_ref[...]
    x = _dot(m_ref[...], jnp.concatenate(_split2(lf_ref[...]), axis=0))
    yield
    bcum = x[0:BLOCK]
    b_last = bcum[BLOCK - 1:BLOCK]
    qe = (q * jnp.exp2(bcum)).astype(BF16)
    kd = (k * jnp.exp2(b_last - bcum)).astype(BF16)
    st_decay = jnp.exp2(b_last)

    row = lax.broadcasted_iota(jnp.int32, (BLOCK, BLOCK), 0)
    col = lax.broadcasted_iota(jnp.int32, (BLOCK, BLOCK), 1)
    qb, kb = q.astype(BF16), k.astype(BF16)
    diag = row == col
    attn = [jnp.where(diag, _dot_nt(hs(qb, h), hs(kb, h)), 0.0) for h in heads]
    rowf = lax.broadcasted_iota(jnp.int32, q.shape, 0)
    for lvl in range(N_LEVELS):
        if lvl % 2 == 1:
            yield
        half = 1 << lvl
        el = jnp.exp2(x[(lvl + 1) * BLOCK:(lvl + 2) * BLOCK])
        if half >= SUBLANES:
            qparts, kparts = [], []
            for b in range(BLOCK // half):
                rows = slice(b * half, (b + 1) * half)
                zero = jnp.zeros((half, q.shape[1]), F32)
                if b % 2 == 1:
                    qparts.append(q[rows] * el[rows])
                    kparts.append(zero)
                else:
                    qparts.append(zero)
                    kparts.append(k[rows] * el[rows])
            ql = jnp.concatenate(qparts, axis=0).astype(BF16)
            kl = jnp.concatenate(kparts, axis=0).astype(BF16)
        else:
            is_q = ((rowf >> lvl) & 1) == 1
            ql = jnp.where(is_q, q * el, 0.0).astype(BF16)
            kl = jnp.where(is_q, 0.0, k * el).astype(BF16)
        same = (row >> (lvl + 1)) == (col >> (lvl + 1))
        for h in heads:
            al = _dot_nt(hs(ql, h), hs(kl, h))
            if lvl + 1 < N_LEVELS:
                al = jnp.where(same, al, 0.0)
            attn[h] = attn[h] + al
    yield

    o = []
    for h in heads:
        st = st_ref[h]
        oh = _dot(attn[h].astype(BF16), hs(v, h))
        oh = oh + _dot_nt(hs(qe, h), st.astype(BF16))
        st_ref[h] = st * hs(st_decay, h) + lax.dot_general(
            hs(v, h), hs(kd, h), TN_DIMS, preferred_element_type=F32)
        o.append(oh)
    yield
    o = jnp.concatenate([_rms(o[h], hs(gain_ref[...], h)) for h in heads], axis=1)
    o_ref[:, 0:HG_W] = (o * g_ref[...]).astype(BF16)


def _interleave(*stage_lists):
    results = [None] * len(stage_lists)
    active = dict(enumerate(stage_lists))
    while active:
        for i, g in list(active.items()):
            try:
                next(g)
            except StopIteration as stop:
                results[i] = stop.value
                del active[i]
    return results


def _sb_sum_matrix():
    j = np.arange(BLOCK)[:, None]
    s = np.arange(BLOCK)[None, :]
    w = np.concatenate([(j >= s), np.ones((BLOCK, BLOCK), bool)],
                       axis=1).astype(np.float32)
    return np.concatenate([w, w], axis=0)


def _softplus2(z):
    return jnp.maximum(z, 0.0) + jnp.log2(1.0 + jnp.exp2(_neg_abs(z)))


def _mixer_kernel(hq_ref, hlf_ref, hk_ref, hv_ref, hg_ref, gain_ref, m_ref,
                  qn_ref, kt_ref, v_ref, w_ref, o_ref,
                  st_ref, acc_ref, crep_ref, *, n_real_blk):
    n_blk = kt_ref.shape[0]
    c = pl.program_id(0)

    @pl.when(c == 0)
    def _():
        st_ref[...] = jnp.zeros_like(st_ref)

    def hgrn():
        return _hgrn_stages(hq_ref, hlf_ref, hk_ref, hv_ref, hg_ref, gain_ref,
                            m_ref, o_ref, st_ref)

    row = lax.broadcasted_iota(jnp.int32, (BLOCK, BLOCK), 0)
    col = lax.broadcasted_iota(jnp.int32, (BLOCK, BLOCK), 1)
    low = col < SB_DH
    pairs = range(SB_PAIRS)
    heads = range(SB_HEADS)

    def fold_stages(j_top, tiles, fresh):
        n = len(tiles)
        nrows = [t[1] for t in tiles]
        assert all(nr == BLOCK for nr in nrows[:-1])
        pjs = [lax.rem(j_top - i + n_real_blk, n_blk) for i in range(n)]
        masks = []
        for i, (kind, nr) in enumerate(tiles):
            assert kind is None or nr == BLOCK
            if kind == "diag":
                m = col < row
            elif kind == "general":
                kpos = (j_top - i) * BLOCK + col
                m = jnp.logical_and(kpos < c * BLOCK + row, kpos >= PAD)
            elif kind == "late_rows":
                m = row >= FAR_ROWS
            else:
                m = None
            masks.append(m)
        z = {}
        for p in pairs:
            kt = jnp.concatenate([kt_ref[pjs[i], _blk(p), :] for i in range(n)],
                                 axis=1)
            zz = _dot(qn_ref[0, p], kt)
            for a in range(2):
                for i in range(n):
                    z[2 * p + a, i] = zz[_blk(a), _blk(i)][:nrows[i]]
        yield
        order = [(hd, i) for hd in heads for i in range(n)]
        packed = []
        for hd, i in order:
            sp = _softplus2(z[hd, i])
            if masks[i] is not None:
                sp = jnp.where(masks[i], sp, 0.0)
            packed.append(jnp.concatenate(_split2(sp), axis=1))
        r = _dot(jnp.concatenate(packed, axis=0), w_ref[...])
        offs = np.cumsum([0] + [nrows[i] for _, i in order])
        yield
        w = {}
        cmax = cmax_rest = None
        for hd in heads:
            crep = None if fresh else crep_ref[hd]
            for i in range(n):
                nr = nrows[i]
                o0 = int(offs[hd * n + i])
                rh = r[o0:o0 + nr]
                arg = z[hd, i] - rh[:, :BLOCK]
                wh = jnp.exp2(arg if crep is None else arg + crep[:nr])
                if masks[i] is not None:
                    wh = jnp.where(masks[i], wh, 0.0)
                w[hd, i] = wh.astype(BF16)
                if crep is None:
                    crep = -rh[:, BLOCK:]
                elif nr == BLOCK:
                    crep = crep - rh[:, BLOCK:]
                else:
                    rest = crep[nr:]
                    cmax_rest = (rest if cmax_rest is None
                                 else jnp.maximum(cmax_rest, rest))
                    crep = jnp.concatenate([crep[:nr] - rh[:, BLOCK:], rest], axis=0)
            crep_ref[hd] = crep
            cmax = crep if cmax is None else jnp.maximum(cmax, crep)
        yield
        full = [i for i in range(n) if nrows[i] == BLOCK]
        for p in pairs:
            wp = jnp.concatenate(
                [jnp.concatenate([w[2 * p + a, i] for i in full], axis=1)
                 for a in range(2)], axis=0)
            vtile = lambda i: v_ref[
                pl.ds(pl.multiple_of(pjs[i] * BLOCK, BLOCK), BLOCK), _blk(p)]
            pv = _dot(wp, jnp.concatenate([vtile(i) for i in full], axis=0))
            pv = jnp.where(low, pv[:BLOCK], pv[BLOCK:])
            if nrows[-1] < BLOCK:
                nr = nrows[-1]
                ps = _dot(jnp.concatenate([w[2 * p, n - 1], w[2 * p + 1, n - 1]],
                                          axis=0), vtile(n - 1))
                low_nr = lax.broadcasted_iota(jnp.int32, (nr, BLOCK), 1) < SB_DH
                ps = jnp.where(low_nr, ps[:nr], ps[nr:])
                pv = jnp.concatenate([pv[:nr] + ps, pv[nr:]], axis=0)
            if fresh:
                acc_ref[p] = pv
                o_ref[:, HG_W + p * BLOCK:HG_W + (p + 1) * BLOCK] = pv.astype(BF16)
            else:
                acc_ref[p] = acc_ref[p] + pv
        return (jnp.max(cmax), jnp.float32(-jnp.inf) if cmax_rest is None
                else jnp.max(cmax_rest))

    n_fast = 3
    fast = c >= n_fast

    def first_fold(tiles):
        return lambda: _interleave(fold_stages(c, tiles, True), hgrn())[0]

    def fold(j_top, kind):
        return _interleave(fold_stages(j_top, ((kind, BLOCK),), False))[0][0]

    cmax0, cmax_rest = lax.cond(
        fast,
        first_fold((("diag", BLOCK), (None, BLOCK), (None, FAR_ROWS))),
        first_fold((("general", BLOCK),)))

    @pl.when(cmax0 > EXP2_ZERO_BELOW)
    def _():
        cmax1 = lax.cond(cmax_rest > EXP2_ZERO_BELOW,
                         lambda: fold(c - (n_fast - 1), "late_rows"),
                         lambda: cmax0)

        def cond(carry):
            j, cmax = carry
            return jnp.logical_and(j >= 1, cmax > EXP2_ZERO_BELOW)

        def body(carry):
            j, _ = carry
            return j - 1, fold(j, None)

        j0 = jnp.where(fast, c - n_fast, c - 1)
        j_end, cmax_end = lax.while_loop(cond, body, (j0, cmax1))

        @pl.when(jnp.logical_and(j_end == 0, cmax_end > EXP2_ZERO_BELOW))
        def _():
            fold(0, "general")

        for p in pairs:
            o_ref[:, HG_W + p * BLOCK:HG_W + (p + 1) * BLOCK] = (
                acc_ref[p].astype(BF16))


def _mixer(hq, hlf, hk, hv, hg, out_gain, qn4, kt3, v, n_real_blk):
    lp = v.shape[0]
    n_blk = lp // BLOCK
    phys = lambda c: (c + n_real_blk) % n_blk
    slab = pl.BlockSpec((BLOCK, HG_W), lambda c: (phys(c), 0))
    whole = lambda a: pl.BlockSpec(a.shape, lambda c: (0,) * a.ndim)
    resident = lambda a: pl.BlockSpec(a.shape, lambda c: (0,) * a.ndim,
                                      pipeline_mode=pl.Buffered(1))
    msum = jnp.asarray(_hgrn_sum_matrix(), dtype=BF16)
    wsum = jnp.asarray(_sb_sum_matrix(), dtype=BF16)
    return pl.pallas_call(
        functools.partial(_mixer_kernel, n_real_blk=n_real_blk),
        grid=(n_blk,),
        in_specs=[slab, slab, slab, slab, slab, whole(out_gain), whole(msum),
                  pl.BlockSpec((1,) + qn4.shape[1:], lambda c: (phys(c), 0, 0, 0)),
                  resident(kt3), resident(v), whole(wsum)],
        out_specs=pl.BlockSpec((BLOCK, HG_W + SB_W), lambda c: (phys(c), 0)),
        out_shape=jax.ShapeDtypeStruct((lp, HG_W + SB_W), BF16),
        scratch_shapes=[pltpu.VMEM((HG_HEADS, HG_DV, HG_DK), F32),
                        pltpu.VMEM((SB_PAIRS, BLOCK, BLOCK), F32),
                        pltpu.VMEM((SB_HEADS, BLOCK, BLOCK), F32)],
        compiler_params=pltpu.CompilerParams(
            dimension_semantics=("arbitrary",), vmem_limit_bytes=VMEM_LIMIT),
        name="mixer",
    )(hq, hlf, hk, hv, hg, out_gain, msum, qn4, kt3, v, wsum)


def _ffn_out_kernel(h1_ref, o_ref, wo_ref, g2_ref, w2i_ref, w2o_ref, out_ref,
                    act_ref):
    h2 = h1_ref[...] + _dot(o_ref[...], wo_ref[...])
    xn = _rms(h2, g2_ref[...]).astype(BF16)
    out_ref[...] = h2 + 0.5 * _swiglu(xn, w2i_ref, w2o_ref, act_ref)


def _ffn_out(h1, o, wo, g2, w2i, w2o, n_rows, tm):
    const = lambda shape: pl.BlockSpec(shape, lambda i: (0,) * len(shape),
                                       pipeline_mode=pl.Buffered(1))
    rows = lambda w: pl.BlockSpec((tm, w), lambda i: (i, 0))
    return pl.pallas_call(
        _ffn_out_kernel,
        grid=(n_rows // tm,),
        in_specs=[rows(D_MODEL), rows(HG_W + SB_W),
                  const((HG_W + SB_W, D_MODEL)),
                  const((1, D_MODEL)), const((D_MODEL, 2 * D_FF)),
                  const((D_FF, D_MODEL))],
        out_specs=rows(D_MODEL),
        out_shape=jax.ShapeDtypeStruct((n_rows, D_MODEL), F32),
        scratch_shapes=[pltpu.VMEM((tm, D_FF), BF16)],
        compiler_params=pltpu.CompilerParams(
            dimension_semantics=("arbitrary",), vmem_limit_bytes=VMEM_LIMIT),
        name="ffn_out",
    )(h1, o, wo, g2, w2i, w2o)


def kernel(x, meta_tokens, ffn1_norm, ffn1_w_in, ffn1_w_out, mix_norm, w_in,
           hgrn_lb_logits, hgrn_out_norm, sb_q_norm, sb_k_norm, w_out,
           ffn2_norm, ffn2_w_in, ffn2_w_out):
    b, seq, _ = x.shape
    assert b == 1 and seq % BLOCK == 0
    assert ffn1_norm.shape[0] == 1, "single layer"
    n_real_blk = seq // BLOCK
    tm = 512
    assert seq % tm == 0

    meta_tile = jnp.zeros((tm, D_MODEL), x.dtype).at[PAD:BLOCK].set(
        meta_tokens.astype(x.dtype))

    win = w_in[0].astype(BF16)
    o_hg, o_sq, o_sk, o_sv = 4 * HG_W, 4 * HG_W + SB_W, 4 * HG_W + 2 * SB_W, 4 * HG_W + 3 * SB_W
    whg, wq = win[:, :o_hg], win[:, o_hg:o_sq]
    wkt, wv = win[:, o_sq:o_sk].T, win[:, o_sk:o_sv]
    gk = jnp.tile(sb_k_norm[0], SB_HEADS).reshape(SB_W, 1)
    gq = jnp.tile(sb_q_norm[0], 2).reshape(1, BLOCK)

    h1, hq, hlf, hk, hv, hg, qn4, kt3, v = _ffn_in(
        x[0], meta_tile, ffn1_norm, ffn1_w_in[0].astype(BF16),
        ffn1_w_out[0].astype(BF16), mix_norm, whg, wq, wkt, wv, gk, gq,
        hgrn_lb_logits, tm)
    o = _mixer(hq, hlf, hk, hv, hg, hgrn_out_norm, qn4, kt3, v, n_real_blk)
    out = _ffn_out(h1, o, w_out[0].astype(BF16), ffn2_norm,
                   ffn2_w_in[0].astype(BF16), ffn2_w_out[0].astype(BF16), seq,
                   2 * tm)
    return out[None]
```

```python
import functools

import numpy as np
import jax
import jax.numpy as jnp
from jax import lax
from jax.experimental import pallas as pl
from jax.experimental.pallas import tpu as pltpu

F32 = jnp.float32
BF16 = jnp.bfloat16

D_MODEL = 1024
N_META = 16
BLOCK = 128
PAD = (-N_META) % BLOCK
HG_HEADS = 4
HG_DK = 128
HG_DV = 128
HG_W = HG_HEADS * HG_DK
SB_HEADS = 8
SB_DH = 64
SB_W = SB_HEADS * SB_DH
SB_PAIRS = SB_HEADS // 2
D_FF = 2816
RMS_EPS = 1e-6
FF_CHUNK = 256
N_LEVELS = 7
SUBLANES = 8
LOG2E = 1.4426950408889634
EXP2_ZERO_BELOW = -150.0
FAR_ROWS = 48
REC_QN = 0
REC_KT = REC_QN + SB_HEADS * BLOCK
REC_V = REC_KT + SB_W
REC_HV = REC_V + SB_PAIRS * BLOCK
REC_ROWS = REC_HV + HG_HEADS * BLOCK
KV_ROWS = REC_HV - REC_KT
VMEM_LIMIT = 56 * 1024 * 1024
NT_DIMS = (((1,), (1,)), ((), ()))
TN_DIMS = (((0,), (0,)), ((), ()))


def _dot(a, b):
    return jnp.dot(a, b, preferred_element_type=F32)


def _dot_nt(a, b):
    return lax.dot_general(a, b, NT_DIMS, preferred_element_type=F32)


def _rms(x, gain):
    ms = jnp.mean(x * x, axis=-1, keepdims=True)
    return x * lax.rsqrt(ms + RMS_EPS) * gain


def _split2(x):
    hi = x.astype(BF16)
    lo = (x - hi.astype(F32)).astype(BF16)
    return hi, lo


def _neg_abs(x):
    return -jnp.abs(x)


def _silu(x):
    return x * jax.nn.sigmoid(x)


def _blk(i):
    return slice(i * BLOCK, (i + 1) * BLOCK)


def _swiglu_stages(xn, w_in_ref, w_out_ref, act_ref):
    for c in range(D_FF // FF_CHUNK):
        lo, hi = c * FF_CHUNK, (c + 1) * FF_CHUNK
        g = _dot(xn, w_in_ref[:, lo:hi])
        u = _dot(xn, w_in_ref[:, D_FF + lo:D_FF + hi])
        act_ref[:, lo:hi] = (_silu(g) * u).astype(BF16)
        yield
    return _dot(act_ref[...], w_out_ref[...])


def _interleave(*stage_lists):
    results = [None] * len(stage_lists)
    active = dict(enumerate(stage_lists))
    while active:
        for i, g in list(active.items()):
            try:
                next(g)
            except StopIteration as stop:
                results[i] = stop.value
                del active[i]
    return results


def _swiglu(xn, w_in_ref, w_out_ref, act_ref):
    return _interleave(_swiglu_stages(xn, w_in_ref, w_out_ref, act_ref))[0]


def _ffn_in_kernel(x_ref, meta_ref, g1_ref, w1i_ref, w1o_ref, gm_ref, whg_ref,
                   wq_ref, wkt_ref, wv_ref, gk_ref, gq_ref, lbl_ref,
                   h1_ref, hf_ref, rec_ref, act_ref):
    tm = x_ref.shape[0]
    tb = tm // BLOCK
    is_meta = pl.program_id(0) == pl.num_programs(0) - 1
    h = jnp.where(is_meta, meta_ref[...], x_ref[...])
    xn = _rms(h, g1_ref[...]).astype(BF16)
    h1 = h + 0.5 * _swiglu(xn, w1i_ref, w1o_ref, act_ref)
    h1_ref[...] = h1
    xm = _rms(h1, gm_ref[...]).astype(BF16)

    def put(base, piece, val):
        for t in range(tb):
            lo = base + piece * BLOCK
            rec_ref[t, lo:lo + BLOCK, :] = val[_blk(t)]

    uhg = _dot(xm, whg_ref[...])
    hf_ref[:, 0:HG_W] = _silu(uhg[:, 0:HG_W])
    hv = uhg[:, 2 * HG_W:3 * HG_W].astype(BF16)
    for hd in range(HG_HEADS):
        put(REC_HV, hd, hv[:, _blk(hd)])
    hf_ref[:, 3 * HG_W:4 * HG_W] = _silu(uhg[:, 3 * HG_W:4 * HG_W])
    lg = lbl_ref[...]
    e = jnp.exp(lg - jnp.max(lg, axis=0, keepdims=True))
    lb = e[0:1] / jnp.sum(e, axis=0, keepdims=True)
    z = uhg[:, HG_W:2 * HG_W]
    ez = jnp.exp(_neg_abs(z))
    rz = 1.0 / (1.0 + ez)
    erz = ez * rz
    pos = z >= 0.0
    f = lb + (1.0 - lb) * jnp.where(pos, rz, erz)
    valid = jnp.logical_or(
        jnp.logical_not(is_meta),
        lax.broadcasted_iota(jnp.int32, z.shape, 0) >= PAD)
    hf_ref[:, HG_W:2 * HG_W] = jnp.where(valid, jnp.log2(f), 0.0)
    hf_ref[:, 2 * HG_W:3 * HG_W] = jnp.where(
        valid, (1.0 - lb) * jnp.where(pos, erz, rz), 0.0)

    q = _dot(xm, wq_ref[...])
    low = lax.broadcasted_iota(jnp.int32, (tm, BLOCK), 1) < SB_DH
    qscale = gq_ref[...] * (LOG2E / np.sqrt(np.float32(SB_DH)))
    for p in range(SB_PAIRS):
        qp = q[:, _blk(p)]
        for a in range(2):
            own = low if a == 0 else jnp.logical_not(low)
            ms = jnp.sum(jnp.where(own, qp * qp, 0.0), axis=-1,
                         keepdims=True) * (1.0 / SB_DH)
            qn = jnp.where(own, qp * lax.rsqrt(ms + RMS_EPS) * qscale, 0.0)
            put(REC_QN, 2 * p + a, qn.astype(BF16))
    v = _dot(xm, wv_ref[...]).astype(BF16)
    for p in range(SB_PAIRS):
        put(REC_V, p, v[:, _blk(p)])
    kt = _dot_nt(wkt_ref[...], xm)
    k3 = kt.reshape(SB_HEADS, SB_DH, tm)
    ms = jnp.mean(k3 * k3, axis=1, keepdims=True)
    kn = (k3 * lax.rsqrt(ms + RMS_EPS)).reshape(SB_W, tm) * gk_ref[...]
    kn = kn.astype(BF16)
    for t in range(tb):
        rec_ref[t, REC_KT:REC_KT + SB_W, :] = kn[:, _blk(t)]


def _ffn_in(x2d, meta_tile, g1, w1i, w1o, gm, whg, wq, wkt, wv, gk, gq, lbl, tm):
    n_real_tiles = x2d.shape[0] // tm
    lp = x2d.shape[0] + BLOCK
    n_blk = lp // BLOCK
    tb = tm // BLOCK
    const = lambda shape: pl.BlockSpec(shape, lambda i: (0,) * len(shape),
                                       pipeline_mode=pl.Buffered(1))
    rows = lambda w: pl.BlockSpec((tm, w), lambda i: (i, 0))
    return pl.pallas_call(
        _ffn_in_kernel,
        grid=(n_real_tiles + 1,),
        in_specs=[pl.BlockSpec((tm, D_MODEL),
                               lambda i: (jnp.minimum(i, n_real_tiles - 1), 0)),
                  const((tm, D_MODEL)), const((1, D_MODEL)),
                  const((D_MODEL, 2 * D_FF)), const((D_FF, D_MODEL)),
                  const((1, D_MODEL)), const((D_MODEL, 4 * HG_W)),
                  const((D_MODEL, SB_W)), const((SB_W, D_MODEL)),
                  const((D_MODEL, SB_W)), const((SB_W, 1)),
                  const((1, BLOCK)), const((2, HG_W))],
        out_specs=[rows(D_MODEL), rows(4 * HG_W),
                   pl.BlockSpec((tb, REC_ROWS, BLOCK), lambda i: (i, 0, 0))],
        out_shape=[jax.ShapeDtypeStruct((lp, D_MODEL), F32),
                   jax.ShapeDtypeStruct((lp, 4 * HG_W), F32),
                   jax.ShapeDtypeStruct((n_blk, REC_ROWS, BLOCK), BF16)],
        scratch_shapes=[pltpu.VMEM((tm, D_FF), BF16)],
        compiler_params=pltpu.CompilerParams(
            dimension_semantics=("arbitrary",), vmem_limit_bytes=VMEM_LIMIT),
        name="ffn_in",
    )(x2d, meta_tile, g1, w1i, w1o, gm, whg, wq, wkt, wv, gk, gq, lbl)


def _hgrn_sum_matrix():
    t = np.arange(BLOCK)[:, None]
    j = np.arange(BLOCK)[None, :]
    mats = [(j <= t)]
    for lvl in range(N_LEVELS):
        c = 1 << lvl
        m = (t // (2 * c)) * (2 * c) + c
        upper = (t >= m) & (j >= m) & (j <= t)
        lower = (t < m) & (j > t) & (j <= m - 1)
        mats.append(upper | lower)
    m = np.concatenate(mats, axis=0).astype(np.float32)
    return np.concatenate([m, m], axis=1)


def _hgrn_stages(hf_ref, rec_ref, gain_ref, m_ref, o_ref, st_ref):
    heads = range(HG_HEADS)
    hs = lambda a, h: a[:, h * HG_DK:(h + 1) * HG_DK]

    q = hf_ref[:, 0:HG_W]
    k = hf_ref[:, 2 * HG_W:3 * HG_W]
    v = jnp.concatenate(
        [rec_ref[REC_HV + h * BLOCK:REC_HV + (h + 1) * BLOCK, :] for h in heads],
        axis=1)
    x = _dot(m_ref[...],
             jnp.concatenate(_split2(hf_ref[:, HG_W:2 * HG_W]), axis=0))
    yield
    bcum = x[0:BLOCK]
    b_last = bcum[BLOCK - 1:BLOCK]
    qe = (q * jnp.exp2(bcum)).astype(BF16)
    kd = (k * jnp.exp2(b_last - bcum)).astype(BF16)
    st_decay = jnp.exp2(b_last)

    row = lax.broadcasted_iota(jnp.int32, (BLOCK, BLOCK), 0)
    col = lax.broadcasted_iota(jnp.int32, (BLOCK, BLOCK), 1)
    qb, kb = q.astype(BF16), k.astype(BF16)
    diag = row == col
    attn = [jnp.where(diag, _dot_nt(hs(qb, h), hs(kb, h)), 0.0) for h in heads]
    rowf = lax.broadcasted_iota(jnp.int32, q.shape, 0)
    for lvl in range(N_LEVELS):
        if lvl % 2 == 1:
            yield
        half = 1 << lvl
        el = jnp.exp2(x[(lvl + 1) * BLOCK:(lvl + 2) * BLOCK])
        if half >= SUBLANES:
            qparts, kparts = [], []
            for b in range(BLOCK // half):
                rows = slice(b * half, (b + 1) * half)
                zero = jnp.zeros((half, q.shape[1]), F32)
                if b % 2 == 1:
                    qparts.append(q[rows] * el[rows])
                    kparts.append(zero)
                else:
                    qparts.append(zero)
                    kparts.append(k[rows] * el[rows])
            ql = jnp.concatenate(qparts, axis=0).astype(BF16)
            kl = jnp.concatenate(kparts, axis=0).astype(BF16)
        else:
            is_q = ((rowf >> lvl) & 1) == 1
            ql = jnp.where(is_q, q * el, 0.0).astype(BF16)
            kl = jnp.where(is_q, 0.0, k * el).astype(BF16)
        same = (row >> (lvl + 1)) == (col >> (lvl + 1))
        for h in heads:
            al = _dot_nt(hs(ql, h), hs(kl, h))
            if lvl + 1 < N_LEVELS:
                al = jnp.where(same, al, 0.0)
            attn[h] = attn[h] + al
    yield

    o = []
    for h in heads:
        st = st_ref[h]
        oh = _dot(attn[h].astype(BF16), hs(v, h))
        oh = oh + _dot_nt(hs(qe, h), st.astype(BF16))
        st_ref[h] = st * hs(st_decay, h) + lax.dot_general(
            hs(v, h), hs(kd, h), TN_DIMS, preferred_element_type=F32)
        o.append(oh)
    yield
    o = jnp.concatenate([_rms(o[h], hs(gain_ref[...], h)) for h in heads], axis=1)
    o_ref[:, 0:HG_W] = (o * hf_ref[:, 3 * HG_W:4 * HG_W]).astype(BF16)


def _sb_sum_matrix():
    j = np.arange(BLOCK)[:, None]
    s = np.arange(BLOCK)[None, :]
    w = np.concatenate([(j >= s), np.ones((BLOCK, BLOCK), bool)],
                       axis=1).astype(np.float32)
    return np.concatenate([w, w], axis=0)


def _softplus2(z):
    return jnp.maximum(z, 0.0) + jnp.log2(1.0 + jnp.exp2(_neg_abs(z)))


def _mix_ffn_kernel(hf_ref, rec_blk_ref, rec_hbm, gain_ref, m_ref, w_ref,
                    h1_ref, wo_ref, g2_ref, w2i_ref, w2o_ref, out_ref,
                    st_ref, acc_ref, crep_ref, o_scr, act_ref, ring, kv_scr, sem,
                    *, n_real_blk):
    n_blk = rec_hbm.shape[0]
    rec_ref = rec_blk_ref.at[0]
    c = pl.program_id(0)
    slot = lax.rem(c, 2)
    o_ref = o_scr.at[slot]
    o_prev = o_scr.at[1 - slot]

    @pl.when(c == 0)
    def _():
        st_ref[...] = jnp.zeros_like(st_ref)

    kv_now = rec_ref.at[REC_KT:REC_HV]
    ring[lax.rem(c, 3)] = kv_now[...]

    def hgrn():
        return _hgrn_stages(hf_ref, rec_ref, gain_ref, m_ref, o_ref, st_ref)

    def ffn():
        h2 = h1_ref[...] + _dot(o_prev[...], wo_ref[...])
        xn = _rms(h2, g2_ref[...]).astype(BF16)
        yield
        y = yield from _swiglu_stages(xn, w2i_ref, w2o_ref, act_ref)
        out_ref[...] = h2 + 0.5 * y

    row = lax.broadcasted_iota(jnp.int32, (BLOCK, BLOCK), 0)
    col = lax.broadcasted_iota(jnp.int32, (BLOCK, BLOCK), 1)
    low = col < SB_DH
    pairs = range(SB_PAIRS)
    heads = range(SB_HEADS)

    def fold_stages(j_top, tiles, srcs, fresh):
        n = len(tiles)
        nrows = [t[1] for t in tiles]
        assert all(nr == BLOCK for nr in nrows[:-1])
        masks = []
        for i, (kind, nr) in enumerate(tiles):
            assert kind is None or nr == BLOCK
            if kind == "diag":
                m = col < row
            elif kind == "general":
                kpos = (j_top - i) * BLOCK + col
                m = jnp.logical_and(kpos < c * BLOCK + row, kpos >= PAD)
            elif kind == "late_rows":
                m = row >= FAR_ROWS
            else:
                m = None
            masks.append(m)
        z = {}
        for p in pairs:
            kt = jnp.concatenate([srcs[i][_blk(p), :] for i in range(n)],
                                 axis=1)
            zz = _dot(rec_ref[REC_QN + 2 * p * BLOCK:REC_QN + 2 * (p + 1) * BLOCK, :],
                      kt)
            for a in range(2):
                for i in range(n):
                    z[2 * p + a, i] = zz[_blk(a), _blk(i)][:nrows[i]]
        yield
        order = [(hd, i) for hd in heads for i in range(n)]
        packed = []
        for hd, i in order:
            sp = _softplus2(z[hd, i])
            if masks[i] is not None:
                sp = jnp.where(masks[i], sp, 0.0)
            packed.append(jnp.concatenate(_split2(sp), axis=1))
        r = _dot(jnp.concatenate(packed, axis=0), w_ref[...])
        offs = np.cumsum([0] + [nrows[i] for _, i in order])
        yield
        w = {}
        cmax = cmax_rest = None
        for hd in heads:
            crep = None if fresh else crep_ref[hd]
            for i in range(n):
                nr = nrows[i]
                o0 = int(offs[hd * n + i])
                rh = r[o0:o0 + nr]
                arg = z[hd, i] - rh[:, :BLOCK]
                wh = jnp.exp2(arg if crep is None else arg + crep[:nr])
                if masks[i] is not None:
                    wh = jnp.where(masks[i], wh, 0.0)
                w[hd, i] = wh.astype(BF16)
                if crep is None:
                    crep = -rh[:, BLOCK:]
                elif nr == BLOCK:
                    crep = crep - rh[:, BLOCK:]
                else:
                    rest = crep[nr:]
                    cmax_rest = (rest if cmax_rest is None
                                 else jnp.maximum(cmax_rest, rest))
                    crep = jnp.concatenate([crep[:nr] - rh[:, BLOCK:], rest], axis=0)
            crep_ref[hd] = crep
            cmax = crep if cmax is None else jnp.maximum(cmax, crep)
        yield
        full = [i for i in range(n) if nrows[i] == BLOCK]
        for p in pairs:
            wp = jnp.concatenate(
                [jnp.concatenate([w[2 * p + a, i] for i in full], axis=1)
                 for a in range(2)], axis=0)
            vtile = lambda i: srcs[i][SB_W + p * BLOCK:SB_W + (p + 1) * BLOCK, :]
            pv = _dot(wp, jnp.concatenate([vtile(i) for i in full], axis=0))
            pv = jnp.where(low, pv[:BLOCK], pv[BLOCK:])
            if nrows[-1] < BLOCK:
                nr = nrows[-1]
                ps = _dot(jnp.concatenate([w[2 * p, n - 1], w[2 * p + 1, n - 1]],
                                          axis=0), vtile(n - 1))
                low_nr = lax.broadcasted_iota(jnp.int32, (nr, BLOCK), 1) < SB_DH
                ps = jnp.where(low_nr, ps[:nr], ps[nr:])
                pv = jnp.concatenate([pv[:nr] + ps, pv[nr:]], axis=0)
            if fresh:
                acc_ref[p] = pv
                o_ref[:, HG_W + p * BLOCK:HG_W + (p + 1) * BLOCK] = pv.astype(BF16)
            else:
                acc_ref[p] = acc_ref[p] + pv
        return (jnp.max(cmax), jnp.float32(-jnp.inf) if cmax_rest is None
                else jnp.max(cmax_rest))

    n_fast = 3
    fast = jnp.logical_and(c >= n_fast, c < n_blk)
    win = [kv_now, ring.at[lax.rem(c + 2, 3)], ring.at[lax.rem(c + 1, 3)]]
    no_more = (jnp.float32(-jnp.inf),) * 2

    def fast_step():
        tiles = (("diag", BLOCK), (None, BLOCK), (None, FAR_ROWS))
        return _interleave(fold_stages(c, tiles, win, True), hgrn(), ffn())[0]

    def first_blocks():
        res = _interleave(fold_stages(c, (("general", BLOCK),), win[:1], True),
                          hgrn())[0]

        @pl.when(c == n_fast - 1)
        def _():
            _interleave(ffn())
        return res

    def last_step():
        _interleave(ffn())
        return no_more

    cmax0, cmax_rest = lax.cond(
        fast, fast_step, lambda: lax.cond(c < n_fast, first_blocks, last_step))

    def fold(j_top, kind):
        pj = lax.rem(j_top + n_real_blk, n_blk)
        cp = pltpu.make_async_copy(rec_hbm.at[pj, REC_KT:REC_HV], kv_scr, sem.at[0])
        cp.start()
        cp.wait()
        return _interleave(fold_stages(j_top, ((kind, BLOCK),), [kv_scr],
                                       False))[0][0]

    @pl.when(cmax0 > EXP2_ZERO_BELOW)
    def _():
        cmax1 = lax.cond(cmax_rest > EXP2_ZERO_BELOW,
                         lambda: fold(c - (n_fast - 1), "late_rows"),
                         lambda: cmax0)

        def cond(carry):
            j, cmax = carry
            return jnp.logical_and(j >= 1, cmax > EXP2_ZERO_BELOW)

        def body(carry):
            j, _ = carry
            return j - 1, fold(j, None)

        j0 = jnp.where(fast, c - n_fast, c - 1)
        j_end, cmax_end = lax.while_loop(cond, body, (j0, cmax1))

        @pl.when(jnp.logical_and(j_end == 0, cmax_end > EXP2_ZERO_BELOW))
        def _():
            fold(0, "general")

        for p in pairs:
            o_ref[:, HG_W + p * BLOCK:HG_W + (p + 1) * BLOCK] = (
                acc_ref[p].astype(BF16))


def _mix_ffn(hf, rec, out_gain, h1, wo, g2, w2i, w2o, n_real_blk):
    n_blk = rec.shape[0]
    phys = lambda c: (jnp.minimum(c, n_blk - 1) + n_real_blk) % n_blk
    ffn_blk = lambda c: jnp.maximum(c - 2, 0)
    whole = lambda a: pl.BlockSpec(a.shape, lambda c: (0,) * a.ndim)
    resident = lambda a: pl.BlockSpec(a.shape, lambda c: (0,) * a.ndim,
                                      pipeline_mode=pl.Buffered(1))
    msum = jnp.asarray(_hgrn_sum_matrix(), dtype=BF16)
    wsum = jnp.asarray(_sb_sum_matrix(), dtype=BF16)
    return pl.pallas_call(
        functools.partial(_mix_ffn_kernel, n_real_blk=n_real_blk),
        grid=(n_blk + 1,),
        in_specs=[pl.BlockSpec((BLOCK, 4 * HG_W), lambda c: (phys(c), 0)),
                  pl.BlockSpec((1, REC_ROWS, BLOCK), lambda c: (phys(c), 0, 0)),
                  pl.BlockSpec(memory_space=pl.ANY),
                  whole(out_gain), whole(msum), whole(wsum),
                  pl.BlockSpec((BLOCK, D_MODEL), lambda c: (ffn_blk(c), 0)),
                  resident(wo), whole(g2), resident(w2i), resident(w2o)],
        out_specs=pl.BlockSpec((BLOCK, D_MODEL), lambda c: (ffn_blk(c), 0)),
        out_shape=jax.ShapeDtypeStruct((n_real_blk * BLOCK, D_MODEL), F32),
        scratch_shapes=[pltpu.VMEM((HG_HEADS, HG_DV, HG_DK), F32),
                        pltpu.VMEM((SB_PAIRS, BLOCK, BLOCK), F32),
                        pltpu.VMEM((SB_HEADS, BLOCK, BLOCK), F32),
                        pltpu.VMEM((2, BLOCK, HG_W + SB_W), BF16),
                        pltpu.VMEM((BLOCK, D_FF), BF16),
                        pltpu.VMEM((3, KV_ROWS, BLOCK), BF16),
                        pltpu.VMEM((KV_ROWS, BLOCK), BF16),
                        pltpu.SemaphoreType.DMA((1,))],
        compiler_params=pltpu.CompilerParams(
            dimension_semantics=("arbitrary",), vmem_limit_bytes=VMEM_LIMIT),
        name="mix_ffn",
    )(hf, rec, rec, out_gain, msum, wsum, h1, wo, g2, w2i, w2o)


def kernel(x, meta_tokens, ffn1_norm, ffn1_w_in, ffn1_w_out, mix_norm, w_in,
           hgrn_lb_logits, hgrn_out_norm, sb_q_norm, sb_k_norm, w_out,
           ffn2_norm, ffn2_w_in, ffn2_w_out):
    b, seq, _ = x.shape
    assert b == 1 and seq % BLOCK == 0
    assert ffn1_norm.shape[0] == 1, "single layer"
    n_real_blk = seq // BLOCK
    tm = 512
    assert seq % tm == 0

    meta_tile = jnp.zeros((tm, D_MODEL), x.dtype).at[PAD:BLOCK].set(
        meta_tokens.astype(x.dtype))

    win = w_in[0].astype(BF16)
    o_hg, o_sq, o_sk, o_sv = 4 * HG_W, 4 * HG_W + SB_W, 4 * HG_W + 2 * SB_W, 4 * HG_W + 3 * SB_W
    whg, wq = win[:, :o_hg], win[:, o_hg:o_sq]
    wkt, wv = win[:, o_sq:o_sk].T, win[:, o_sk:o_sv]
    gk = jnp.tile(sb_k_norm[0], SB_HEADS).reshape(SB_W, 1)
    gq = jnp.tile(sb_q_norm[0], 2).reshape(1, BLOCK)

    h1, hf, rec = _ffn_in(
        x[0], meta_tile, ffn1_norm, ffn1_w_in[0].astype(BF16),
        ffn1_w_out[0].astype(BF16), mix_norm, whg, wq, wkt, wv, gk, gq,
        hgrn_lb_logits, tm)
    out = _mix_ffn(hf, rec, hgrn_out_norm, h1, w_out[0].astype(BF16), ffn2_norm,
                   ffn2_w_in[0].astype(BF16), ffn2_w_out[0].astype(BF16),
                   n_real_blk)
    return out[None]
```

```python
import functools

import numpy as np
import jax
import jax.numpy as jnp
from jax import lax
from jax.experimental import pallas as pl
from jax.experimental.pallas import tpu as pltpu

F32 = jnp.float32
BF16 = jnp.bfloat16

D_MODEL = 1024
N_META = 16
BLOCK = 128
PAD = (-N_META) % BLOCK
HG_HEADS = 4
HG_DK = 128
HG_DV = 128
HG_W = HG_HEADS * HG_DK
SB_HEADS = 8
SB_DH = 64
SB_W = SB_HEADS * SB_DH
SB_PAIRS = SB_HEADS // 2
D_FF = 2816
RMS_EPS = 1e-6
FF_CHUNK = 256
N_LEVELS = 7
SUBLANES = 8
LOG2E = 1.4426950408889634
EXP2_ZERO_BELOW = -150.0
FAR_ROWS = 48
REC_QN = 0
REC_KT = REC_QN + SB_HEADS * BLOCK
REC_V = REC_KT + SB_W
REC_HV = REC_V + SB_PAIRS * BLOCK
REC_ROWS = REC_HV + HG_HEADS * BLOCK
KV_ROWS = REC_HV - REC_KT
VMEM_LIMIT = 56 * 1024 * 1024
NT_DIMS = (((1,), (1,)), ((), ()))
TN_DIMS = (((0,), (0,)), ((), ()))


def _dot(a, b):
    return jnp.dot(a, b, preferred_element_type=F32)


def _dot_nt(a, b):
    return lax.dot_general(a, b, NT_DIMS, preferred_element_type=F32)


def _rms(x, gain):
    ms = jnp.mean(x * x, axis=-1, keepdims=True)
    return x * lax.rsqrt(ms + RMS_EPS) * gain


def _split2(x):
    hi = x.astype(BF16)
    lo = (x - hi.astype(F32)).astype(BF16)
    return hi, lo


def _neg_abs(x):
    return -jnp.abs(x)


def _silu(x):
    return x * jax.nn.sigmoid(x)


def _blk(i):
    return slice(i * BLOCK, (i + 1) * BLOCK)


def _swiglu_stages(xn, w_in_ref, w_out_ref, act_ref):
    for c in range(D_FF // FF_CHUNK):
        lo, hi = c * FF_CHUNK, (c + 1) * FF_CHUNK
        g = _dot(xn, w_in_ref[:, lo:hi])
        u = _dot(xn, w_in_ref[:, D_FF + lo:D_FF + hi])
        act_ref[:, lo:hi] = (_silu(g) * u).astype(BF16)
        yield
    return _dot(act_ref[...], w_out_ref[...])


def _interleave(*stage_lists):
    results = [None] * len(stage_lists)
    active = dict(enumerate(stage_lists))
    while active:
        for i, g in list(active.items()):
            try:
                next(g)
            except StopIteration as stop:
                results[i] = stop.value
                del active[i]
    return results


def _swiglu(xn, w_in_ref, w_out_ref, act_ref):
    return _interleave(_swiglu_stages(xn, w_in_ref, w_out_ref, act_ref))[0]


def _ffn_in_tile(h, pads, g1_ref, w1i_ref, w1o_ref, gm_ref, whg_ref, wq_ref,
                 wkt_ref, wv_ref, gk_ref, gq_ref, lbl_ref,
                 h1_ref, hf_ref, rec_ref, act_ref):
    n = h.shape[0]
    rows = slice(0, n)
    xn = _rms(h, g1_ref[...]).astype(BF16)
    h1 = h + 0.5 * _swiglu(xn, w1i_ref, w1o_ref, act_ref.at[rows])
    h1_ref[rows] = h1
    xm = _rms(h1, gm_ref[...]).astype(BF16)

    def put(base, piece, val):
        for t in range(n // BLOCK):
            lo = base + piece * BLOCK
            rec_ref[t, lo:lo + BLOCK, :] = val[_blk(t)]

    uhg = _dot(xm, whg_ref[...])
    hf_ref[rows, 0:HG_W] = _silu(uhg[:, 0:HG_W])
    hv = uhg[:, 2 * HG_W:3 * HG_W].astype(BF16)
    for hd in range(HG_HEADS):
        put(REC_HV, hd, hv[:, _blk(hd)])
    hf_ref[rows, 3 * HG_W:4 * HG_W] = _silu(uhg[:, 3 * HG_W:4 * HG_W])
    lg = lbl_ref[...]
    e = jnp.exp(lg - jnp.max(lg, axis=0, keepdims=True))
    lb = e[0:1] / jnp.sum(e, axis=0, keepdims=True)
    z = uhg[:, HG_W:2 * HG_W]
    ez = jnp.exp(_neg_abs(z))
    rz = 1.0 / (1.0 + ez)
    erz = ez * rz
    pos = z >= 0.0
    lf = jnp.log2(lb + (1.0 - lb) * jnp.where(pos, rz, erz))
    k = (1.0 - lb) * jnp.where(pos, erz, rz)
    if pads:
        valid = lax.broadcasted_iota(jnp.int32, z.shape, 0) >= PAD
        lf, k = jnp.where(valid, lf, 0.0), jnp.where(valid, k, 0.0)
    hf_ref[rows, HG_W:2 * HG_W] = lf
    hf_ref[rows, 2 * HG_W:3 * HG_W] = k

    q = _dot(xm, wq_ref[...])
    low = lax.broadcasted_iota(jnp.int32, (n, BLOCK), 1) < SB_DH
    qscale = gq_ref[...] * (LOG2E / np.sqrt(np.float32(SB_DH)))
    for p in range(SB_PAIRS):
        qp = q[:, _blk(p)]
        for a in range(2):
            own = low if a == 0 else jnp.logical_not(low)
            ms = jnp.sum(jnp.where(own, qp * qp, 0.0), axis=-1,
                         keepdims=True) * (1.0 / SB_DH)
            qn = jnp.where(own, qp * lax.rsqrt(ms + RMS_EPS) * qscale, 0.0)
            put(REC_QN, 2 * p + a, qn.astype(BF16))
    v = _dot(xm, wv_ref[...]).astype(BF16)
    for p in range(SB_PAIRS):
        put(REC_V, p, v[:, _blk(p)])
    kt = _dot_nt(wkt_ref[...], xm)
    k3 = kt.reshape(SB_HEADS, SB_DH, n)
    ms = jnp.mean(k3 * k3, axis=1, keepdims=True)
    kn = (k3 * lax.rsqrt(ms + RMS_EPS)).reshape(SB_W, n) * gk_ref[...]
    kn = kn.astype(BF16)
    for t in range(n // BLOCK):
        rec_ref[t, REC_KT:REC_KT + SB_W, :] = kn[:, _blk(t)]


def _ffn_in_kernel(x_ref, meta_ref, *refs):
    is_meta = pl.program_id(0) == pl.num_programs(0) - 1

    @pl.when(jnp.logical_not(is_meta))
    def _():
        _ffn_in_tile(x_ref[...], False, *refs)

    @pl.when(is_meta)
    def _():
        _ffn_in_tile(meta_ref[...], True, *refs)


def _ffn_in(x2d, meta_tile, g1, w1i, w1o, gm, whg, wq, wkt, wv, gk, gq, lbl, tm):
    n_real_tiles = x2d.shape[0] // tm
    lp = x2d.shape[0] + BLOCK
    n_blk = lp // BLOCK
    tb = tm // BLOCK
    const = lambda shape: pl.BlockSpec(shape, lambda i: (0,) * len(shape),
                                       pipeline_mode=pl.Buffered(1))
    rows = lambda w: pl.BlockSpec((tm, w), lambda i: (i, 0))
    return pl.pallas_call(
        _ffn_in_kernel,
        grid=(n_real_tiles + 1,),
        in_specs=[pl.BlockSpec((tm, D_MODEL),
                               lambda i: (jnp.minimum(i, n_real_tiles - 1), 0)),
                  const((BLOCK, D_MODEL)), const((1, D_MODEL)),
                  const((D_MODEL, 2 * D_FF)), const((D_FF, D_MODEL)),
                  const((1, D_MODEL)), const((D_MODEL, 4 * HG_W)),
                  const((D_MODEL, SB_W)), const((SB_W, D_MODEL)),
                  const((D_MODEL, SB_W)), const((SB_W, 1)),
                  const((1, BLOCK)), const((2, HG_W))],
        out_specs=[rows(D_MODEL), rows(4 * HG_W),
                   pl.BlockSpec((tb, REC_ROWS, BLOCK), lambda i: (i, 0, 0))],
        out_shape=[jax.ShapeDtypeStruct((lp, D_MODEL), F32),
                   jax.ShapeDtypeStruct((lp, 4 * HG_W), F32),
                   jax.ShapeDtypeStruct((n_blk, REC_ROWS, BLOCK), BF16)],
        scratch_shapes=[pltpu.VMEM((tm, D_FF), BF16)],
        compiler_params=pltpu.CompilerParams(
            dimension_semantics=("arbitrary",), vmem_limit_bytes=VMEM_LIMIT),
        name="ffn_in",
    )(x2d, meta_tile, g1, w1i, w1o, gm, whg, wq, wkt, wv, gk, gq, lbl)


def _hgrn_sum_matrix():
    t = np.arange(BLOCK)[:, None]
    j = np.arange(BLOCK)[None, :]
    mats = [(j <= t)]
    for lvl in range(N_LEVELS):
        c = 1 << lvl
        m = (t // (2 * c)) * (2 * c) + c
        upper = (t >= m) & (j >= m) & (j <= t)
        lower = (t < m) & (j > t) & (j <= m - 1)
        mats.append(upper | lower)
    m = np.concatenate(mats, axis=0).astype(np.float32)
    return np.concatenate([m, m], axis=1)


def _hgrn_stages(hf_ref, rec_ref, gain_ref, m_ref, o_ref, st_ref):
    heads = range(HG_HEADS)
    hs = lambda a, h: a[:, h * HG_DK:(h + 1) * HG_DK]

    q = hf_ref[:, 0:HG_W]
    k = hf_ref[:, 2 * HG_W:3 * HG_W]
    v = jnp.concatenate(
        [rec_ref[REC_HV + h * BLOCK:REC_HV + (h + 1) * BLOCK, :] for h in heads],
        axis=1)
    x = _dot(m_ref[...],
             jnp.concatenate(_split2(hf_ref[:, HG_W:2 * HG_W]), axis=0))
    yield
    bcum = x[0:BLOCK]
    b_last = bcum[BLOCK - 1:BLOCK]
    qe = (q * jnp.exp2(bcum)).astype(BF16)
    kd = (k * jnp.exp2(b_last - bcum)).astype(BF16)
    st_decay = jnp.exp2(b_last)

    row = lax.broadcasted_iota(jnp.int32, (BLOCK, BLOCK), 0)
    col = lax.broadcasted_iota(jnp.int32, (BLOCK, BLOCK), 1)
    qb, kb = q.astype(BF16), k.astype(BF16)
    diag = row == col
    attn = [jnp.where(diag, _dot_nt(hs(qb, h), hs(kb, h)), 0.0) for h in heads]
    rowf = lax.broadcasted_iota(jnp.int32, q.shape, 0)
    for lvl in range(N_LEVELS):
        if lvl % 2 == 1:
            yield
        half = 1 << lvl
        el = jnp.exp2(x[(lvl + 1) * BLOCK:(lvl + 2) * BLOCK])
        if half >= SUBLANES:
            qparts, kparts = [], []
            for b in range(BLOCK // half):
                rows = slice(b * half, (b + 1) * half)
                zero = jnp.zeros((half, q.shape[1]), F32)
                if b % 2 == 1:
                    qparts.append(q[rows] * el[rows])
                    kparts.append(zero)
                else:
                    qparts.append(zero)
                    kparts.append(k[rows] * el[rows])
            ql = jnp.concatenate(qparts, axis=0).astype(BF16)
            kl = jnp.concatenate(kparts, axis=0).astype(BF16)
        else:
            is_q = ((rowf >> lvl) & 1) == 1
            ql = jnp.where(is_q, q * el, 0.0).astype(BF16)
            kl = jnp.where(is_q, 0.0, k * el).astype(BF16)
        same = (row >> (lvl + 1)) == (col >> (lvl + 1))
        for h in heads:
            al = _dot_nt(hs(ql, h), hs(kl, h))
            if lvl + 1 < N_LEVELS:
                al = jnp.where(same, al, 0.0)
            attn[h] = attn[h] + al
    yield

    o = []
    for h in heads:
        st = st_ref[h]
        oh = _dot(attn[h].astype(BF16), hs(v, h))
        oh = oh + _dot_nt(hs(qe, h), st.astype(BF16))
        st_ref[h] = st * hs(st_decay, h) + lax.dot_general(
            hs(v, h), hs(kd, h), TN_DIMS, preferred_element_type=F32)
        o.append(oh)
    yield
    o = jnp.concatenate([_rms(o[h], hs(gain_ref[...], h)) for h in heads], axis=1)
    o_ref[:, 0:HG_W] = (o * hf_ref[:, 3 * HG_W:4 * HG_W]).astype(BF16)


def _sb_sum_matrix():
    j = np.arange(BLOCK)[:, None]
    s = np.arange(BLOCK)[None, :]
    w = np.concatenate([(j >= s), np.ones((BLOCK, BLOCK), bool)],
                       axis=1).astype(np.float32)
    return np.concatenate([w, w], axis=0)


def _softplus2(z):
    return jnp.maximum(z, 0.0) + jnp.log2(1.0 + jnp.exp2(_neg_abs(z)))


def _mixer_kernel(hf_ref, rec_blk_ref, rec_hbm, gain_ref, m_ref, w_ref, o_ref,
                  st_ref, acc_ref, crep_ref, ring, kv_scr, sem, *, n_real_blk):
    n_blk = rec_hbm.shape[0]
    rec_ref = rec_blk_ref.at[0]
    c = pl.program_id(0)

    @pl.when(c == 0)
    def _():
        st_ref[...] = jnp.zeros_like(st_ref)

    kv_now = rec_ref.at[REC_KT:REC_HV]
    ring[lax.rem(c, 3)] = kv_now[...]

    def hgrn():
        return _hgrn_stages(hf_ref, rec_ref, gain_ref, m_ref, o_ref, st_ref)

    row = lax.broadcasted_iota(jnp.int32, (BLOCK, BLOCK), 0)
    col = lax.broadcasted_iota(jnp.int32, (BLOCK, BLOCK), 1)
    low = col < SB_DH
    pairs = range(SB_PAIRS)
    heads = range(SB_HEADS)

    def fold_stages(j_top, tiles, srcs, fresh):
        n = len(tiles)
        nrows = [t[1] for t in tiles]
        assert all(nr == BLOCK for nr in nrows[:-1])
        masks = []
        for i, (kind, nr) in enumerate(tiles):
            assert kind is None or nr == BLOCK
            if kind == "diag":
                m = col < row
            elif kind == "general":
                kpos = (j_top - i) * BLOCK + col
                m = jnp.logical_and(kpos < c * BLOCK + row, kpos >= PAD)
            elif kind == "late_rows":
                m = row >= FAR_ROWS
            else:
                m = None
            masks.append(m)
        z = {}
        for p in pairs:
            kt = jnp.concatenate([srcs[i][_blk(p), :] for i in range(n)],
                                 axis=1)
            zz = _dot(rec_ref[REC_QN + 2 * p * BLOCK:REC_QN + 2 * (p + 1) * BLOCK, :],
                      kt)
            for a in range(2):
                for i in range(n):
                    z[2 * p + a, i] = zz[_blk(a), _blk(i)][:nrows[i]]
        yield
        order = [(hd, i) for hd in heads for i in range(n)]
        packed = []
        for hd, i in order:
            sp = _softplus2(z[hd, i])
            if masks[i] is not None:
                sp = jnp.where(masks[i], sp, 0.0)
            packed.append(jnp.concatenate(_split2(sp), axis=1))
        r = _dot(jnp.concatenate(packed, axis=0), w_ref[...])
        offs = np.cumsum([0] + [nrows[i] for _, i in order])
        yield
        w = {}
        cmax = cmax_rest = None
        for hd in heads:
            crep = None if fresh else crep_ref[hd]
            for i in range(n):
                nr = nrows[i]
                o0 = int(offs[hd * n + i])
                rh = r[o0:o0 + nr]
                arg = z[hd, i] - rh[:, :BLOCK]
                wh = jnp.exp2(arg if crep is None else arg + crep[:nr])
                if masks[i] is not None:
                    wh = jnp.where(masks[i], wh, 0.0)
                w[hd, i] = wh.astype(BF16)
                if crep is None:
                    crep = -rh[:, BLOCK:]
                elif nr == BLOCK:
                    crep = crep - rh[:, BLOCK:]
                else:
                    rest = crep[nr:]
                    cmax_rest = (rest if cmax_rest is None
                                 else jnp.maximum(cmax_rest, rest))
                    crep = jnp.concatenate([crep[:nr] - rh[:, BLOCK:], rest], axis=0)
            crep_ref[hd] = crep
            cmax = crep if cmax is None else jnp.maximum(cmax, crep)
        yield
        full = [i for i in range(n) if nrows[i] == BLOCK]
        for p in pairs:
            wp = jnp.concatenate(
                [jnp.concatenate([w[2 * p + a, i] for i in full], axis=1)
                 for a in range(2)], axis=0)
            vtile = lambda i: srcs[i][SB_W + p * BLOCK:SB_W + (p + 1) * BLOCK, :]
            pv = _dot(wp, jnp.concatenate([vtile(i) for i in full], axis=0))
            pv = jnp.where(low, pv[:BLOCK], pv[BLOCK:])
            if nrows[-1] < BLOCK:
                nr = nrows[-1]
                ps = _dot(jnp.concatenate([w[2 * p, n - 1], w[2 * p + 1, n - 1]],
                                          axis=0), vtile(n - 1))
                low_nr = lax.broadcasted_iota(jnp.int32, (nr, BLOCK), 1) < SB_DH
                ps = jnp.where(low_nr, ps[:nr], ps[nr:])
                pv = jnp.concatenate([pv[:nr] + ps, pv[nr:]], axis=0)
            if fresh:
                acc_ref[p] = pv
                o_ref[:, HG_W + p * BLOCK:HG_W + (p + 1) * BLOCK] = pv.astype(BF16)
            else:
                acc_ref[p] = acc_ref[p] + pv
        return (jnp.max(cmax), jnp.float32(-jnp.inf) if cmax_rest is None
                else jnp.max(cmax_rest))

    n_fast = 3
    fast = c >= n_fast
    win = [kv_now, ring.at[lax.rem(c + 2, 3)], ring.at[lax.rem(c + 1, 3)]]

    def first_fold(tiles):
        return lambda: _interleave(
            fold_stages(c, tiles, win[:len(tiles)], True), hgrn())[0]

    cmax0, cmax_rest = lax.cond(
        fast,
        first_fold((("diag", BLOCK), (None, BLOCK), (None, FAR_ROWS))),
        first_fold((("general", BLOCK),)))

    def fold(j_top, kind):
        pj = lax.rem(j_top + n_real_blk, n_blk)
        cp = pltpu.make_async_copy(rec_hbm.at[pj, REC_KT:REC_HV], kv_scr, sem.at[0])
        cp.start()
        cp.wait()
        return _interleave(fold_stages(j_top, ((kind, BLOCK),), [kv_scr],
                                       False))[0][0]

    @pl.when(cmax0 > EXP2_ZERO_BELOW)
    def _():
        cmax1 = lax.cond(cmax_rest > EXP2_ZERO_BELOW,
                         lambda: fold(c - (n_fast - 1), "late_rows"),
                         lambda: cmax0)

        def cond(carry):
            j, cmax = carry
            return jnp.logical_and(j >= 1, cmax > EXP2_ZERO_BELOW)

        def body(carry):
            j, _ = carry
            return j - 1, fold(j, None)

        j0 = jnp.where(fast, c - n_fast, c - 1)
        j_end, cmax_end = lax.while_loop(cond, body, (j0, cmax1))

        @pl.when(jnp.logical_and(j_end == 0, cmax_end > EXP2_ZERO_BELOW))
        def _():
            fold(0, "general")

        for p in pairs:
            o_ref[:, HG_W + p * BLOCK:HG_W + (p + 1) * BLOCK] = (
                acc_ref[p].astype(BF16))


def _mixer(hf, rec, out_gain, n_real_blk):
    n_blk = rec.shape[0]
    lp = n_blk * BLOCK
    phys = lambda c: (c + n_real_blk) % n_blk
    whole = lambda a: pl.BlockSpec(a.shape, lambda c: (0,) * a.ndim)
    msum = jnp.asarray(_hgrn_sum_matrix(), dtype=BF16)
    wsum = jnp.asarray(_sb_sum_matrix(), dtype=BF16)
    return pl.pallas_call(
        functools.partial(_mixer_kernel, n_real_blk=n_real_blk),
        grid=(n_blk,),
        in_specs=[pl.BlockSpec((BLOCK, 4 * HG_W), lambda c: (phys(c), 0)),
                  pl.BlockSpec((1, REC_ROWS, BLOCK), lambda c: (phys(c), 0, 0)),
                  pl.BlockSpec(memory_space=pl.ANY),
                  whole(out_gain), whole(msum), whole(wsum)],
        out_specs=pl.BlockSpec((BLOCK, HG_W + SB_W), lambda c: (phys(c), 0)),
        out_shape=jax.ShapeDtypeStruct((lp, HG_W + SB_W), BF16),
        scratch_shapes=[pltpu.VMEM((HG_HEADS, HG_DV, HG_DK), F32),
                        pltpu.VMEM((SB_PAIRS, BLOCK, BLOCK), F32),
                        pltpu.VMEM((SB_HEADS, BLOCK, BLOCK), F32),
                        pltpu.VMEM((3, KV_ROWS, BLOCK), BF16),
                        pltpu.VMEM((KV_ROWS, BLOCK), BF16),
                        pltpu.SemaphoreType.DMA((1,))],
        compiler_params=pltpu.CompilerParams(
            dimension_semantics=("arbitrary",)),
        name="mixer",
    )(hf, rec, rec, out_gain, msum, wsum)


def _ffn_out_kernel(h1_ref, o_ref, wo_ref, g2_ref, w2i_ref, w2o_ref, out_ref,
                    act_ref):
    h2 = h1_ref[...] + _dot(o_ref[...], wo_ref[...])
    xn = _rms(h2, g2_ref[...]).astype(BF16)
    out_ref[...] = h2 + 0.5 * _swiglu(xn, w2i_ref, w2o_ref, act_ref)


def _ffn_out(h1, o, wo, g2, w2i, w2o, n_rows, tm):
    const = lambda shape: pl.BlockSpec(shape, lambda i: (0,) * len(shape),
                                       pipeline_mode=pl.Buffered(1))
    rows = lambda w: pl.BlockSpec((tm, w), lambda i: (i, 0))
    return pl.pallas_call(
        _ffn_out_kernel,
        grid=(n_rows // tm,),
        in_specs=[rows(D_MODEL), rows(HG_W + SB_W),
                  const((HG_W + SB_W, D_MODEL)),
                  const((1, D_MODEL)), const((D_MODEL, 2 * D_FF)),
                  const((D_FF, D_MODEL))],
        out_specs=rows(D_MODEL),
        out_shape=jax.ShapeDtypeStruct((n_rows, D_MODEL), F32),
        scratch_shapes=[pltpu.VMEM((tm, D_FF), BF16)],
        compiler_params=pltpu.CompilerParams(
            dimension_semantics=("arbitrary",), vmem_limit_bytes=VMEM_LIMIT),
        name="ffn_out",
    )(h1, o, wo, g2, w2i, w2o)


def kernel(x, meta_tokens, ffn1_norm, ffn1_w_in, ffn1_w_out, mix_norm, w_in,
           hgrn_lb_logits, hgrn_out_norm, sb_q_norm, sb_k_norm, w_out,
           ffn2_norm, ffn2_w_in, ffn2_w_out):
    b, seq, _ = x.shape
    assert b == 1 and seq % BLOCK == 0
    assert ffn1_norm.shape[0] == 1, "single layer"
    n_real_blk = seq // BLOCK
    tm = 512
    assert seq % tm == 0

    meta_tile = jnp.zeros((BLOCK, D_MODEL), x.dtype).at[PAD:].set(
        meta_tokens.astype(x.dtype))

    win = w_in[0].astype(BF16)
    o_hg, o_sq, o_sk, o_sv = 4 * HG_W, 4 * HG_W + SB_W, 4 * HG_W + 2 * SB_W, 4 * HG_W + 3 * SB_W
    whg, wq = win[:, :o_hg], win[:, o_hg:o_sq]
    wkt, wv = win[:, o_sq:o_sk].T, win[:, o_sk:o_sv]
    gk = jnp.tile(sb_k_norm[0], SB_HEADS).reshape(SB_W, 1)
    gq = jnp.tile(sb_q_norm[0], 2).reshape(1, BLOCK)

    h1, hf, rec = _ffn_in(
        x[0], meta_tile, ffn1_norm, ffn1_w_in[0].astype(BF16),
        ffn1_w_out[0].astype(BF16), mix_norm, whg, wq, wkt, wv, gk, gq,
        hgrn_lb_logits, tm)
    o = _mixer(hf, rec, hgrn_out_norm, n_real_blk)
    out = _ffn_out(h1, o, w_out[0].astype(BF16), ffn2_norm,
                   ffn2_w_in[0].astype(BF16), ffn2_w_out[0].astype(BF16), seq,
                   2 * tm)
    return out[None]
```

```python
import functools

import numpy as np
import jax
import jax.numpy as jnp
from jax import lax
from jax.experimental import pallas as pl
from jax.experimental.pallas import tpu as pltpu

F32 = jnp.float32
BF16 = jnp.bfloat16

D_MODEL = 1024
N_META = 16
BLOCK = 128
PAD = (-N_META) % BLOCK
HG_HEADS = 4
HG_DK = 128
HG_DV = 128
HG_W = HG_HEADS * HG_DK
SB_HEADS = 8
SB_DH = 64
SB_W = SB_HEADS * SB_DH
SB_PAIRS = SB_HEADS // 2
D_FF = 2816
RMS_EPS = 1e-6
FF_CHUNK = 256
N_LEVELS = 7
SUBLANES = 8
LOG2E = 1.4426950408889634
EXP2_ZERO_BELOW = -150.0
FAR_ROWS = 48
REC_QN = 0
REC_KT = REC_QN + SB_HEADS * BLOCK
REC_V = REC_KT + SB_W
REC_HV = REC_V + SB_PAIRS * BLOCK
REC_ROWS = REC_HV + HG_HEADS * BLOCK
KV_ROWS = REC_HV - REC_KT
VMEM_LIMIT = 56 * 1024 * 1024
NT_DIMS = (((1,), (1,)), ((), ()))
TN_DIMS = (((0,), (0,)), ((), ()))


def _dot(a, b):
    return jnp.dot(a, b, preferred_element_type=F32)


def _dot_nt(a, b):
    return lax.dot_general(a, b, NT_DIMS, preferred_element_type=F32)


def _rms(x, gain):
    ms = jnp.mean(x * x, axis=-1, keepdims=True)
    return x * lax.rsqrt(ms + RMS_EPS) * gain


def _split2(x):
    hi = x.astype(BF16)
    lo = (x - hi.astype(F32)).astype(BF16)
    return hi, lo


def _neg_abs(x):
    return -jnp.abs(x)


def _silu(x):
    return x * jax.nn.sigmoid(x)


def _blk(i):
    return slice(i * BLOCK, (i + 1) * BLOCK)


def _swiglu_stages(xn, w_in_ref, w_out_ref, act_ref):
    for c in range(D_FF // FF_CHUNK):
        lo, hi = c * FF_CHUNK, (c + 1) * FF_CHUNK
        g = _dot(xn, w_in_ref[:, lo:hi])
        u = _dot(xn, w_in_ref[:, D_FF + lo:D_FF + hi])
        act_ref[:, lo:hi] = (_silu(g) * u).astype(BF16)
        yield
    return _dot(act_ref[...], w_out_ref[...])


def _interleave(*stage_lists):
    results = [None] * len(stage_lists)
    active = dict(enumerate(stage_lists))
    while active:
        for i, g in list(active.items()):
            try:
                next(g)
            except StopIteration as stop:
                results[i] = stop.value
                del active[i]
    return results


def _swiglu(xn, w_in_ref, w_out_ref, act_ref):
    return _interleave(_swiglu_stages(xn, w_in_ref, w_out_ref, act_ref))[0]


def _ffn_in_tile(h, pads, g1_ref, w1i_ref, w1o_ref, gm_ref, whg_ref, wq_ref,
                 wkt_ref, wv_ref, gk_ref, gq_ref, lbl_ref,
                 h1_ref, hf_ref, rec_ref, act_ref):
    n = h.shape[0]
    rows = slice(0, n)
    xn = _rms(h, g1_ref[...]).astype(BF16)
    h1 = h + 0.5 * _swiglu(xn, w1i_ref, w1o_ref, act_ref.at[rows])
    h1_ref[rows] = h1
    xm = _rms(h1, gm_ref[...]).astype(BF16)

    def put(base, piece, val):
        for t in range(n // BLOCK):
            lo = base + piece * BLOCK
            rec_ref[t, lo:lo + BLOCK, :] = val[_blk(t)]

    uhg = _dot(xm, whg_ref[...])
    hf_ref[rows, 0:HG_W] = _silu(uhg[:, 0:HG_W])
    hv = uhg[:, 2 * HG_W:3 * HG_W].astype(BF16)
    for hd in range(HG_HEADS):
        put(REC_HV, hd, hv[:, _blk(hd)])
    hf_ref[rows, 3 * HG_W:4 * HG_W] = _silu(uhg[:, 3 * HG_W:4 * HG_W])
    lg = lbl_ref[...]
    e = jnp.exp(lg - jnp.max(lg, axis=0, keepdims=True))
    lb = e[0:1] / jnp.sum(e, axis=0, keepdims=True)
    z = uhg[:, HG_W:2 * HG_W]
    ez = jnp.exp(_neg_abs(z))
    rz = 1.0 / (1.0 + ez)
    erz = ez * rz
    pos = z >= 0.0
    lf = jnp.log2(lb + (1.0 - lb) * jnp.where(pos, rz, erz))
    k = (1.0 - lb) * jnp.where(pos, erz, rz)
    if pads:
        valid = lax.broadcasted_iota(jnp.int32, z.shape, 0) >= PAD
        lf, k = jnp.where(valid, lf, 0.0), jnp.where(valid, k, 0.0)
    hf_ref[rows, HG_W:2 * HG_W] = lf
    hf_ref[rows, 2 * HG_W:3 * HG_W] = k

    q = _dot(xm, wq_ref[...])
    low = lax.broadcasted_iota(jnp.int32, (n, BLOCK), 1) < SB_DH
    qscale = gq_ref[...] * (LOG2E / np.sqrt(np.float32(SB_DH)))
    for p in range(SB_PAIRS):
        qp = q[:, _blk(p)]
        for a in range(2):
            own = low if a == 0 else jnp.logical_not(low)
            ms = jnp.sum(jnp.where(own, qp * qp, 0.0), axis=-1,
                         keepdims=True) * (1.0 / SB_DH)
            qn = jnp.where(own, qp * lax.rsqrt(ms + RMS_EPS) * qscale, 0.0)
            put(REC_QN, 2 * p + a, qn.astype(BF16))
    v = _dot(xm, wv_ref[...]).astype(BF16)
    for p in range(SB_PAIRS):
        put(REC_V, p, v[:, _blk(p)])
    kt = _dot_nt(wkt_ref[...], xm)
    k3 = kt.reshape(SB_HEADS, SB_DH, n)
    ms = jnp.mean(k3 * k3, axis=1, keepdims=True)
    kn = (k3 * lax.rsqrt(ms + RMS_EPS)).reshape(SB_W, n) * gk_ref[...]
    kn = kn.astype(BF16)
    for t in range(n // BLOCK):
        rec_ref[t, REC_KT:REC_KT + SB_W, :] = kn[:, _blk(t)]


N_FFN_IN_PARAMS = 11


def _ffn_in_kernel(x_ref, meta_ref, *refs, cast_chunks):
    nc = len(cast_chunks)
    params = refs[:N_FFN_IN_PARAMS]
    cast_in = refs[N_FFN_IN_PARAMS:N_FFN_IN_PARAMS + nc]
    outs = refs[N_FFN_IN_PARAMS + nc:N_FFN_IN_PARAMS + nc + 3]
    cast_out = refs[N_FFN_IN_PARAMS + nc + 3:N_FFN_IN_PARAMS + 2 * nc + 3]
    act_ref = refs[-1]
    i = pl.program_id(0)
    is_meta = i == pl.num_programs(0) - 1

    for src, dst, chunks in zip(cast_in, cast_out, cast_chunks):
        @pl.when(i < chunks)
        def _():
            dst[...] = src[...].astype(BF16)

    @pl.when(jnp.logical_not(is_meta))
    def _():
        _ffn_in_tile(x_ref[...], False, *params, *outs, act_ref)

    @pl.when(is_meta)
    def _():
        _ffn_in_tile(meta_ref[...], True, *params, *outs, act_ref)


def _cast_chunks(n_rows, n_steps):
    return max(k for k in range(1, n_steps + 1)
               if n_rows % k == 0 and (n_rows // k) % 16 == 0)


def _ffn_in(x2d, meta_tile, g1, w1i, w1o, gm, whg, wq, wkt, wv, gk, gq, lbl,
            to_cast, tm):
    n_real_tiles = x2d.shape[0] // tm
    lp = x2d.shape[0] + BLOCK
    n_blk = lp // BLOCK
    tb = tm // BLOCK
    const = lambda shape: pl.BlockSpec(shape, lambda i: (0,) * len(shape),
                                       pipeline_mode=pl.Buffered(1))
    rows = lambda w: pl.BlockSpec((tm, w), lambda i: (i, 0))
    chunks = tuple(_cast_chunks(w.shape[0], n_real_tiles + 1) for w in to_cast)
    cast_specs = [
        pl.BlockSpec((w.shape[0] // k, w.shape[1]),
                     lambda i, k=k: (jnp.minimum(i, k - 1), 0))
        for w, k in zip(to_cast, chunks)]
    return pl.pallas_call(
        functools.partial(_ffn_in_kernel, cast_chunks=chunks),
        grid=(n_real_tiles + 1,),
        in_specs=[pl.BlockSpec((tm, D_MODEL),
                               lambda i: (jnp.minimum(i, n_real_tiles - 1), 0)),
                  const((BLOCK, D_MODEL)), const((1, D_MODEL)),
                  const((D_MODEL, 2 * D_FF)), const((D_FF, D_MODEL)),
                  const((1, D_MODEL)), const((D_MODEL, 4 * HG_W)),
                  const((D_MODEL, SB_W)), const((SB_W, D_MODEL)),
                  const((D_MODEL, SB_W)), const((SB_W, 1)),
                  const((1, BLOCK)), const((2, HG_W))] + cast_specs,
        out_specs=[rows(D_MODEL), rows(4 * HG_W),
                   pl.BlockSpec((tb, REC_ROWS, BLOCK), lambda i: (i, 0, 0))]
        + cast_specs,
        out_shape=[jax.ShapeDtypeStruct((lp, D_MODEL), F32),
                   jax.ShapeDtypeStruct((lp, 4 * HG_W), F32),
                   jax.ShapeDtypeStruct((n_blk, REC_ROWS, BLOCK), BF16)]
        + [jax.ShapeDtypeStruct(w.shape, BF16) for w in to_cast],
        scratch_shapes=[pltpu.VMEM((tm, D_FF), BF16)],
        compiler_params=pltpu.CompilerParams(
            dimension_semantics=("arbitrary",), vmem_limit_bytes=VMEM_LIMIT),
        name="ffn_in",
    )(x2d, meta_tile, g1, w1i, w1o, gm, whg, wq, wkt, wv, gk, gq, lbl, *to_cast)


def _hgrn_sum_matrix():
    t = np.arange(BLOCK)[:, None]
    j = np.arange(BLOCK)[None, :]
    mats = [(j <= t)]
    for lvl in range(N_LEVELS):
        c = 1 << lvl
        m = (t // (2 * c)) * (2 * c) + c
        upper = (t >= m) & (j >= m) & (j <= t)
        lower = (t < m) & (j > t) & (j <= m - 1)
        mats.append(upper | lower)
    m = np.concatenate(mats, axis=0).astype(np.float32)
    return np.concatenate([m, m], axis=1)


def _hgrn_stages(hf_ref, rec_ref, gain_ref, m_ref, o_ref, st_ref):
    heads = range(HG_HEADS)
    hs = lambda a, h: a[:, h * HG_DK:(h + 1) * HG_DK]

    q = hf_ref[:, 0:HG_W]
    k = hf_ref[:, 2 * HG_W:3 * HG_W]
    v = jnp.concatenate(
        [rec_ref[REC_HV + h * BLOCK:REC_HV + (h + 1) * BLOCK, :] for h in heads],
        axis=1)
    x = _dot(m_ref[...],
             jnp.concatenate(_split2(hf_ref[:, HG_W:2 * HG_W]), axis=0))
    yield
    bcum = x[0:BLOCK]
    b_last = bcum[BLOCK - 1:BLOCK]
    qe = (q * jnp.exp2(bcum)).astype(BF16)
    kd = (k * jnp.exp2(b_last - bcum)).astype(BF16)
    st_decay = jnp.exp2(b_last)

    row = lax.broadcasted_iota(jnp.int32, (BLOCK, BLOCK), 0)
    col = lax.broadcasted_iota(jnp.int32, (BLOCK, BLOCK), 1)
    qb, kb = q.astype(BF16), k.astype(BF16)
    diag = row == col
    attn = [jnp.where(diag, _dot_nt(hs(qb, h), hs(kb, h)), 0.0) for h in heads]
    rowf = lax.broadcasted_iota(jnp.int32, q.shape, 0)
    for lvl in range(N_LEVELS):
        if lvl % 2 == 1:
            yield
        half = 1 << lvl
        el = jnp.exp2(x[(lvl + 1) * BLOCK:(lvl + 2) * BLOCK])
        if half >= SUBLANES:
            qparts, kparts = [], []
            for b in range(BLOCK // half):
                rows = slice(b * half, (b + 1) * half)
                zero = jnp.zeros((half, q.shape[1]), F32)
                if b % 2 == 1:
                    qparts.append(q[rows] * el[rows])
                    kparts.append(zero)
                else:
                    qparts.append(zero)
                    kparts.append(k[rows] * el[rows])
            ql = jnp.concatenate(qparts, axis=0).astype(BF16)
            kl = jnp.concatenate(kparts, axis=0).astype(BF16)
        else:
            is_q = ((rowf >> lvl) & 1) == 1
            ql = jnp.where(is_q, q * el, 0.0).astype(BF16)
            kl = jnp.where(is_q, 0.0, k * el).astype(BF16)
        same = (row >> (lvl + 1)) == (col >> (lvl + 1))
        for h in heads:
            al = _dot_nt(hs(ql, h), hs(kl, h))
            if lvl + 1 < N_LEVELS:
                al = jnp.where(same, al, 0.0)
            attn[h] = attn[h] + al
    yield

    o = []
    for h in heads:
        st = st_ref[h]
        oh = _dot(attn[h].astype(BF16), hs(v, h))
        oh = oh + _dot_nt(hs(qe, h), st.astype(BF16))
        st_ref[h] = st * hs(st_decay, h) + lax.dot_general(
            hs(v, h), hs(kd, h), TN_DIMS, preferred_element_type=F32)
        o.append(oh)
    yield
    o = jnp.concatenate([_rms(o[h], hs(gain_ref[...], h)) for h in heads], axis=1)
    o_ref[:, 0:HG_W] = (o * hf_ref[:, 3 * HG_W:4 * HG_W]).astype(BF16)


def _sb_sum_matrix():
    j = np.arange(BLOCK)[:, None]
    s = np.arange(BLOCK)[None, :]
    w = np.concatenate([(j >= s), np.ones((BLOCK, BLOCK), bool)],
                       axis=1).astype(np.float32)
    return np.concatenate([w, w], axis=0)


def _softplus2(z):
    return jnp.maximum(z, 0.0) + jnp.log2(1.0 + jnp.exp2(_neg_abs(z)))


def _mixer_kernel(hf_ref, rec_blk_ref, rec_hbm, gain_ref, m_ref, w_ref, o_ref,
                  st_ref, acc_ref, crep_ref, ring, kv_scr, sem, *, n_real_blk):
    n_blk = rec_hbm.shape[0]
    rec_ref = rec_blk_ref.at[0]
    c = pl.program_id(0)

    @pl.when(c == 0)
    def _():
        st_ref[...] = jnp.zeros_like(st_ref)

    kv_now = rec_ref.at[REC_KT:REC_HV]
    ring[lax.rem(c, 3)] = kv_now[...]

    def hgrn():
        return _hgrn_stages(hf_ref, rec_ref, gain_ref, m_ref, o_ref, st_ref)

    row = lax.broadcasted_iota(jnp.int32, (BLOCK, BLOCK), 0)
    col = lax.broadcasted_iota(jnp.int32, (BLOCK, BLOCK), 1)
    low = col < SB_DH
    pairs = range(SB_PAIRS)
    heads = range(SB_HEADS)

    def fold_stages(j_top, tiles, srcs, fresh):
        n = len(tiles)
        nrows = [t[1] for t in tiles]
        assert all(nr == BLOCK for nr in nrows[:-1])
        masks = []
        for i, (kind, nr) in enumerate(tiles):
            assert kind is None or nr == BLOCK
            if kind == "diag":
                m = col < row
            elif kind == "general":
                kpos = (j_top - i) * BLOCK + col
                m = jnp.logical_and(kpos < c * BLOCK + row, kpos >= PAD)
            elif kind == "late_rows":
                m = row >= FAR_ROWS
            else:
                m = None
            masks.append(m)
        z = {}
        for p in pairs:
            kt = jnp.concatenate([srcs[i][_blk(p), :] for i in range(n)],
                                 axis=1)
            zz = _dot(rec_ref[REC_QN + 2 * p * BLOCK:REC_QN + 2 * (p + 1) * BLOCK, :],
                      kt)
            for a in range(2):
                for i in range(n):
                    z[2 * p + a, i] = zz[_blk(a), _blk(i)][:nrows[i]]
        yield
        order = [(hd, i) for hd in heads for i in range(n)]
        packed = []
        for hd, i in order:
            sp = _softplus2(z[hd, i])
            if masks[i] is not None:
                sp = jnp.where(masks[i], sp, 0.0)
            packed.append(jnp.concatenate(_split2(sp), axis=1))
        r = _dot(jnp.concatenate(packed, axis=0), w_ref[...])
        offs = np.cumsum([0] + [nrows[i] for _, i in order])
        yield
        w = {}
        cmax = cmax_rest = None
        for hd in heads:
            crep = None if fresh else crep_ref[hd]
            for i in range(n):
                nr = nrows[i]
                o0 = int(offs[hd * n + i])
                rh = r[o0:o0 + nr]
                arg = z[hd, i] - rh[:, :BLOCK]
                wh = jnp.exp2(arg if crep is None else arg + crep[:nr])
                if masks[i] is not None:
                    wh = jnp.where(masks[i], wh, 0.0)
                w[hd, i] = wh.astype(BF16)
                if crep is None:
                    crep = -rh[:, BLOCK:]
                elif nr == BLOCK:
                    crep = crep - rh[:, BLOCK:]
                else:
                    rest = crep[nr:]
                    cmax_rest = (rest if cmax_rest is None
                                 else jnp.maximum(cmax_rest, rest))
                    crep = jnp.concatenate([crep[:nr] - rh[:, BLOCK:], rest], axis=0)
            crep_ref[hd] = crep
            cmax = crep if cmax is None else jnp.maximum(cmax, crep)
        yield
        full = [i for i in range(n) if nrows[i] == BLOCK]
        for p in pairs:
            wp = jnp.concatenate(
                [jnp.concatenate([w[2 * p + a, i] for i in full], axis=1)
                 for a in range(2)], axis=0)
            vtile = lambda i: srcs[i][SB_W + p * BLOCK:SB_W + (p + 1) * BLOCK, :]
            pv = _dot(wp, jnp.concatenate([vtile(i) for i in full], axis=0))
            pv = jnp.where(low, pv[:BLOCK], pv[BLOCK:])
            if nrows[-1] < BLOCK:
                nr = nrows[-1]
                ps = _dot(jnp.concatenate([w[2 * p, n - 1], w[2 * p + 1, n - 1]],
                                          axis=0), vtile(n - 1))
                low_nr = lax.broadcasted_iota(jnp.int32, (nr, BLOCK), 1) < SB_DH
                ps = jnp.where(low_nr, ps[:nr], ps[nr:])
                pv = jnp.concatenate([pv[:nr] + ps, pv[nr:]], axis=0)
            if fresh:
                acc_ref[p] = pv
                o_ref[:, HG_W + p * BLOCK:HG_W + (p + 1) * BLOCK] = pv.astype(BF16)
            else:
                acc_ref[p] = acc_ref[p] + pv
        return (jnp.max(cmax), jnp.float32(-jnp.inf) if cmax_rest is None
                else jnp.max(cmax_rest))

    n_fast = 3
    fast = c >= n_fast
    win = [kv_now, ring.at[lax.rem(c + 2, 3)], ring.at[lax.rem(c + 1, 3)]]

    def first_fold(tiles):
        return lambda: _interleave(
            fold_stages(c, tiles, win[:len(tiles)], True), hgrn())[0]

    cmax0, cmax_rest = lax.cond(
        fast,
        first_fold((("diag", BLOCK), (None, BLOCK), (None, FAR_ROWS))),
        first_fold((("general", BLOCK),)))

    def fold(j_top, kind):
        pj = lax.rem(j_top + n_real_blk, n_blk)
        cp = pltpu.make_async_copy(rec_hbm.at[pj, REC_KT:REC_HV], kv_scr, sem.at[0])
        cp.start()
        cp.wait()
        return _interleave(fold_stages(j_top, ((kind, BLOCK),), [kv_scr],
                                       False))[0][0]

    @pl.when(cmax0 > EXP2_ZERO_BELOW)
    def _():
        cmax1 = lax.cond(cmax_rest > EXP2_ZERO_BELOW,
                         lambda: fold(c - (n_fast - 1), "late_rows"),
                         lambda: cmax0)

        def cond(carry):
            j, cmax = carry
            return jnp.logical_and(j >= 1, cmax > EXP2_ZERO_BELOW)

        def body(carry):
            j, _ = carry
            return j - 1, fold(j, None)

        j0 = jnp.where(fast, c - n_fast, c - 1)
        j_end, cmax_end = lax.while_loop(cond, body, (j0, cmax1))

        @pl.when(jnp.logical_and(j_end == 0, cmax_end > EXP2_ZERO_BELOW))
        def _():
            fold(0, "general")

        for p in pairs:
            o_ref[:, HG_W + p * BLOCK:HG_W + (p + 1) * BLOCK] = (
                acc_ref[p].astype(BF16))


def _mixer(hf, rec, out_gain, n_real_blk):
    n_blk = rec.shape[0]
    lp = n_blk * BLOCK
    phys = lambda c: (c + n_real_blk) % n_blk
    whole = lambda a: pl.BlockSpec(a.shape, lambda c: (0,) * a.ndim)
    msum = jnp.asarray(_hgrn_sum_matrix(), dtype=BF16)
    wsum = jnp.asarray(_sb_sum_matrix(), dtype=BF16)
    return pl.pallas_call(
        functools.partial(_mixer_kernel, n_real_blk=n_real_blk),
        grid=(n_blk,),
        in_specs=[pl.BlockSpec((BLOCK, 4 * HG_W), lambda c: (phys(c), 0)),
                  pl.BlockSpec((1, REC_ROWS, BLOCK), lambda c: (phys(c), 0, 0)),
                  pl.BlockSpec(memory_space=pl.ANY),
                  whole(out_gain), whole(msum), whole(wsum)],
        out_specs=pl.BlockSpec((BLOCK, HG_W + SB_W), lambda c: (phys(c), 0)),
        out_shape=jax.ShapeDtypeStruct((lp, HG_W + SB_W), BF16),
        scratch_shapes=[pltpu.VMEM((HG_HEADS, HG_DV, HG_DK), F32),
                        pltpu.VMEM((SB_PAIRS, BLOCK, BLOCK), F32),
                        pltpu.VMEM((SB_HEADS, BLOCK, BLOCK), F32),
                        pltpu.VMEM((3, KV_ROWS, BLOCK), BF16),
                        pltpu.VMEM((KV_ROWS, BLOCK), BF16),
                        pltpu.SemaphoreType.DMA((1,))],
        compiler_params=pltpu.CompilerParams(
            dimension_semantics=("arbitrary",)),
        name="mixer",
    )(hf, rec, rec, out_gain, msum, wsum)


def _ffn_out_kernel(h1_ref, o_ref, wo_ref, g2_ref, w2i_ref, w2o_ref, out_ref,
                    act_ref):
    h2 = h1_ref[...] + _dot(o_ref[...], wo_ref[...])
    xn = _rms(h2, g2_ref[...]).astype(BF16)
    out_ref[...] = h2 + 0.5 * _swiglu(xn, w2i_ref, w2o_ref, act_ref)


def _ffn_out(h1, o, wo, g2, w2i, w2o, n_rows, tm):
    const = lambda shape: pl.BlockSpec(shape, lambda i: (0,) * len(shape),
                                       pipeline_mode=pl.Buffered(1))
    rows = lambda w: pl.BlockSpec((tm, w), lambda i: (i, 0))
    return pl.pallas_call(
        _ffn_out_kernel,
        grid=(n_rows // tm,),
        in_specs=[rows(D_MODEL), rows(HG_W + SB_W),
                  const((HG_W + SB_W, D_MODEL)),
                  const((1, D_MODEL)), const((D_MODEL, 2 * D_FF)),
                  const((D_FF, D_MODEL))],
        out_specs=rows(D_MODEL),
        out_shape=jax.ShapeDtypeStruct((n_rows, D_MODEL), F32),
        scratch_shapes=[pltpu.VMEM((tm, D_FF), BF16)],
        compiler_params=pltpu.CompilerParams(
            dimension_semantics=("arbitrary",), vmem_limit_bytes=VMEM_LIMIT),
        name="ffn_out",
    )(h1, o, wo, g2, w2i, w2o)


def kernel(x, meta_tokens, ffn1_norm, ffn1_w_in, ffn1_w_out, mix_norm, w_in,
           hgrn_lb_logits, hgrn_out_norm, sb_q_norm, sb_k_norm, w_out,
           ffn2_norm, ffn2_w_in, ffn2_w_out):
    b, seq, _ = x.shape
    assert b == 1 and seq % BLOCK == 0
    assert ffn1_norm.shape[0] == 1, "single layer"
    n_real_blk = seq // BLOCK
    tm = 512
    assert seq % tm == 0

    meta_tile = jnp.zeros((BLOCK, D_MODEL), x.dtype).at[PAD:].set(
        meta_tokens.astype(x.dtype))

    win = w_in[0].astype(BF16)
    o_hg, o_sq, o_sk, o_sv = 4 * HG_W, 4 * HG_W + SB_W, 4 * HG_W + 2 * SB_W, 4 * HG_W + 3 * SB_W
    whg, wq = win[:, :o_hg], win[:, o_hg:o_sq]
    wkt, wv = win[:, o_sq:o_sk].T, win[:, o_sk:o_sv]
    gk = jnp.tile(sb_k_norm[0], SB_HEADS).reshape(SB_W, 1)
    gq = jnp.tile(sb_q_norm[0], 2).reshape(1, BLOCK)

    h1, hf, rec, wo, w2i, w2o = _ffn_in(
        x[0], meta_tile, ffn1_norm, ffn1_w_in[0].astype(BF16),
        ffn1_w_out[0].astype(BF16), mix_norm, whg, wq, wkt, wv, gk, gq,
        hgrn_lb_logits, (w_out[0], ffn2_w_in[0], ffn2_w_out[0]), tm)
    o = _mixer(hf, rec, hgrn_out_norm, n_real_blk)
    out = _ffn_out(h1, o, wo, ffn2_norm, w2i, w2o, seq, 2 * tm)
    return out[None]
```

```python
import functools

import numpy as np
import jax
import jax.numpy as jnp
from jax import lax
from jax.experimental import pallas as pl
from jax.experimental.pallas import tpu as pltpu

F32 = jnp.float32
BF16 = jnp.bfloat16

D_MODEL = 1024
N_META = 16
BLOCK = 128
PAD = (-N_META) % BLOCK
HG_HEADS = 4
HG_DK = 128
HG_DV = 128
HG_W = HG_HEADS * HG_DK
SB_HEADS = 8
SB_DH = 64
SB_W = SB_HEADS * SB_DH
SB_PAIRS = SB_HEADS // 2
D_FF = 2816
RMS_EPS = 1e-6
FF_CHUNK = 256
N_LEVELS = 7
SUBLANES = 8
LOG2E = 1.4426950408889634
EXP2_ZERO_BELOW = -150.0
FAR_ROWS = 48
REC_QN = 0
REC_KT = REC_QN + SB_HEADS * BLOCK
REC_V = REC_KT + SB_W
REC_HV = REC_V + SB_PAIRS * BLOCK
REC_ROWS = REC_HV + HG_HEADS * BLOCK
KV_ROWS = REC_HV - REC_KT
VMEM_LIMIT = 56 * 1024 * 1024
NT_DIMS = (((1,), (1,)), ((), ()))
TN_DIMS = (((0,), (0,)), ((), ()))


def _dot(a, b):
    return jnp.dot(a, b, preferred_element_type=F32)


def _dot_nt(a, b):
    return lax.dot_general(a, b, NT_DIMS, preferred_element_type=F32)


def _rms(x, gain):
    ms = jnp.mean(x * x, axis=-1, keepdims=True)
    return x * lax.rsqrt(ms + RMS_EPS) * gain


def _split2(x):
    hi = x.astype(BF16)
    lo = (x - hi.astype(F32)).astype(BF16)
    return hi, lo


def _neg_abs(x):
    return -jnp.abs(x)


def _silu(x):
    return x * jax.nn.sigmoid(x)


def _blk(i):
    return slice(i * BLOCK, (i + 1) * BLOCK)


def _swiglu_stages(xn, w_in_ref, w_out_ref, act_ref):
    for c in range(D_FF // FF_CHUNK):
        lo, hi = c * FF_CHUNK, (c + 1) * FF_CHUNK
        g = _dot(xn, w_in_ref[:, lo:hi])
        u = _dot(xn, w_in_ref[:, D_FF + lo:D_FF + hi])
        act_ref[:, lo:hi] = (_silu(g) * u).astype(BF16)
        yield
    return _dot(act_ref[...], w_out_ref[...])


def _interleave(*stage_lists):
    results = [None] * len(stage_lists)
    active = dict(enumerate(stage_lists))
    while active:
        for i, g in list(active.items()):
            try:
                next(g)
            except StopIteration as stop:
                results[i] = stop.value
                del active[i]
    return results


def _swiglu(xn, w_in_ref, w_out_ref, act_ref):
    return _interleave(_swiglu_stages(xn, w_in_ref, w_out_ref, act_ref))[0]


def _ffn_in_tile(h, pads, g1_ref, w1i_ref, w1o_ref, gm_ref, whg_ref, wq_ref,
                 wkt_ref, wv_ref, gk_ref, gq_ref, lbl_ref,
                 h1_ref, hf_ref, rec_ref, act_ref):
    n = h.shape[0]
    rows = slice(0, n)
    xn = _rms(h, g1_ref[...]).astype(BF16)
    h1 = h + 0.5 * _swiglu(xn, w1i_ref, w1o_ref, act_ref.at[rows])
    h1_ref[rows] = h1
    xm = _rms(h1, gm_ref[...]).astype(BF16)

    def put(base, piece, val):
        for t in range(n // BLOCK):
            lo = base + piece * BLOCK
            rec_ref[t, lo:lo + BLOCK, :] = val[_blk(t)]

    uhg = _dot(xm, whg_ref[...])
    hf_ref[rows, 0:HG_W] = _silu(uhg[:, 0:HG_W])
    hv = uhg[:, 2 * HG_W:3 * HG_W].astype(BF16)
    for hd in range(HG_HEADS):
        put(REC_HV, hd, hv[:, _blk(hd)])
    hf_ref[rows, 3 * HG_W:4 * HG_W] = _silu(uhg[:, 3 * HG_W:4 * HG_W])
    lg = lbl_ref[...]
    e = jnp.exp(lg - jnp.max(lg, axis=0, keepdims=True))
    lb = e[0:1] / jnp.sum(e, axis=0, keepdims=True)
    z = uhg[:, HG_W:2 * HG_W]
    ez = jnp.exp(_neg_abs(z))
    rz = 1.0 / (1.0 + ez)
    erz = ez * rz
    pos = z >= 0.0
    lf = jnp.log2(lb + (1.0 - lb) * jnp.where(pos, rz, erz))
    k = (1.0 - lb) * jnp.where(pos, erz, rz)
    if pads:
        valid = lax.broadcasted_iota(jnp.int32, z.shape, 0) >= PAD
        lf, k = jnp.where(valid, lf, 0.0), jnp.where(valid, k, 0.0)
    hf_ref[rows, HG_W:2 * HG_W] = lf
    hf_ref[rows, 2 * HG_W:3 * HG_W] = k

    q = _dot(xm, wq_ref[...])
    low = lax.broadcasted_iota(jnp.int32, (n, BLOCK), 1) < SB_DH
    qscale = gq_ref[...] * (LOG2E / np.sqrt(np.float32(SB_DH)))
    for p in range(SB_PAIRS):
        qp = q[:, _blk(p)]
        for a in range(2):
            own = low if a == 0 else jnp.logical_not(low)
            ms = jnp.sum(jnp.where(own, qp * qp, 0.0), axis=-1,
                         keepdims=True) * (1.0 / SB_DH)
            qn = jnp.where(own, qp * lax.rsqrt(ms + RMS_EPS) * qscale, 0.0)
            put(REC_QN, 2 * p + a, qn.astype(BF16))
    v = _dot(xm, wv_ref[...]).astype(BF16)
    for p in range(SB_PAIRS):
        put(REC_V, p, v[:, _blk(p)])
    kt = _dot_nt(wkt_ref[...], xm)
    k3 = kt.reshape(SB_HEADS, SB_DH, n)
    ms = jnp.mean(k3 * k3, axis=1, keepdims=True)
    kn = (k3 * lax.rsqrt(ms + RMS_EPS)).reshape(SB_W, n) * gk_ref[...]
    kn = kn.astype(BF16)
    for t in range(n // BLOCK):
        rec_ref[t, REC_KT:REC_KT + SB_W, :] = kn[:, _blk(t)]


N_FFN_IN_PARAMS = 11


def _ffn_in_kernel(x_ref, meta_ref, *refs, cast_chunks):
    nc = len(cast_chunks)
    params = refs[:N_FFN_IN_PARAMS]
    cast_in = refs[N_FFN_IN_PARAMS:N_FFN_IN_PARAMS + nc]
    outs = refs[N_FFN_IN_PARAMS + nc:N_FFN_IN_PARAMS + nc + 3]
    cast_out = refs[N_FFN_IN_PARAMS + nc + 3:N_FFN_IN_PARAMS + 2 * nc + 3]
    act_ref = refs[-1]
    i = pl.program_id(0)
    is_meta = i == pl.num_programs(0) - 1

    for src, dst, chunks in zip(cast_in, cast_out, cast_chunks):
        @pl.when(i < chunks)
        def _():
            dst[...] = src[...].astype(BF16)

    @pl.when(jnp.logical_not(is_meta))
    def _():
        _ffn_in_tile(x_ref[...], False, *params, *outs, act_ref)

    @pl.when(is_meta)
    def _():
        _ffn_in_tile(meta_ref[...], True, *params, *outs, act_ref)


def _cast_chunks(n_rows, n_steps):
    return max(k for k in range(1, n_steps + 1)
               if n_rows % k == 0 and (n_rows // k) % 16 == 0)


def _ffn_in(x2d, meta_tile, g1, w1i, w1o, gm, whg, wq, wkt, wv, gk, gq, lbl,
            to_cast, tm):
    n_real_tiles = x2d.shape[0] // tm
    lp = x2d.shape[0] + BLOCK
    n_blk = lp // BLOCK
    tb = tm // BLOCK
    const = lambda shape: pl.BlockSpec(shape, lambda i: (0,) * len(shape),
                                       pipeline_mode=pl.Buffered(1))
    rows = lambda w: pl.BlockSpec((tm, w), lambda i: (i, 0))
    chunks = tuple(_cast_chunks(w.shape[0], n_real_tiles + 1) for w in to_cast)
    cast_specs = [
        pl.BlockSpec((w.shape[0] // k, w.shape[1]),
                     lambda i, k=k: (jnp.minimum(i, k - 1), 0))
        for w, k in zip(to_cast, chunks)]
    return pl.pallas_call(
        functools.partial(_ffn_in_kernel, cast_chunks=chunks),
        grid=(n_real_tiles + 1,),
        in_specs=[pl.BlockSpec((tm, D_MODEL),
                               lambda i: (jnp.minimum(i, n_real_tiles - 1), 0)),
                  const((BLOCK, D_MODEL)), const((1, D_MODEL)),
                  const((D_MODEL, 2 * D_FF)), const((D_FF, D_MODEL)),
                  const((1, D_MODEL)), const((D_MODEL, 4 * HG_W)),
                  const((D_MODEL, SB_W)), const((SB_W, D_MODEL)),
                  const((D_MODEL, SB_W)), const((SB_W, 1)),
                  const((1, BLOCK)), const((2, HG_W))] + cast_specs,
        out_specs=[rows(D_MODEL), rows(4 * HG_W),
                   pl.BlockSpec((tb, REC_ROWS, BLOCK), lambda i: (i, 0, 0))]
        + cast_specs,
        out_shape=[jax.ShapeDtypeStruct((lp, D_MODEL), F32),
                   jax.ShapeDtypeStruct((lp, 4 * HG_W), F32),
                   jax.ShapeDtypeStruct((n_blk, REC_ROWS, BLOCK), BF16)]
        + [jax.ShapeDtypeStruct(w.shape, BF16) for w in to_cast],
        scratch_shapes=[pltpu.VMEM((tm, D_FF), BF16)],
        compiler_params=pltpu.CompilerParams(
            dimension_semantics=("arbitrary",), vmem_limit_bytes=VMEM_LIMIT),
        name="ffn_in",
    )(x2d, meta_tile, g1, w1i, w1o, gm, whg, wq, wkt, wv, gk, gq, lbl, *to_cast)


def _hgrn_sum_matrix():
    t = np.arange(BLOCK)[:, None]
    j = np.arange(BLOCK)[None, :]
    mats = [(j <= t)]
    for lvl in range(N_LEVELS):
        c = 1 << lvl
        m = (t // (2 * c)) * (2 * c) + c
        upper = (t >= m) & (j >= m) & (j <= t)
        lower = (t < m) & (j > t) & (j <= m - 1)
        mats.append(upper | lower)
    m = np.concatenate(mats, axis=0).astype(np.float32)
    return np.concatenate([m, m], axis=1)


def _hgrn_stages(hf_ref, rec_ref, gain_ref, m_ref, o_ref, st_ref):
    heads = range(HG_HEADS)
    hs = lambda a, h: a[:, h * HG_DK:(h + 1) * HG_DK]

    q = hf_ref[:, 0:HG_W]
    k = hf_ref[:, 2 * HG_W:3 * HG_W]
    v = jnp.concatenate(
        [rec_ref[REC_HV + h * BLOCK:REC_HV + (h + 1) * BLOCK, :] for h in heads],
        axis=1)
    x = _dot(m_ref[...],
             jnp.concatenate(_split2(hf_ref[:, HG_W:2 * HG_W]), axis=0))
    yield
    bcum = x[0:BLOCK]
    b_last = bcum[BLOCK - 1:BLOCK]
    qe = (q * jnp.exp2(bcum)).astype(BF16)
    kd = (k * jnp.exp2(b_last - bcum)).astype(BF16)
    st_decay = jnp.exp2(b_last)

    row = lax.broadcasted_iota(jnp.int32, (BLOCK, BLOCK), 0)
    col = lax.broadcasted_iota(jnp.int32, (BLOCK, BLOCK), 1)
    qb, kb = q.astype(BF16), k.astype(BF16)
    diag = row == col
    attn = [jnp.where(diag, _dot_nt(hs(qb, h), hs(kb, h)), 0.0) for h in heads]
    rowf = lax.broadcasted_iota(jnp.int32, q.shape, 0)
    for lvl in range(N_LEVELS):
        if lvl % 2 == 1:
            yield
        half = 1 << lvl
        el = jnp.exp2(x[(lvl + 1) * BLOCK:(lvl + 2) * BLOCK])
        if half >= SUBLANES:
            qparts, kparts = [], []
            for b in range(BLOCK // half):
                rows = slice(b * half, (b + 1) * half)
                zero = jnp.zeros((half, q.shape[1]), F32)
                if b % 2 == 1:
                    qparts.append(q[rows] * el[rows])
                    kparts.append(zero)
                else:
                    qparts.append(zero)
                    kparts.append(k[rows] * el[rows])
            ql = jnp.concatenate(qparts, axis=0).astype(BF16)
            kl = jnp.concatenate(kparts, axis=0).astype(BF16)
        else:
            is_q = ((rowf >> lvl) & 1) == 1
            ql = jnp.where(is_q, q * el, 0.0).astype(BF16)
            kl = jnp.where(is_q, 0.0, k * el).astype(BF16)
        same = (row >> (lvl + 1)) == (col >> (lvl + 1))
        for h in heads:
            al = _dot_nt(hs(ql, h), hs(kl, h))
            if lvl + 1 < N_LEVELS:
                al = jnp.where(same, al, 0.0)
            attn[h] = attn[h] + al
    yield

    o = []
    for h in heads:
        st = st_ref[h]
        oh = _dot(attn[h].astype(BF16), hs(v, h))
        oh = oh + _dot_nt(hs(qe, h), st.astype(BF16))
        st_ref[h] = st * hs(st_decay, h) + lax.dot_general(
            hs(v, h), hs(kd, h), TN_DIMS, preferred_element_type=F32)
        o.append(oh)
    yield
    o = jnp.concatenate([_rms(o[h], hs(gain_ref[...], h)) for h in heads], axis=1)
    o_ref[:, 0:HG_W] = (o * hf_ref[:, 3 * HG_W:4 * HG_W]).astype(BF16)


def _sb_sum_matrix():
    j = np.arange(BLOCK)[:, None]
    s = np.arange(BLOCK)[None, :]
    w = np.concatenate([(j >= s), np.ones((BLOCK, BLOCK), bool)],
                       axis=1).astype(np.float32)
    return np.concatenate([w, w], axis=0)


def _softplus2(z):
    return jnp.maximum(z, 0.0) + jnp.log2(1.0 + jnp.exp2(_neg_abs(z)))


BLOCKS_PER_STEP = 2


def _mixer_kernel(hf_ref, rec_blk_ref, rec_hbm, gain_ref, m_ref, w_ref, o_ref,
                  st_ref, acc_ref, crep_ref, ring, kv_scr, sem, *, n_real_blk):
    n_blk = rec_hbm.shape[0]
    s = pl.program_id(0)
    subs = range(BLOCKS_PER_STEP)
    blk_of = lambda u: jnp.where(s == 0, 0, BLOCKS_PER_STEP * s - 1 + u)
    rec_of = lambda u: rec_blk_ref.at[u]
    kv_of = lambda u: rec_blk_ref.at[u, REC_KT:REC_HV]
    rows_of = lambda u: slice(u * BLOCK, (u + 1) * BLOCK)

    @pl.when(s == 0)
    def _():
        st_ref[...] = jnp.zeros_like(st_ref)

    def hgrn(u):
        return _hgrn_stages(hf_ref.at[rows_of(u)], rec_of(u), gain_ref, m_ref,
                            o_ref.at[rows_of(u)], st_ref)

    row = lax.broadcasted_iota(jnp.int32, (BLOCK, BLOCK), 0)
    col = lax.broadcasted_iota(jnp.int32, (BLOCK, BLOCK), 1)
    low = col < SB_DH
    pairs = range(SB_PAIRS)
    heads = range(SB_HEADS)

    def fold_stages(u, j_top, tiles, srcs, fresh):
        c = blk_of(u)
        rec_ref, o_u = rec_of(u), o_ref.at[rows_of(u)]
        n = len(tiles)
        nrows = [t[1] for t in tiles]
        assert all(nr == BLOCK for nr in nrows[:-1])
        masks = []
        for i, (kind, nr) in enumerate(tiles):
            assert kind is None or nr == BLOCK
            if kind == "diag":
                m = col < row
            elif kind == "general":
                kpos = (j_top - i) * BLOCK + col
                m = jnp.logical_and(kpos < c * BLOCK + row, kpos >= PAD)
            elif kind == "late_rows":
                m = row >= FAR_ROWS
            else:
                m = None
            masks.append(m)
        z = {}
        for p in pairs:
            kt = jnp.concatenate([srcs[i][_blk(p), :] for i in range(n)],
                                 axis=1)
            zz = _dot(rec_ref[REC_QN + 2 * p * BLOCK:REC_QN + 2 * (p + 1) * BLOCK, :],
                      kt)
            for a in range(2):
                for i in range(n):
                    z[2 * p + a, i] = zz[_blk(a), _blk(i)][:nrows[i]]
        yield
        order = [(hd, i) for hd in heads for i in range(n)]
        packed = []
        for hd, i in order:
            sp = _softplus2(z[hd, i])
            if masks[i] is not None:
                sp = jnp.where(masks[i], sp, 0.0)
            packed.append(jnp.concatenate(_split2(sp), axis=1))
        r = _dot(jnp.concatenate(packed, axis=0), w_ref[...])
        offs = np.cumsum([0] + [nrows[i] for _, i in order])
        yield
        w = {}
        cmax = cmax_rest = None
        for hd in heads:
            crep = None if fresh else crep_ref[u, hd]
            for i in range(n):
                nr = nrows[i]
                o0 = int(offs[hd * n + i])
                rh = r[o0:o0 + nr]
                arg = z[hd, i] - rh[:, :BLOCK]
                wh = jnp.exp2(arg if crep is None else arg + crep[:nr])
                if masks[i] is not None:
                    wh = jnp.where(masks[i], wh, 0.0)
                w[hd, i] = wh.astype(BF16)
                if crep is None:
                    crep = -rh[:, BLOCK:]
                elif nr == BLOCK:
                    crep = crep - rh[:, BLOCK:]
                else:
                    rest = crep[nr:]
                    cmax_rest = (rest if cmax_rest is None
                                 else jnp.maximum(cmax_rest, rest))
                    crep = jnp.concatenate([crep[:nr] - rh[:, BLOCK:], rest], axis=0)
            crep_ref[u, hd] = crep
            cmax = crep if cmax is None else jnp.maximum(cmax, crep)
        yield
        full = [i for i in range(n) if nrows[i] == BLOCK]
        for p in pairs:
            wp = jnp.concatenate(
                [jnp.concatenate([w[2 * p + a, i] for i in full], axis=1)
                 for a in range(2)], axis=0)
            vtile = lambda i: srcs[i][SB_W + p * BLOCK:SB_W + (p + 1) * BLOCK, :]
            pv = _dot(wp, jnp.concatenate([vtile(i) for i in full], axis=0))
            pv = jnp.where(low, pv[:BLOCK], pv[BLOCK:])
            if nrows[-1] < BLOCK:
                nr = nrows[-1]
                ps = _dot(jnp.concatenate([w[2 * p, n - 1], w[2 * p + 1, n - 1]],
                                          axis=0), vtile(n - 1))
                low_nr = lax.broadcasted_iota(jnp.int32, (nr, BLOCK), 1) < SB_DH
                ps = jnp.where(low_nr, ps[:nr], ps[nr:])
                pv = jnp.concatenate([pv[:nr] + ps, pv[nr:]], axis=0)
            if fresh:
                acc_ref[u, p] = pv
                o_u[:, HG_W + p * BLOCK:HG_W + (p + 1) * BLOCK] = pv.astype(BF16)
            else:
                acc_ref[u, p] = acc_ref[u, p] + pv
        return (jnp.max(cmax), jnp.float32(-jnp.inf) if cmax_rest is None
                else jnp.max(cmax_rest))

    n_fast = 3
    fast = s >= 2
    no_more = (jnp.float32(-jnp.inf),) * 2

    def fast_step():
        tiles = (("diag", BLOCK), (None, BLOCK), (None, FAR_ROWS))
        srcs = ([kv_of(0), ring.at[1], ring.at[0]],
                [kv_of(1), kv_of(0), ring.at[1]])
        res = _interleave(*[fold_stages(u, blk_of(u), tiles, srcs[u], True)
                            for u in subs], *[hgrn(u) for u in subs])
        return res[0] + res[1]

    def first_steps():
        def start(u):
            return lambda: _interleave(
                fold_stages(u, blk_of(u), (("general", BLOCK),), [kv_of(u)], True),
                hgrn(u))[0]
        res0 = start(0)()
        return res0 + lax.cond(s >= 1, start(1), lambda: no_more)

    first = lax.cond(fast, fast_step, first_steps)

    def fold(u, j_top, kind):
        pj = lax.rem(j_top + n_real_blk, n_blk)
        cp = pltpu.make_async_copy(rec_hbm.at[pj, REC_KT:REC_HV], kv_scr, sem.at[0])
        cp.start()
        cp.wait()
        return _interleave(fold_stages(u, j_top, ((kind, BLOCK),), [kv_scr],
                                       False))[0][0]

    for u in subs:
        cmax0, cmax_rest = first[2 * u], first[2 * u + 1]
        c = blk_of(u)

        @pl.when(cmax0 > EXP2_ZERO_BELOW)
        def _(u=u, c=c, cmax0=cmax0, cmax_rest=cmax_rest):
            cmax1 = lax.cond(cmax_rest > EXP2_ZERO_BELOW,
                             lambda: fold(u, c - (n_fast - 1), "late_rows"),
                             lambda: cmax0)

            def cond(carry):
                j, cmax = carry
                return jnp.logical_and(j >= 1, cmax > EXP2_ZERO_BELOW)

            def body(carry):
                j, _ = carry
                return j - 1, fold(u, j, None)

            j0 = jnp.where(fast, c - n_fast, c - 1)
            j_end, cmax_end = lax.while_loop(cond, body, (j0, cmax1))

            @pl.when(jnp.logical_and(j_end == 0, cmax_end > EXP2_ZERO_BELOW))
            def _():
                fold(u, 0, "general")

            for p in pairs:
                o_ref[rows_of(u), HG_W + p * BLOCK:HG_W + (p + 1) * BLOCK] = (
                    acc_ref[u, p].astype(BF16))

    @pl.when(s == 0)
    def _():
        ring[1] = kv_of(0)[...]

    @pl.when(s > 0)
    def _():
        for u in subs:
            ring[u] = kv_of(u)[...]


def _mixer(hf, rec, out_gain, n_real_blk):
    n_blk = rec.shape[0]
    lp = n_blk * BLOCK
    assert n_real_blk % BLOCKS_PER_STEP == 0 and n_blk == n_real_blk + 1
    n_steps = n_real_blk // BLOCKS_PER_STEP + 1
    tile = lambda s: (s + n_steps - 1) % n_steps
    rows = BLOCKS_PER_STEP * BLOCK
    whole = lambda a: pl.BlockSpec(a.shape, lambda s: (0,) * a.ndim)
    msum = jnp.asarray(_hgrn_sum_matrix(), dtype=BF16)
    wsum = jnp.asarray(_sb_sum_matrix(), dtype=BF16)
    return pl.pallas_call(
        functools.partial(_mixer_kernel, n_real_blk=n_real_blk),
        grid=(n_steps,),
        in_specs=[pl.BlockSpec((rows, 4 * HG_W), lambda s: (tile(s), 0)),
                  pl.BlockSpec((BLOCKS_PER_STEP, REC_ROWS, BLOCK),
                               lambda s: (tile(s), 0, 0)),
                  pl.BlockSpec(memory_space=pl.ANY),
                  whole(out_gain), whole(msum), whole(wsum)],
        out_specs=pl.BlockSpec((rows, HG_W + SB_W), lambda s: (tile(s), 0)),
        out_shape=jax.ShapeDtypeStruct((lp, HG_W + SB_W), BF16),
        scratch_shapes=[pltpu.VMEM((HG_HEADS, HG_DV, HG_DK), F32),
                        pltpu.VMEM((BLOCKS_PER_STEP, SB_PAIRS, BLOCK, BLOCK), F32),
                        pltpu.VMEM((BLOCKS_PER_STEP, SB_HEADS, BLOCK, BLOCK), F32),
                        pltpu.VMEM((BLOCKS_PER_STEP, KV_ROWS, BLOCK), BF16),
                        pltpu.VMEM((KV_ROWS, BLOCK), BF16),
                        pltpu.SemaphoreType.DMA((1,))],
        compiler_params=pltpu.CompilerParams(
            dimension_semantics=("arbitrary",)),
        name="mixer",
    )(hf, rec, rec, out_gain, msum, wsum)


def _ffn_out_kernel(h1_ref, o_ref, wo_ref, g2_ref, w2i_ref, w2o_ref, out_ref,
                    act_ref):
    h2 = h1_ref[...] + _dot(o_ref[...], wo_ref[...])
    xn = _rms(h2, g2_ref[...]).astype(BF16)
    out_ref[...] = h2 + 0.5 * _swiglu(xn, w2i_ref, w2o_ref, act_ref)


def _ffn_out(h1, o, wo, g2, w2i, w2o, n_rows, tm):
    const = lambda shape: pl.BlockSpec(shape, lambda i: (0,) * len(shape),
                                       pipeline_mode=pl.Buffered(1))
    rows = lambda w: pl.BlockSpec((tm, w), lambda i: (i, 0))
    return pl.pallas_call(
        _ffn_out_kernel,
        grid=(n_rows // tm,),
        in_specs=[rows(D_MODEL), rows(HG_W + SB_W),
                  const((HG_W + SB_W, D_MODEL)),
                  const((1, D_MODEL)), const((D_MODEL, 2 * D_FF)),
                  const((D_FF, D_MODEL))],
        out_specs=rows(D_MODEL),
        out_shape=jax.ShapeDtypeStruct((n_rows, D_MODEL), F32),
        scratch_shapes=[pltpu.VMEM((tm, D_FF), BF16)],
        compiler_params=pltpu.CompilerParams(
            dimension_semantics=("arbitrary",), vmem_limit_bytes=VMEM_LIMIT),
        name="ffn_out",
    )(h1, o, wo, g2, w2i, w2o)


def kernel(x, meta_tokens, ffn1_norm, ffn1_w_in, ffn1_w_out, mix_norm, w_in,
           hgrn_lb_logits, hgrn_out_norm, sb_q_norm, sb_k_norm, w_out,
           ffn2_norm, ffn2_w_in, ffn2_w_out):
    b, seq, _ = x.shape
    assert b == 1 and seq % BLOCK == 0
    assert ffn1_norm.shape[0] == 1, "single layer"
    n_real_blk = seq // BLOCK
    tm = 512
    assert seq % tm == 0

    meta_tile = jnp.zeros((BLOCK, D_MODEL), x.dtype).at[PAD:].set(
        meta_tokens.astype(x.dtype))

    win = w_in[0].astype(BF16)
    o_hg, o_sq, o_sk, o_sv = 4 * HG_W, 4 * HG_W + SB_W, 4 * HG_W + 2 * SB_W, 4 * HG_W + 3 * SB_W
    whg, wq = win[:, :o_hg], win[:, o_hg:o_sq]
    wkt, wv = win[:, o_sq:o_sk].T, win[:, o_sk:o_sv]
    gk = jnp.tile(sb_k_norm[0], SB_HEADS).reshape(SB_W, 1)
    gq = jnp.tile(sb_q_norm[0], 2).reshape(1, BLOCK)

    h1, hf, rec, wo, w2i, w2o = _ffn_in(
        x[0], meta_tile, ffn1_norm, ffn1_w_in[0].astype(BF16),
        ffn1_w_out[0].astype(BF16), mix_norm, whg, wq, wkt, wv, gk, gq,
        hgrn_lb_logits, (w_out[0], ffn2_w_in[0], ffn2_w_out[0]), tm)
    o = _mixer(hf, rec, hgrn_out_norm, n_real_blk)
    out = _ffn_out(h1, o, wo, ffn2_norm, w2i, w2o, seq, 2 * tm)
    return out[None]
```

```python
import functools

import numpy as np
import jax
import jax.numpy as jnp
from jax import lax
from jax.experimental import pallas as pl
from jax.experimental.pallas import tpu as pltpu

F32 = jnp.float32
BF16 = jnp.bfloat16

D_MODEL = 1024
N_META = 16
BLOCK = 128
PAD = (-N_META) % BLOCK
HG_HEADS = 4
HG_DK = 128
HG_DV = 128
HG_W = HG_HEADS * HG_DK
SB_HEADS = 8
SB_DH = 64
SB_W = SB_HEADS * SB_DH
SB_PAIRS = SB_HEADS // 2
D_FF = 2816
RMS_EPS = 1e-6
FF_CHUNK = 256
N_LEVELS = 7
SUBLANES = 8
LOG2E = 1.4426950408889634
EXP2_ZERO_BELOW = -150.0
FAR_ROWS = 48
REC_QN = 0
REC_KT = REC_QN + SB_HEADS * BLOCK
REC_V = REC_KT + SB_W
REC_HV = REC_V + SB_PAIRS * BLOCK
REC_ROWS = REC_HV + HG_HEADS * BLOCK
KV_ROWS = REC_HV - REC_KT
VMEM_LIMIT = 56 * 1024 * 1024
NT_DIMS = (((1,), (1,)), ((), ()))
TN_DIMS = (((0,), (0,)), ((), ()))


def _dot(a, b):
    return jnp.dot(a, b, preferred_element_type=F32)


def _dot_nt(a, b):
    return lax.dot_general(a, b, NT_DIMS, preferred_element_type=F32)


def _rms(x, gain):
    ms = jnp.mean(x * x, axis=-1, keepdims=True)
    return x * lax.rsqrt(ms + RMS_EPS) * gain


def _split2(x):
    hi = x.astype(BF16)
    lo = (x - hi.astype(F32)).astype(BF16)
    return hi, lo


def _neg_abs(x):
    return -jnp.abs(x)


def _silu(x):
    return x * jax.nn.sigmoid(x)


def _blk(i):
    return slice(i * BLOCK, (i + 1) * BLOCK)


def _swiglu_stages(xn, w_in_ref, w_out_ref, act_ref):
    for c in range(D_FF // FF_CHUNK):
        lo, hi = c * FF_CHUNK, (c + 1) * FF_CHUNK
        g = _dot(xn, w_in_ref[:, lo:hi])
        u = _dot(xn, w_in_ref[:, D_FF + lo:D_FF + hi])
        act_ref[:, lo:hi] = (_silu(g) * u).astype(BF16)
        yield
    return _dot(act_ref[...], w_out_ref[...])


def _interleave(*stage_lists):
    results = [None] * len(stage_lists)
    active = dict(enumerate(stage_lists))
    while active:
        for i, g in list(active.items()):
            try:
                next(g)
            except StopIteration as stop:
                results[i] = stop.value
                del active[i]
    return results


def _swiglu(xn, w_in_ref, w_out_ref, act_ref):
    return _interleave(_swiglu_stages(xn, w_in_ref, w_out_ref, act_ref))[0]


def _ffn_in_tile(h, pads, g1_ref, w1i_ref, w1o_ref, gm_ref, whg_ref, wq_ref,
                 wkt_ref, wv_ref, gk_ref, gq_ref, lbl_ref,
                 h1_ref, hf_ref, rec_ref, act_ref):
    n = h.shape[0]
    rows = slice(0, n)
    xn = _rms(h, g1_ref[...]).astype(BF16)
    h1 = h + 0.5 * _swiglu(xn, w1i_ref, w1o_ref, act_ref.at[rows])
    h1_ref[rows] = h1
    xm = _rms(h1, gm_ref[...]).astype(BF16)

    def put(base, piece, val):
        for t in range(n // BLOCK):
            lo = base + piece * BLOCK
            rec_ref[t, lo:lo + BLOCK, :] = val[_blk(t)]

    uhg = _dot(xm, whg_ref[...])
    hf_ref[rows, 0:HG_W] = _silu(uhg[:, 0:HG_W])
    hv = uhg[:, 2 * HG_W:3 * HG_W].astype(BF16)
    for hd in range(HG_HEADS):
        put(REC_HV, hd, hv[:, _blk(hd)])
    hf_ref[rows, 3 * HG_W:4 * HG_W] = _silu(uhg[:, 3 * HG_W:4 * HG_W])
    lg = lbl_ref[...]
    e = jnp.exp(lg - jnp.max(lg, axis=0, keepdims=True))
    lb = e[0:1] / jnp.sum(e, axis=0, keepdims=True)
    z = uhg[:, HG_W:2 * HG_W]
    ez = jnp.exp(_neg_abs(z))
    rz = 1.0 / (1.0 + ez)
    erz = ez * rz
    pos = z >= 0.0
    lf = jnp.log2(lb + (1.0 - lb) * jnp.where(pos, rz, erz))
    k = (1.0 - lb) * jnp.where(pos, erz, rz)
    if pads:
        valid = lax.broadcasted_iota(jnp.int32, z.shape, 0) >= PAD
        lf, k = jnp.where(valid, lf, 0.0), jnp.where(valid, k, 0.0)
    hf_ref[rows, HG_W:2 * HG_W] = lf
    hf_ref[rows, 2 * HG_W:3 * HG_W] = k

    q = _dot(xm, wq_ref[...])
    low = lax.broadcasted_iota(jnp.int32, (n, BLOCK), 1) < SB_DH
    qscale = gq_ref[...] * (LOG2E / np.sqrt(np.float32(SB_DH)))
    for p in range(SB_PAIRS):
        qp = q[:, _blk(p)]
        for a in range(2):
            own = low if a == 0 else jnp.logical_not(low)
            ms = jnp.sum(jnp.where(own, qp * qp, 0.0), axis=-1,
                         keepdims=True) * (1.0 / SB_DH)
            qn = jnp.where(own, qp * lax.rsqrt(ms + RMS_EPS) * qscale, 0.0)
            put(REC_QN, 2 * p + a, qn.astype(BF16))
    v = _dot(xm, wv_ref[...]).astype(BF16)
    for p in range(SB_PAIRS):
        put(REC_V, p, v[:, _blk(p)])
    kt = _dot_nt(wkt_ref[...], xm)
    k3 = kt.reshape(SB_HEADS, SB_DH, n)
    ms = jnp.mean(k3 * k3, axis=1, keepdims=True)
    kn = (k3 * lax.rsqrt(ms + RMS_EPS)).reshape(SB_W, n) * gk_ref[...]
    kn = kn.astype(BF16)
    for t in range(n // BLOCK):
        rec_ref[t, REC_KT:REC_KT + SB_W, :] = kn[:, _blk(t)]


N_FFN_IN_PARAMS = 11


def _ffn_in_kernel(x_ref, meta_ref, *refs, cast_chunks):
    nc = len(cast_chunks)
    params = refs[:N_FFN_IN_PARAMS]
    cast_in = refs[N_FFN_IN_PARAMS:N_FFN_IN_PARAMS + nc]
    outs = refs[N_FFN_IN_PARAMS + nc:N_FFN_IN_PARAMS + nc + 3]
    cast_out = refs[N_FFN_IN_PARAMS + nc + 3:N_FFN_IN_PARAMS + 2 * nc + 3]
    act_ref = refs[-1]
    i = pl.program_id(0)
    is_meta = i == pl.num_programs(0) - 1

    for src, dst, chunks in zip(cast_in, cast_out, cast_chunks):
        @pl.when(i < chunks)
        def _():
            dst[...] = src[...].astype(BF16)

    @pl.when(jnp.logical_not(is_meta))
    def _():
        _ffn_in_tile(x_ref[...], False, *params, *outs, act_ref)

    @pl.when(is_meta)
    def _():
        _ffn_in_tile(meta_ref[...], True, *params, *outs, act_ref)


def _cast_chunks(n_rows, n_steps):
    return max(k for k in range(1, n_steps + 1)
               if n_rows % k == 0 and (n_rows // k) % 16 == 0)


def _ffn_in(x2d, meta_tile, g1, w1i, w1o, gm, whg, wq, wkt, wv, gk, gq, lbl,
            to_cast, tm):
    n_real_tiles = x2d.shape[0] // tm
    lp = x2d.shape[0] + BLOCK
    n_blk = lp // BLOCK
    tb = tm // BLOCK
    const = lambda shape: pl.BlockSpec(shape, lambda i: (0,) * len(shape),
                                       pipeline_mode=pl.Buffered(1))
    rows = lambda w: pl.BlockSpec((tm, w), lambda i: (i, 0))
    chunks = tuple(_cast_chunks(w.shape[0], n_real_tiles + 1) for w in to_cast)
    cast_specs = [
        pl.BlockSpec((w.shape[0] // k, w.shape[1]),
                     lambda i, k=k: (jnp.minimum(i, k - 1), 0))
        for w, k in zip(to_cast, chunks)]
    return pl.pallas_call(
        functools.partial(_ffn_in_kernel, cast_chunks=chunks),
        grid=(n_real_tiles + 1,),
        in_specs=[pl.BlockSpec((tm, D_MODEL),
                               lambda i: (jnp.minimum(i, n_real_tiles - 1), 0)),
                  const((BLOCK, D_MODEL)), const((1, D_MODEL)),
                  const((D_MODEL, 2 * D_FF)), const((D_FF, D_MODEL)),
                  const((1, D_MODEL)), const((D_MODEL, 4 * HG_W)),
                  const((D_MODEL, SB_W)), const((SB_W, D_MODEL)),
                  const((D_MODEL, SB_W)), const((SB_W, 1)),
                  const((1, BLOCK)), const((2, HG_W))] + cast_specs,
        out_specs=[rows(D_MODEL), rows(4 * HG_W),
                   pl.BlockSpec((tb, REC_ROWS, BLOCK), lambda i: (i, 0, 0))]
        + cast_specs,
        out_shape=[jax.ShapeDtypeStruct((lp, D_MODEL), F32),
                   jax.ShapeDtypeStruct((lp, 4 * HG_W), F32),
                   jax.ShapeDtypeStruct((n_blk, REC_ROWS, BLOCK), BF16)]
        + [jax.ShapeDtypeStruct(w.shape, BF16) for w in to_cast],
        scratch_shapes=[pltpu.VMEM((tm, D_FF), BF16)],
        compiler_params=pltpu.CompilerParams(
            dimension_semantics=("arbitrary",), vmem_limit_bytes=VMEM_LIMIT),
        name="ffn_in",
    )(x2d, meta_tile, g1, w1i, w1o, gm, whg, wq, wkt, wv, gk, gq, lbl, *to_cast)


def _hgrn_sum_matrix():
    t = np.arange(BLOCK)[:, None]
    j = np.arange(BLOCK)[None, :]
    mats = [(j <= t)]
    for lvl in range(N_LEVELS):
        c = 1 << lvl
        m = (t // (2 * c)) * (2 * c) + c
        upper = (t >= m) & (j >= m) & (j <= t)
        lower = (t < m) & (j > t) & (j <= m - 1)
        mats.append(upper | lower)
    m = np.concatenate(mats, axis=0).astype(np.float32)
    return np.concatenate([m, m], axis=1)


def _hgrn_stages(hf_ref, rec_ref, gain_ref, m_ref, o_ref, st_ref):
    heads = range(HG_HEADS)
    hs = lambda a, h: a[:, h * HG_DK:(h + 1) * HG_DK]

    q = hf_ref[:, 0:HG_W]
    k = hf_ref[:, 2 * HG_W:3 * HG_W]
    v = jnp.concatenate(
        [rec_ref[REC_HV + h * BLOCK:REC_HV + (h + 1) * BLOCK, :] for h in heads],
        axis=1)
    x = _dot(m_ref[...],
             jnp.concatenate(_split2(hf_ref[:, HG_W:2 * HG_W]), axis=0))
    yield
    bcum = x[0:BLOCK]
    b_last = bcum[BLOCK - 1:BLOCK]
    qe = (q * jnp.exp2(bcum)).astype(BF16)
    kd = (k * jnp.exp2(b_last - bcum)).astype(BF16)
    st_decay = jnp.exp2(b_last)

    row = lax.broadcasted_iota(jnp.int32, (BLOCK, BLOCK), 0)
    col = lax.broadcasted_iota(jnp.int32, (BLOCK, BLOCK), 1)
    qb, kb = q.astype(BF16), k.astype(BF16)
    diag = row == col
    attn = [jnp.where(diag, _dot_nt(hs(qb, h), hs(kb, h)), 0.0) for h in heads]
    rowf = lax.broadcasted_iota(jnp.int32, q.shape, 0)
    for lvl in range(N_LEVELS):
        if lvl % 2 == 1:
            yield
        half = 1 << lvl
        el = jnp.exp2(x[(lvl + 1) * BLOCK:(lvl + 2) * BLOCK])
        if half >= SUBLANES:
            qparts, kparts = [], []
            for b in range(BLOCK // half):
                rows = slice(b * half, (b + 1) * half)
                zero = jnp.zeros((half, q.shape[1]), F32)
                if b % 2 == 1:
                    qparts.append(q[rows] * el[rows])
                    kparts.append(zero)
                else:
                    qparts.append(zero)
                    kparts.append(k[rows] * el[rows])
            ql = jnp.concatenate(qparts, axis=0).astype(BF16)
            kl = jnp.concatenate(kparts, axis=0).astype(BF16)
        else:
            is_q = ((rowf >> lvl) & 1) == 1
            ql = jnp.where(is_q, q * el, 0.0).astype(BF16)
            kl = jnp.where(is_q, 0.0, k * el).astype(BF16)
        same = (row >> (lvl + 1)) == (col >> (lvl + 1))
        for h in heads:
            al = _dot_nt(hs(ql, h), hs(kl, h))
            if lvl + 1 < N_LEVELS:
                al = jnp.where(same, al, 0.0)
            attn[h] = attn[h] + al
    yield

    o = []
    for h in heads:
        st = st_ref[h]
        oh = _dot(attn[h].astype(BF16), hs(v, h))
        oh = oh + _dot_nt(hs(qe, h), st.astype(BF16))
        st_ref[h] = st * hs(st_decay, h) + lax.dot_general(
            hs(v, h), hs(kd, h), TN_DIMS, preferred_element_type=F32)
        o.append(oh)
    yield
    o = jnp.concatenate([_rms(o[h], hs(gain_ref[...], h)) for h in heads], axis=1)
    o_ref[:, 0:HG_W] = (o * hf_ref[:, 3 * HG_W:4 * HG_W]).astype(BF16)


def _sb_sum_matrix():
    j = np.arange(BLOCK)[:, None]
    s = np.arange(BLOCK)[None, :]
    w = np.concatenate([(j >= s), np.ones((BLOCK, BLOCK), bool)],
                       axis=1).astype(np.float32)
    return np.concatenate([w, w], axis=0)


def _softplus2(z):
    return jnp.maximum(z, 0.0) + jnp.log2(1.0 + jnp.exp2(_neg_abs(z)))


BLOCKS_PER_STEP = 4


def _mixer_kernel(hf_ref, rec_blk_ref, rec_hbm, gain_ref, m_ref, w_ref, o_ref,
                  st_ref, acc_ref, crep_ref, ring, kv_scr, sem, *, n_real_blk):
    n_blk = rec_hbm.shape[0]
    s = pl.program_id(0)
    subs = range(BLOCKS_PER_STEP)
    blk_of = lambda u: jnp.where(s == 0, 0, BLOCKS_PER_STEP * (s - 1) + 1 + u)
    rec_of = lambda u: rec_blk_ref.at[u]
    kv_of = lambda u: rec_blk_ref.at[u, REC_KT:REC_HV]
    rows_of = lambda u: slice(u * BLOCK, (u + 1) * BLOCK)

    @pl.when(s == 0)
    def _():
        st_ref[...] = jnp.zeros_like(st_ref)

    def hgrn(u):
        return _hgrn_stages(hf_ref.at[rows_of(u)], rec_of(u), gain_ref, m_ref,
                            o_ref.at[rows_of(u)], st_ref)

    row = lax.broadcasted_iota(jnp.int32, (BLOCK, BLOCK), 0)
    col = lax.broadcasted_iota(jnp.int32, (BLOCK, BLOCK), 1)
    low = col < SB_DH
    pairs = range(SB_PAIRS)
    heads = range(SB_HEADS)

    def fold_stages(u, j_top, tiles, srcs, fresh):
        c = blk_of(u)
        rec_ref, o_u = rec_of(u), o_ref.at[rows_of(u)]
        n = len(tiles)
        nrows = [t[1] for t in tiles]
        assert all(nr == BLOCK for nr in nrows[:-1])
        masks = []
        for i, (kind, nr) in enumerate(tiles):
            assert kind is None or nr == BLOCK
            if kind == "diag":
                m = col < row
            elif kind == "general":
                kpos = (j_top - i) * BLOCK + col
                m = jnp.logical_and(kpos < c * BLOCK + row, kpos >= PAD)
            elif kind == "late_rows":
                m = row >= FAR_ROWS
            else:
                m = None
            masks.append(m)
        z = {}
        for p in pairs:
            kt = jnp.concatenate([srcs[i][_blk(p), :] for i in range(n)],
                                 axis=1)
            zz = _dot(rec_ref[REC_QN + 2 * p * BLOCK:REC_QN + 2 * (p + 1) * BLOCK, :],
                      kt)
            for a in range(2):
                for i in range(n):
                    z[2 * p + a, i] = zz[_blk(a), _blk(i)][:nrows[i]]
        yield
        order = [(hd, i) for hd in heads for i in range(n)]
        packed = []
        for hd, i in order:
            sp = _softplus2(z[hd, i])
            if masks[i] is not None:
                sp = jnp.where(masks[i], sp, 0.0)
            packed.append(jnp.concatenate(_split2(sp), axis=1))
        r = _dot(jnp.concatenate(packed, axis=0), w_ref[...])
        offs = np.cumsum([0] + [nrows[i] for _, i in order])
        yield
        w = {}
        cmax = cmax_rest = None
        for hd in heads:
            crep = None if fresh else crep_ref[u, hd]
            for i in range(n):
                nr = nrows[i]
                o0 = int(offs[hd * n + i])
                rh = r[o0:o0 + nr]
                arg = z[hd, i] - rh[:, :BLOCK]
                wh = jnp.exp2(arg if crep is None else arg + crep[:nr])
                if masks[i] is not None:
                    wh = jnp.where(masks[i], wh, 0.0)
                w[hd, i] = wh.astype(BF16)
                if crep is None:
                    crep = -rh[:, BLOCK:]
                elif nr == BLOCK:
                    crep = crep - rh[:, BLOCK:]
                else:
                    rest = crep[nr:]
                    cmax_rest = (rest if cmax_rest is None
                                 else jnp.maximum(cmax_rest, rest))
                    crep = jnp.concatenate([crep[:nr] - rh[:, BLOCK:], rest], axis=0)
            crep_ref[u, hd] = crep
            cmax = crep if cmax is None else jnp.maximum(cmax, crep)
        yield
        full = [i for i in range(n) if nrows[i] == BLOCK]
        for p in pairs:
            wp = jnp.concatenate(
                [jnp.concatenate([w[2 * p + a, i] for i in full], axis=1)
                 for a in range(2)], axis=0)
            vtile = lambda i: srcs[i][SB_W + p * BLOCK:SB_W + (p + 1) * BLOCK, :]
            pv = _dot(wp, jnp.concatenate([vtile(i) for i in full], axis=0))
            pv = jnp.where(low, pv[:BLOCK], pv[BLOCK:])
            if nrows[-1] < BLOCK:
                nr = nrows[-1]
                ps = _dot(jnp.concatenate([w[2 * p, n - 1], w[2 * p + 1, n - 1]],
                                          axis=0), vtile(n - 1))
                low_nr = lax.broadcasted_iota(jnp.int32, (nr, BLOCK), 1) < SB_DH
                ps = jnp.where(low_nr, ps[:nr], ps[nr:])
                pv = jnp.concatenate([pv[:nr] + ps, pv[nr:]], axis=0)
            if fresh:
                acc_ref[u, p] = pv
                o_u[:, HG_W + p * BLOCK:HG_W + (p + 1) * BLOCK] = pv.astype(BF16)
            else:
                acc_ref[u, p] = acc_ref[u, p] + pv
        return (jnp.max(cmax), jnp.float32(-jnp.inf) if cmax_rest is None
                else jnp.max(cmax_rest))

    n_fast = 3
    fast = s >= 2
    no_more = (jnp.float32(-jnp.inf),) * 2

    def fast_step():
        tiles = (("diag", BLOCK), (None, BLOCK), (None, FAR_ROWS))
        back = lambda u, i: kv_of(u - i) if u >= i else ring.at[u - i + 2]
        res = _interleave(
            *[fold_stages(u, blk_of(u), tiles, [back(u, i) for i in range(3)], True)
              for u in subs], *[hgrn(u) for u in subs])
        return sum(res[:BLOCKS_PER_STEP], ())

    def first_steps():
        def start(u):
            return lambda: _interleave(
                fold_stages(u, blk_of(u), (("general", BLOCK),), [kv_of(u)], True),
                hgrn(u))[0]
        res = start(0)()
        for u in subs[1:]:
            res = res + lax.cond(s >= 1, start(u), lambda: no_more)
        return res

    first = lax.cond(fast, fast_step, first_steps)

    def fold(u, j_top, kind):
        pj = lax.rem(j_top + n_real_blk, n_blk)
        cp = pltpu.make_async_copy(rec_hbm.at[pj, REC_KT:REC_HV], kv_scr, sem.at[0])
        cp.start()
        cp.wait()
        return _interleave(fold_stages(u, j_top, ((kind, BLOCK),), [kv_scr],
                                       False))[0][0]

    for u in subs:
        cmax0, cmax_rest = first[2 * u], first[2 * u + 1]
        c = blk_of(u)

        @pl.when(cmax0 > EXP2_ZERO_BELOW)
        def _(u=u, c=c, cmax0=cmax0, cmax_rest=cmax_rest):
            cmax1 = lax.cond(cmax_rest > EXP2_ZERO_BELOW,
                             lambda: fold(u, c - (n_fast - 1), "late_rows"),
                             lambda: cmax0)

            def cond(carry):
                j, cmax = carry
                return jnp.logical_and(j >= 1, cmax > EXP2_ZERO_BELOW)

            def body(carry):
                j, _ = carry
                return j - 1, fold(u, j, None)

            j0 = jnp.where(fast, c - n_fast, c - 1)
            j_end, cmax_end = lax.while_loop(cond, body, (j0, cmax1))

            @pl.when(jnp.logical_and(j_end == 0, cmax_end > EXP2_ZERO_BELOW))
            def _():
                fold(u, 0, "general")

            for p in pairs:
                o_ref[rows_of(u), HG_W + p * BLOCK:HG_W + (p + 1) * BLOCK] = (
                    acc_ref[u, p].astype(BF16))

    @pl.when(s == 0)
    def _():
        ring[1] = kv_of(0)[...]

    @pl.when(s > 0)
    def _():
        for i in range(2):
            ring[i] = kv_of(BLOCKS_PER_STEP - 2 + i)[...]


def _mixer(hf, rec, out_gain, n_real_blk):
    n_blk = rec.shape[0]
    lp = n_blk * BLOCK
    assert n_real_blk % BLOCKS_PER_STEP == 0 and n_blk == n_real_blk + 1
    n_steps = n_real_blk // BLOCKS_PER_STEP + 1
    tile = lambda s: (s + n_steps - 1) % n_steps
    rows = BLOCKS_PER_STEP * BLOCK
    whole = lambda a: pl.BlockSpec(a.shape, lambda s: (0,) * a.ndim)
    msum = jnp.asarray(_hgrn_sum_matrix(), dtype=BF16)
    wsum = jnp.asarray(_sb_sum_matrix(), dtype=BF16)
    return pl.pallas_call(
        functools.partial(_mixer_kernel, n_real_blk=n_real_blk),
        grid=(n_steps,),
        in_specs=[pl.BlockSpec((rows, 4 * HG_W), lambda s: (tile(s), 0)),
                  pl.BlockSpec((BLOCKS_PER_STEP, REC_ROWS, BLOCK),
                               lambda s: (tile(s), 0, 0)),
                  pl.BlockSpec(memory_space=pl.ANY),
                  whole(out_gain), whole(msum), whole(wsum)],
        out_specs=pl.BlockSpec((rows, HG_W + SB_W), lambda s: (tile(s), 0)),
        out_shape=jax.ShapeDtypeStruct((lp, HG_W + SB_W), BF16),
        scratch_shapes=[pltpu.VMEM((HG_HEADS, HG_DV, HG_DK), F32),
                        pltpu.VMEM((BLOCKS_PER_STEP, SB_PAIRS, BLOCK, BLOCK), F32),
                        pltpu.VMEM((BLOCKS_PER_STEP, SB_HEADS, BLOCK, BLOCK), F32),
                        pltpu.VMEM((2, KV_ROWS, BLOCK), BF16),
                        pltpu.VMEM((KV_ROWS, BLOCK), BF16),
                        pltpu.SemaphoreType.DMA((1,))],
        compiler_params=pltpu.CompilerParams(
            dimension_semantics=("arbitrary",)),
        name="mixer",
    )(hf, rec, rec, out_gain, msum, wsum)


def _ffn_out_kernel(h1_ref, o_ref, wo_ref, g2_ref, w2i_ref, w2o_ref, out_ref,
                    act_ref):
    h2 = h1_ref[...] + _dot(o_ref[...], wo_ref[...])
    xn = _rms(h2, g2_ref[...]).astype(BF16)
    out_ref[...] = h2 + 0.5 * _swiglu(xn, w2i_ref, w2o_ref, act_ref)


def _ffn_out(h1, o, wo, g2, w2i, w2o, n_rows, tm):
    const = lambda shape: pl.BlockSpec(shape, lambda i: (0,) * len(shape),
                                       pipeline_mode=pl.Buffered(1))
    rows = lambda w: pl.BlockSpec((tm, w), lambda i: (i, 0))
    return pl.pallas_call(
        _ffn_out_kernel,
        grid=(n_rows // tm,),
        in_specs=[rows(D_MODEL), rows(HG_W + SB_W),
                  const((HG_W + SB_W, D_MODEL)),
                  const((1, D_MODEL)), const((D_MODEL, 2 * D_FF)),
                  const((D_FF, D_MODEL))],
        out_specs=rows(D_MODEL),
        out_shape=jax.ShapeDtypeStruct((n_rows, D_MODEL), F32),
        scratch_shapes=[pltpu.VMEM((tm, D_FF), BF16)],
        compiler_params=pltpu.CompilerParams(
            dimension_semantics=("arbitrary",), vmem_limit_bytes=VMEM_LIMIT),
        name="ffn_out",
    )(h1, o, wo, g2, w2i, w2o)


def kernel(x, meta_tokens, ffn1_norm, ffn1_w_in, ffn1_w_out, mix_norm, w_in,
           hgrn_lb_logits, hgrn_out_norm, sb_q_norm, sb_k_norm, w_out,
           ffn2_norm, ffn2_w_in, ffn2_w_out):
    b, seq, _ = x.shape
    assert b == 1 and seq % BLOCK == 0
    assert ffn1_norm.shape[0] == 1, "single layer"
    n_real_blk = seq // BLOCK
    tm = 512
    assert seq % tm == 0

    meta_tile = jnp.zeros((BLOCK, D_MODEL), x.dtype).at[PAD:].set(
        meta_tokens.astype(x.dtype))

    win = w_in[0].astype(BF16)
    o_hg, o_sq, o_sk, o_sv = 4 * HG_W, 4 * HG_W + SB_W, 4 * HG_W + 2 * SB_W, 4 * HG_W + 3 * SB_W
    whg, wq = win[:, :o_hg], win[:, o_hg:o_sq]
    wkt, wv = win[:, o_sq:o_sk].T, win[:, o_sk:o_sv]
    gk = jnp.tile(sb_k_norm[0], SB_HEADS).reshape(SB_W, 1)
    gq = jnp.tile(sb_q_norm[0], 2).reshape(1, BLOCK)

    h1, hf, rec, wo, w2i, w2o = _ffn_in(
        x[0], meta_tile, ffn1_norm, ffn1_w_in[0].astype(BF16),
        ffn1_w_out[0].astype(BF16), mix_norm, whg, wq, wkt, wv, gk, gq,
        hgrn_lb_logits, (w_out[0], ffn2_w_in[0], ffn2_w_out[0]), tm)
    o = _mixer(hf, rec, hgrn_out_norm, n_real_blk)
    out = _ffn_out(h1, o, wo, ffn2_norm, w2i, w2o, seq, 2 * tm)
    return out[None]
```

```python
import functools

import numpy as np
import jax
import jax.numpy as jnp
from jax import lax
from jax.experimental import pallas as pl
from jax.experimental.pallas import tpu as pltpu

F32 = jnp.float32
BF16 = jnp.bfloat16

D_MODEL = 1024
N_META = 16
BLOCK = 128
PAD = (-N_META) % BLOCK
HG_HEADS = 4
HG_DK = 128
HG_DV = 128
HG_W = HG_HEADS * HG_DK
SB_HEADS = 8
SB_DH = 64
SB_W = SB_HEADS * SB_DH
SB_PAIRS = SB_HEADS // 2
D_FF = 2816
RMS_EPS = 1e-6
FF_CHUNK = 256
N_LEVELS = 7
SUBLANES = 8
LOG2E = 1.4426950408889634
EXP2_ZERO_BELOW = -150.0
FAR_ROWS = 48
REC_QN = 0
REC_KT = REC_QN + SB_HEADS * BLOCK
REC_V = REC_KT + SB_W
REC_HV = REC_V + SB_PAIRS * BLOCK
REC_ROWS = REC_HV + HG_HEADS * BLOCK
KV_ROWS = REC_HV - REC_KT
VMEM_LIMIT = 56 * 1024 * 1024
NT_DIMS = (((1,), (1,)), ((), ()))
TN_DIMS = (((0,), (0,)), ((), ()))


def _dot(a, b):
    return jnp.dot(a, b, preferred_element_type=F32)


def _dot_nt(a, b):
    return lax.dot_general(a, b, NT_DIMS, preferred_element_type=F32)


def _rms(x, gain):
    ms = jnp.mean(x * x, axis=-1, keepdims=True)
    return x * lax.rsqrt(ms + RMS_EPS) * gain


def _split2(x):
    hi = x.astype(BF16)
    lo = (x - hi.astype(F32)).astype(BF16)
    return hi, lo


def _neg_abs(x):
    return -jnp.abs(x)


def _silu(x):
    return x * jax.nn.sigmoid(x)


def _blk(i):
    return slice(i * BLOCK, (i + 1) * BLOCK)


def _swiglu_stages(xn, w_in_ref, w_out_ref, act_ref):
    for c in range(D_FF // FF_CHUNK):
        lo, hi = c * FF_CHUNK, (c + 1) * FF_CHUNK
        g = _dot(xn, w_in_ref[:, lo:hi])
        u = _dot(xn, w_in_ref[:, D_FF + lo:D_FF + hi])
        act_ref[:, lo:hi] = (_silu(g) * u).astype(BF16)
        yield
    return _dot(act_ref[...], w_out_ref[...])


def _interleave(*stage_lists):
    results = [None] * len(stage_lists)
    active = dict(enumerate(stage_lists))
    while active:
        for i, g in list(active.items()):
            try:
                next(g)
            except StopIteration as stop:
                results[i] = stop.value
                del active[i]
    return results


def _swiglu(xn, w_in_ref, w_out_ref, act_ref):
    return _interleave(_swiglu_stages(xn, w_in_ref, w_out_ref, act_ref))[0]


def _ffn_in_tile(h, pads, g1_ref, w1i_ref, w1o_ref, gm_ref, whg_ref, wq_ref,
                 wkt_ref, wv_ref, gk_ref, gq_ref, lbl_ref,
                 h1_ref, hf_ref, rec_ref, act_ref):
    n = h.shape[0]
    rows = slice(0, n)
    xn = _rms(h, g1_ref[...]).astype(BF16)
    h1 = h + 0.5 * _swiglu(xn, w1i_ref, w1o_ref, act_ref.at[rows])
    h1_ref[rows] = h1
    xm = _rms(h1, gm_ref[...]).astype(BF16)

    def put(base, piece, val):
        for t in range(n // BLOCK):
            lo = base + piece * BLOCK
            rec_ref[t, lo:lo + BLOCK, :] = val[_blk(t)]

    q = _dot(xm, wq_ref[...])
    low = lax.broadcasted_iota(jnp.int32, (n, BLOCK), 1) < SB_DH
    qscale = gq_ref[...] * (LOG2E / np.sqrt(np.float32(SB_DH)))
    for p in range(SB_PAIRS):
        qp = q[:, _blk(p)]
        for a in range(2):
            own = low if a == 0 else jnp.logical_not(low)
            ms = jnp.sum(jnp.where(own, qp * qp, 0.0), axis=-1,
                         keepdims=True) * (1.0 / SB_DH)
            qn = jnp.where(own, qp * lax.rsqrt(ms + RMS_EPS) * qscale, 0.0)
            put(REC_QN, 2 * p + a, qn.astype(BF16))
    v = _dot(xm, wv_ref[...]).astype(BF16)
    for p in range(SB_PAIRS):
        put(REC_V, p, v[:, _blk(p)])
    kt = _dot_nt(wkt_ref[...], xm)
    k3 = kt.reshape(SB_HEADS, SB_DH, n)
    ms = jnp.mean(k3 * k3, axis=1, keepdims=True)
    kn = (k3 * lax.rsqrt(ms + RMS_EPS)).reshape(SB_W, n) * gk_ref[...]
    kn = kn.astype(BF16)
    for t in range(n // BLOCK):
        rec_ref[t, REC_KT:REC_KT + SB_W, :] = kn[:, _blk(t)]

    part = lambda i: _dot(xm, whg_ref[:, i * HG_W:(i + 1) * HG_W])
    hv = part(2).astype(BF16)
    for hd in range(HG_HEADS):
        put(REC_HV, hd, hv[:, _blk(hd)])
    hf_ref[rows, 0:HG_W] = _silu(part(0))
    hf_ref[rows, 3 * HG_W:4 * HG_W] = _silu(part(3))
    lg = lbl_ref[...]
    e = jnp.exp(lg - jnp.max(lg, axis=0, keepdims=True))
    lb = e[0:1] / jnp.sum(e, axis=0, keepdims=True)
    z = part(1)
    ez = jnp.exp(_neg_abs(z))
    rz = 1.0 / (1.0 + ez)
    erz = ez * rz
    pos = z >= 0.0
    lf = jnp.log2(lb + (1.0 - lb) * jnp.where(pos, rz, erz))
    k = (1.0 - lb) * jnp.where(pos, erz, rz)
    if pads:
        valid = lax.broadcasted_iota(jnp.int32, z.shape, 0) >= PAD
        lf, k = jnp.where(valid, lf, 0.0), jnp.where(valid, k, 0.0)
    hf_ref[rows, HG_W:2 * HG_W] = lf
    hf_ref[rows, 2 * HG_W:3 * HG_W] = k


N_FFN_IN_PARAMS = 11


def _ffn_in_kernel(x_ref, meta_ref, *refs, cast_chunks):
    nc = len(cast_chunks)
    params = refs[:N_FFN_IN_PARAMS]
    cast_in = refs[N_FFN_IN_PARAMS:N_FFN_IN_PARAMS + nc]
    outs = refs[N_FFN_IN_PARAMS + nc:N_FFN_IN_PARAMS + nc + 3]
    cast_out = refs[N_FFN_IN_PARAMS + nc + 3:N_FFN_IN_PARAMS + 2 * nc + 3]
    act_ref = refs[-1]
    i = pl.program_id(0)
    is_meta = i == pl.num_programs(0) - 1

    for src, dst, chunks in zip(cast_in, cast_out, cast_chunks):
        @pl.when(i < chunks)
        def _():
            dst[...] = src[...].astype(BF16)

    @pl.when(jnp.logical_not(is_meta))
    def _():
        _ffn_in_tile(x_ref[...], False, *params, *outs, act_ref)

    @pl.when(is_meta)
    def _():
        _ffn_in_tile(meta_ref[...], True, *params, *outs, act_ref)


def _cast_chunks(n_rows, n_steps):
    return max(k for k in range(1, n_steps + 1)
               if n_rows % k == 0 and (n_rows // k) % 16 == 0)


def _ffn_in(x2d, meta_tile, g1, w1i, w1o, gm, whg, wq, wkt, wv, gk, gq, lbl,
            to_cast, tm):
    n_real_tiles = x2d.shape[0] // tm
    lp = x2d.shape[0] + BLOCK
    n_blk = lp // BLOCK
    tb = tm // BLOCK
    const = lambda shape: pl.BlockSpec(shape, lambda i: (0,) * len(shape),
                                       pipeline_mode=pl.Buffered(1))
    rows = lambda w: pl.BlockSpec((tm, w), lambda i: (i, 0))
    chunks = tuple(_cast_chunks(w.shape[0], n_real_tiles + 1) for w in to_cast)
    cast_specs = [
        pl.BlockSpec((w.shape[0] // k, w.shape[1]),
                     lambda i, k=k: (jnp.minimum(i, k - 1), 0))
        for w, k in zip(to_cast, chunks)]
    return pl.pallas_call(
        functools.partial(_ffn_in_kernel, cast_chunks=chunks),
        grid=(n_real_tiles + 1,),
        in_specs=[pl.BlockSpec((tm, D_MODEL),
                               lambda i: (jnp.minimum(i, n_real_tiles - 1), 0)),
                  const((BLOCK, D_MODEL)), const((1, D_MODEL)),
                  const((D_MODEL, 2 * D_FF)), const((D_FF, D_MODEL)),
                  const((1, D_MODEL)), const((D_MODEL, 4 * HG_W)),
                  const((D_MODEL, SB_W)), const((SB_W, D_MODEL)),
                  const((D_MODEL, SB_W)), const((SB_W, 1)),
                  const((1, BLOCK)), const((2, HG_W))] + cast_specs,
        out_specs=[rows(D_MODEL), rows(4 * HG_W),
                   pl.BlockSpec((tb, REC_ROWS, BLOCK), lambda i: (i, 0, 0))]
        + cast_specs,
        out_shape=[jax.ShapeDtypeStruct((lp, D_MODEL), F32),
                   jax.ShapeDtypeStruct((lp, 4 * HG_W), F32),
                   jax.ShapeDtypeStruct((n_blk, REC_ROWS, BLOCK), BF16)]
        + [jax.ShapeDtypeStruct(w.shape, BF16) for w in to_cast],
        scratch_shapes=[pltpu.VMEM((tm, D_FF), BF16)],
        compiler_params=pltpu.CompilerParams(
            dimension_semantics=("arbitrary",), vmem_limit_bytes=VMEM_LIMIT),
        name="ffn_in",
    )(x2d, meta_tile, g1, w1i, w1o, gm, whg, wq, wkt, wv, gk, gq, lbl, *to_cast)


def _hgrn_sum_matrix():
    t = np.arange(BLOCK)[:, None]
    j = np.arange(BLOCK)[None, :]
    mats = [(j <= t)]
    for lvl in range(N_LEVELS):
        c = 1 << lvl
        m = (t // (2 * c)) * (2 * c) + c
        upper = (t >= m) & (j >= m) & (j <= t)
        lower = (t < m) & (j > t) & (j <= m - 1)
        mats.append(upper | lower)
    m = np.concatenate(mats, axis=0).astype(np.float32)
    return np.concatenate([m, m], axis=1)


def _hgrn_stages(hf_ref, rec_ref, gain_ref, m_ref, o_ref, st_ref):
    heads = range(HG_HEADS)
    hs = lambda a, h: a[:, h * HG_DK:(h + 1) * HG_DK]

    q = hf_ref[:, 0:HG_W]
    k = hf_ref[:, 2 * HG_W:3 * HG_W]
    v = jnp.concatenate(
        [rec_ref[REC_HV + h * BLOCK:REC_HV + (h + 1) * BLOCK, :] for h in heads],
        axis=1)
    x = _dot(m_ref[...],
             jnp.concatenate(_split2(hf_ref[:, HG_W:2 * HG_W]), axis=0))
    yield
    bcum = x[0:BLOCK]
    b_last = bcum[BLOCK - 1:BLOCK]
    qe = (q * jnp.exp2(bcum)).astype(BF16)
    kd = (k * jnp.exp2(b_last - bcum)).astype(BF16)
    st_decay = jnp.exp2(b_last)

    row = lax.broadcasted_iota(jnp.int32, (BLOCK, BLOCK), 0)
    col = lax.broadcasted_iota(jnp.int32, (BLOCK, BLOCK), 1)
    qb, kb = q.astype(BF16), k.astype(BF16)
    diag = row == col
    attn = [jnp.where(diag, _dot_nt(hs(qb, h), hs(kb, h)), 0.0) for h in heads]
    rowf = lax.broadcasted_iota(jnp.int32, q.shape, 0)
    for lvl in range(N_LEVELS):
        if lvl % 2 == 1:
            yield
        half = 1 << lvl
        el = jnp.exp2(x[(lvl + 1) * BLOCK:(lvl + 2) * BLOCK])
        if half >= SUBLANES:
            qparts, kparts = [], []
            for b in range(BLOCK // half):
                rows = slice(b * half, (b + 1) * half)
                zero = jnp.zeros((half, q.shape[1]), F32)
                if b % 2 == 1:
                    qparts.append(q[rows] * el[rows])
                    kparts.append(zero)
                else:
                    qparts.append(zero)
                    kparts.append(k[rows] * el[rows])
            ql = jnp.concatenate(qparts, axis=0).astype(BF16)
            kl = jnp.concatenate(kparts, axis=0).astype(BF16)
        else:
            is_q = ((rowf >> lvl) & 1) == 1
            ql = jnp.where(is_q, q * el, 0.0).astype(BF16)
            kl = jnp.where(is_q, 0.0, k * el).astype(BF16)
        same = (row >> (lvl + 1)) == (col >> (lvl + 1))
        for h in heads:
            al = _dot_nt(hs(ql, h), hs(kl, h))
            if lvl + 1 < N_LEVELS:
                al = jnp.where(same, al, 0.0)
            attn[h] = attn[h] + al
    yield

    o = []
    for h in heads:
        st = st_ref[h]
        oh = _dot(attn[h].astype(BF16), hs(v, h))
        oh = oh + _dot_nt(hs(qe, h), st.astype(BF16))
        st_ref[h] = st * hs(st_decay, h) + lax.dot_general(
            hs(v, h), hs(kd, h), TN_DIMS, preferred_element_type=F32)
        o.append(oh)
    yield
    o = jnp.concatenate([_rms(o[h], hs(gain_ref[...], h)) for h in heads], axis=1)
    o_ref[:, 0:HG_W] = (o * hf_ref[:, 3 * HG_W:4 * HG_W]).astype(BF16)


def _sb_sum_matrix():
    j = np.arange(BLOCK)[:, None]
    s = np.arange(BLOCK)[None, :]
    w = np.concatenate([(j >= s), np.ones((BLOCK, BLOCK), bool)],
                       axis=1).astype(np.float32)
    return np.concatenate([w, w], axis=0)


def _softplus2(z):
    return jnp.maximum(z, 0.0) + jnp.log2(1.0 + jnp.exp2(_neg_abs(z)))


BLOCKS_PER_STEP = 2


def _mixer_kernel(hf_ref, rec_blk_ref, rec_hbm, gain_ref, m_ref, w_ref, o_ref,
                  st_ref, acc_ref, crep_ref, ring, kv_scr, sem, *, n_real_blk):
    n_blk = rec_hbm.shape[0]
    s = pl.program_id(0)
    subs = range(BLOCKS_PER_STEP)
    blk_of = lambda u: jnp.where(s == 0, 0, BLOCKS_PER_STEP * (s - 1) + 1 + u)
    rec_of = lambda u: rec_blk_ref.at[u]
    kv_of = lambda u: rec_blk_ref.at[u, REC_KT:REC_HV]
    rows_of = lambda u: slice(u * BLOCK, (u + 1) * BLOCK)

    @pl.when(s == 0)
    def _():
        st_ref[...] = jnp.zeros_like(st_ref)

    def hgrn(u):
        return _hgrn_stages(hf_ref.at[rows_of(u)], rec_of(u), gain_ref, m_ref,
                            o_ref.at[rows_of(u)], st_ref)

    row = lax.broadcasted_iota(jnp.int32, (BLOCK, BLOCK), 0)
    col = lax.broadcasted_iota(jnp.int32, (BLOCK, BLOCK), 1)
    low = col < SB_DH
    pairs = range(SB_PAIRS)
    heads = range(SB_HEADS)

    def fold_stages(u, j_top, tiles, srcs, fresh):
        c = blk_of(u)
        rec_ref, o_u = rec_of(u), o_ref.at[rows_of(u)]
        n = len(tiles)
        nrows = [t[1] for t in tiles]
        assert all(nr == BLOCK for nr in nrows[:-1])
        masks = []
        for i, (kind, nr) in enumerate(tiles):
            assert kind is None or nr == BLOCK
            if kind == "diag":
                m = col < row
            elif kind == "general":
                kpos = (j_top - i) * BLOCK + col
                m = jnp.logical_and(kpos < c * BLOCK + row, kpos >= PAD)
            elif kind == "late_rows":
                m = row >= FAR_ROWS
            else:
                m = None
            masks.append(m)
        z = {}
        for p in pairs:
            kt = jnp.concatenate([srcs[i][_blk(p), :] for i in range(n)],
                                 axis=1)
            zz = _dot(rec_ref[REC_QN + 2 * p * BLOCK:REC_QN + 2 * (p + 1) * BLOCK, :],
                      kt)
            for a in range(2):
                for i in range(n):
                    z[2 * p + a, i] = zz[_blk(a), _blk(i)][:nrows[i]]
        yield
        order = [(hd, i) for hd in heads for i in range(n)]
        packed = []
        for hd, i in order:
            sp = _softplus2(z[hd, i])
            if masks[i] is not None:
                sp = jnp.where(masks[i], sp, 0.0)
            packed.append(jnp.concatenate(_split2(sp), axis=1))
        r = _dot(jnp.concatenate(packed, axis=0), w_ref[...])
        offs = np.cumsum([0] + [nrows[i] for _, i in order])
        yield
        w = {}
        cmax = cmax_rest = None
        for hd in heads:
            crep = None if fresh else crep_ref[u, hd]
            for i in range(n):
                nr = nrows[i]
                o0 = int(offs[hd * n + i])
                rh = r[o0:o0 + nr]
                arg = z[hd, i] - rh[:, :BLOCK]
                wh = jnp.exp2(arg if crep is None else arg + crep[:nr])
                if masks[i] is not None:
                    wh = jnp.where(masks[i], wh, 0.0)
                w[hd, i] = wh.astype(BF16)
                if crep is None:
                    crep = -rh[:, BLOCK:]
                elif nr == BLOCK:
                    crep = crep - rh[:, BLOCK:]
                else:
                    rest = crep[nr:]
                    cmax_rest = (rest if cmax_rest is None
                                 else jnp.maximum(cmax_rest, rest))
                    crep = jnp.concatenate([crep[:nr] - rh[:, BLOCK:], rest], axis=0)
            crep_ref[u, hd] = crep
            cmax = crep if cmax is None else jnp.maximum(cmax, crep)
        yield
        full = [i for i in range(n) if nrows[i] == BLOCK]
        for p in pairs:
            wp = jnp.concatenate(
                [jnp.concatenate([w[2 * p + a, i] for i in full], axis=1)
                 for a in range(2)], axis=0)
            vtile = lambda i: srcs[i][SB_W + p * BLOCK:SB_W + (p + 1) * BLOCK, :]
            pv = _dot(wp, jnp.concatenate([vtile(i) for i in full], axis=0))
            pv = jnp.where(low, pv[:BLOCK], pv[BLOCK:])
            if nrows[-1] < BLOCK:
                nr = nrows[-1]
                ps = _dot(jnp.concatenate([w[2 * p, n - 1], w[2 * p + 1, n - 1]],
                                          axis=0), vtile(n - 1))
                low_nr = lax.broadcasted_iota(jnp.int32, (nr, BLOCK), 1) < SB_DH
                ps = jnp.where(low_nr, ps[:nr], ps[nr:])
                pv = jnp.concatenate([pv[:nr] + ps, pv[nr:]], axis=0)
            if fresh:
                acc_ref[u, p] = pv
                o_u[:, HG_W + p * BLOCK:HG_W + (p + 1) * BLOCK] = pv.astype(BF16)
            else:
                acc_ref[u, p] = acc_ref[u, p] + pv
        return (jnp.max(cmax), jnp.float32(-jnp.inf) if cmax_rest is None
                else jnp.max(cmax_rest))

    n_fast = 3
    fast = s >= 2
    no_more = (jnp.float32(-jnp.inf),) * 2

    def fast_step():
        tiles = (("diag", BLOCK), (None, BLOCK), (None, FAR_ROWS))
        back = lambda u, i: kv_of(u - i) if u >= i else ring.at[u - i + 2]
        def delayed(gen, turns):
            for _ in range(turns):
                yield
            return (yield from gen)

        res = _interleave(
            *[delayed(fold_stages(u, blk_of(u), tiles,
                                  [back(u, i) for i in range(3)], True), 4 * u)
              for u in subs], *[delayed(hgrn(u), 4 * u + 2) for u in subs])
        return sum(res[:BLOCKS_PER_STEP], ())

    def first_steps():
        def start(u):
            return lambda: _interleave(
                fold_stages(u, blk_of(u), (("general", BLOCK),), [kv_of(u)], True),
                hgrn(u))[0]
        res = start(0)()
        for u in subs[1:]:
            res = res + lax.cond(s >= 1, start(u), lambda: no_more)
        return res

    first = lax.cond(fast, fast_step, first_steps)

    def fold(u, j_top, kind):
        pj = lax.rem(j_top + n_real_blk, n_blk)
        cp = pltpu.make_async_copy(rec_hbm.at[pj, REC_KT:REC_HV], kv_scr, sem.at[0])
        cp.start()
        cp.wait()
        return _interleave(fold_stages(u, j_top, ((kind, BLOCK),), [kv_scr],
                                       False))[0][0]

    for u in subs:
        cmax0, cmax_rest = first[2 * u], first[2 * u + 1]
        c = blk_of(u)

        @pl.when(cmax0 > EXP2_ZERO_BELOW)
        def _(u=u, c=c, cmax0=cmax0, cmax_rest=cmax_rest):
            cmax1 = lax.cond(cmax_rest > EXP2_ZERO_BELOW,
                             lambda: fold(u, c - (n_fast - 1), "late_rows"),
                             lambda: cmax0)

            def cond(carry):
                j, cmax = carry
                return jnp.logical_and(j >= 1, cmax > EXP2_ZERO_BELOW)

            def body(carry):
                j, _ = carry
                return j - 1, fold(u, j, None)

            j0 = jnp.where(fast, c - n_fast, c - 1)
            j_end, cmax_end = lax.while_loop(cond, body, (j0, cmax1))

            @pl.when(jnp.logical_and(j_end == 0, cmax_end > EXP2_ZERO_BELOW))
            def _():
                fold(u, 0, "general")

            for p in pairs:
                o_ref[rows_of(u), HG_W + p * BLOCK:HG_W + (p + 1) * BLOCK] = (
                    acc_ref[u, p].astype(BF16))

    @pl.when(s == 0)
    def _():
        ring[1] = kv_of(0)[...]

    @pl.when(s > 0)
    def _():
        for i in range(2):
            ring[i] = kv_of(BLOCKS_PER_STEP - 2 + i)[...]


def _mixer(hf, rec, out_gain, n_real_blk):
    n_blk = rec.shape[0]
    lp = n_blk * BLOCK
    assert n_real_blk % BLOCKS_PER_STEP == 0 and n_blk == n_real_blk + 1
    n_steps = n_real_blk // BLOCKS_PER_STEP + 1
    tile = lambda s: (s + n_steps - 1) % n_steps
    rows = BLOCKS_PER_STEP * BLOCK
    whole = lambda a: pl.BlockSpec(a.shape, lambda s: (0,) * a.ndim)
    msum = jnp.asarray(_hgrn_sum_matrix(), dtype=BF16)
    wsum = jnp.asarray(_sb_sum_matrix(), dtype=BF16)
    return pl.pallas_call(
        functools.partial(_mixer_kernel, n_real_blk=n_real_blk),
        grid=(n_steps,),
        in_specs=[pl.BlockSpec((rows, 4 * HG_W), lambda s: (tile(s), 0)),
                  pl.BlockSpec((BLOCKS_PER_STEP, REC_ROWS, BLOCK),
                               lambda s: (tile(s), 0, 0)),
                  pl.BlockSpec(memory_space=pl.ANY),
                  whole(out_gain), whole(msum), whole(wsum)],
        out_specs=pl.BlockSpec((rows, HG_W + SB_W), lambda s: (tile(s), 0)),
        out_shape=jax.ShapeDtypeStruct((lp, HG_W + SB_W), BF16),
        scratch_shapes=[pltpu.VMEM((HG_HEADS, HG_DV, HG_DK), F32),
                        pltpu.VMEM((BLOCKS_PER_STEP, SB_PAIRS, BLOCK, BLOCK), F32),
                        pltpu.VMEM((BLOCKS_PER_STEP, SB_HEADS, BLOCK, BLOCK), F32),
                        pltpu.VMEM((2, KV_ROWS, BLOCK), BF16),
                        pltpu.VMEM((KV_ROWS, BLOCK), BF16),
                        pltpu.SemaphoreType.DMA((1,))],
        compiler_params=pltpu.CompilerParams(
            dimension_semantics=("arbitrary",)),
        name="mixer",
    )(hf, rec, rec, out_gain, msum, wsum)


def _ffn_out_kernel(h1_ref, o_ref, wo_ref, g2_ref, w2i_ref, w2o_ref, out_ref,
                    act_ref):
    h2 = h1_ref[...] + _dot(o_ref[...], wo_ref[...])
    xn = _rms(h2, g2_ref[...]).astype(BF16)
    out_ref[...] = h2 + 0.5 * _swiglu(xn, w2i_ref, w2o_ref, act_ref)


def _ffn_out(h1, o, wo, g2, w2i, w2o, n_rows, tm):
    const = lambda shape: pl.BlockSpec(shape, lambda i: (0,) * len(shape),
                                       pipeline_mode=pl.Buffered(1))
    rows = lambda w: pl.BlockSpec((tm, w), lambda i: (i, 0))
    return pl.pallas_call(
        _ffn_out_kernel,
        grid=(n_rows // tm,),
        in_specs=[rows(D_MODEL), rows(HG_W + SB_W),
                  const((HG_W + SB_W, D_MODEL)),
                  const((1, D_MODEL)), const((D_MODEL, 2 * D_FF)),
                  const((D_FF, D_MODEL))],
        out_specs=rows(D_MODEL),
        out_shape=jax.ShapeDtypeStruct((n_rows, D_MODEL), F32),
        scratch_shapes=[pltpu.VMEM((tm, D_FF), BF16)],
        compiler_params=pltpu.CompilerParams(
            dimension_semantics=("arbitrary",), vmem_limit_bytes=VMEM_LIMIT),
        name="ffn_out",
    )(h1, o, wo, g2, w2i, w2o)


def kernel(x, meta_tokens, ffn1_norm, ffn1_w_in, ffn1_w_out, mix_norm, w_in,
           hgrn_lb_logits, hgrn_out_norm, sb_q_norm, sb_k_norm, w_out,
           ffn2_norm, ffn2_w_in, ffn2_w_out):
    b, seq, _ = x.shape
    assert b == 1 and seq % BLOCK == 0
    assert ffn1_norm.shape[0] == 1, "single layer"
    n_real_blk = seq // BLOCK
    tm = 512
    assert seq % tm == 0

    meta_tile = jnp.zeros((BLOCK, D_MODEL), x.dtype).at[PAD:].set(
        meta_tokens.astype(x.dtype))

    win = w_in[0].astype(BF16)
    o_hg, o_sq, o_sk, o_sv = 4 * HG_W, 4 * HG_W + SB_W, 4 * HG_W + 2 * SB_W, 4 * HG_W + 3 * SB_W
    whg, wq = win[:, :o_hg], win[:, o_hg:o_sq]
    wkt, wv = win[:, o_sq:o_sk].T, win[:, o_sk:o_sv]
    gk = jnp.tile(sb_k_norm[0], SB_HEADS).reshape(SB_W, 1)
    gq = jnp.tile(sb_q_norm[0], 2).reshape(1, BLOCK)

    h1, hf, rec, wo, w2i, w2o = _ffn_in(
        x[0], meta_tile, ffn1_norm, ffn1_w_in[0].astype(BF16),
        ffn1_w_out[0].astype(BF16), mix_norm, whg, wq, wkt, wv, gk, gq,
        hgrn_lb_logits, (w_out[0], ffn2_w_in[0], ffn2_w_out[0]), tm)
    o = _mixer(hf, rec, hgrn_out_norm, n_real_blk)
    out = _ffn_out(h1, o, wo, ffn2_norm, w2i, w2o, seq, 2 * tm)
    return out[None]
```

```python
import functools

import numpy as np
import jax
import jax.numpy as jnp
from jax import lax
from jax.experimental import pallas as pl
from jax.experimental.pallas import tpu as pltpu

F32 = jnp.float32
BF16 = jnp.bfloat16

D_MODEL = 1024
N_META = 16
BLOCK = 128
PAD = (-N_META) % BLOCK
HG_HEADS = 4
HG_DK = 128
HG_DV = 128
HG_W = HG_HEADS * HG_DK
SB_HEADS = 8
SB_DH = 64
SB_W = SB_HEADS * SB_DH
SB_PAIRS = SB_HEADS // 2
D_FF = 2816
RMS_EPS = 1e-6
FF_CHUNK = 256
N_LEVELS = 7
SUBLANES = 8
LOG2E = 1.4426950408889634
EXP2_ZERO_BELOW = -150.0
FAR_ROWS = 48
REC_QN = 0
REC_KT = REC_QN + SB_HEADS * BLOCK
REC_V = REC_KT + SB_W
REC_HV = REC_V + SB_PAIRS * BLOCK
REC_ROWS = REC_HV + HG_HEADS * BLOCK
KV_ROWS = REC_HV - REC_KT
VMEM_LIMIT = 56 * 1024 * 1024
NT_DIMS = (((1,), (1,)), ((), ()))
TN_DIMS = (((0,), (0,)), ((), ()))


def _dot(a, b):
    return jnp.dot(a, b, preferred_element_type=F32)


def _dot_nt(a, b):
    return lax.dot_general(a, b, NT_DIMS, preferred_element_type=F32)


def _rms(x, gain):
    ms = jnp.mean(x * x, axis=-1, keepdims=True)
    return x * lax.rsqrt(ms + RMS_EPS) * gain


def _split2(x):
    hi = x.astype(BF16)
    lo = (x - hi.astype(F32)).astype(BF16)
    return hi, lo


def _neg_abs(x):
    return -jnp.abs(x)


def _silu(x):
    return x * jax.nn.sigmoid(x)


def _blk(i):
    return slice(i * BLOCK, (i + 1) * BLOCK)


def _swiglu_stages(xn, w_in_ref, w_out_ref, act_ref):
    for c in range(D_FF // FF_CHUNK):
        lo, hi = c * FF_CHUNK, (c + 1) * FF_CHUNK
        g = _dot(xn, w_in_ref[:, lo:hi])
        u = _dot(xn, w_in_ref[:, D_FF + lo:D_FF + hi])
        act_ref[:, lo:hi] = (_silu(g) * u).astype(BF16)
        yield
    return _dot(act_ref[...], w_out_ref[...])


def _interleave(*stage_lists):
    results = [None] * len(stage_lists)
    active = dict(enumerate(stage_lists))
    while active:
        for i, g in list(active.items()):
            try:
                next(g)
            except StopIteration as stop:
                results[i] = stop.value
                del active[i]
    return results


def _swiglu(xn, w_in_ref, w_out_ref, act_ref):
    return _interleave(_swiglu_stages(xn, w_in_ref, w_out_ref, act_ref))[0]


def _ffn_in_tile(h, pads, g1_ref, w1i_ref, w1o_ref, gm_ref, whg_ref, wq_ref,
                 wkt_ref, wv_ref, gk_ref, gq_ref, lbl_ref,
                 h1_ref, hf_ref, rec_ref, act_ref):
    n = h.shape[0]
    rows = slice(0, n)
    xn = _rms(h, g1_ref[...]).astype(BF16)
    h1 = h + 0.5 * _swiglu(xn, w1i_ref, w1o_ref, act_ref.at[rows])
    h1_ref[rows] = h1
    xm = _rms(h1, gm_ref[...]).astype(BF16)

    def put(base, piece, val):
        for t in range(n // BLOCK):
            lo = base + piece * BLOCK
            rec_ref[t, lo:lo + BLOCK, :] = val[_blk(t)]

    q = _dot(xm, wq_ref[...])
    low = lax.broadcasted_iota(jnp.int32, (n, BLOCK), 1) < SB_DH
    qscale = gq_ref[...] * (LOG2E / np.sqrt(np.float32(SB_DH)))
    for p in range(SB_PAIRS):
        qp = q[:, _blk(p)]
        for a in range(2):
            own = low if a == 0 else jnp.logical_not(low)
            ms = jnp.sum(jnp.where(own, qp * qp, 0.0), axis=-1,
                         keepdims=True) * (1.0 / SB_DH)
            qn = jnp.where(own, qp * lax.rsqrt(ms + RMS_EPS) * qscale, 0.0)
            put(REC_QN, 2 * p + a, qn.astype(BF16))
    v = _dot(xm, wv_ref[...]).astype(BF16)
    for p in range(SB_PAIRS):
        put(REC_V, p, v[:, _blk(p)])
    kt = _dot_nt(wkt_ref[...], xm)
    k3 = kt.reshape(SB_HEADS, SB_DH, n)
    ms = jnp.mean(k3 * k3, axis=1, keepdims=True)
    kn = (k3 * lax.rsqrt(ms + RMS_EPS)).reshape(SB_W, n) * gk_ref[...]
    kn = kn.astype(BF16)
    for t in range(n // BLOCK):
        rec_ref[t, REC_KT:REC_KT + SB_W, :] = kn[:, _blk(t)]

    part = lambda i: _dot(xm, whg_ref[:, i * HG_W:(i + 1) * HG_W])
    hv = part(2).astype(BF16)
    for hd in range(HG_HEADS):
        put(REC_HV, hd, hv[:, _blk(hd)])
    hf_ref[rows, 0:HG_W] = _silu(part(0))
    hf_ref[rows, 3 * HG_W:4 * HG_W] = _silu(part(3))
    lg = lbl_ref[...]
    e = jnp.exp(lg - jnp.max(lg, axis=0, keepdims=True))
    lb = e[0:1] / jnp.sum(e, axis=0, keepdims=True)
    z = part(1)
    ez = jnp.exp(_neg_abs(z))
    rz = 1.0 / (1.0 + ez)
    erz = ez * rz
    pos = z >= 0.0
    lf = jnp.log2(lb + (1.0 - lb) * jnp.where(pos, rz, erz))
    k = (1.0 - lb) * jnp.where(pos, erz, rz)
    if pads:
        valid = lax.broadcasted_iota(jnp.int32, z.shape, 0) >= PAD
        lf, k = jnp.where(valid, lf, 0.0), jnp.where(valid, k, 0.0)
    hf_ref[rows, HG_W:2 * HG_W] = lf
    hf_ref[rows, 2 * HG_W:3 * HG_W] = k


N_FFN_IN_PARAMS = 11


def _ffn_in_kernel(x_ref, meta_ref, *refs, cast_chunks):
    nc = len(cast_chunks)
    params = refs[:N_FFN_IN_PARAMS]
    cast_in = refs[N_FFN_IN_PARAMS:N_FFN_IN_PARAMS + nc]
    outs = refs[N_FFN_IN_PARAMS + nc:N_FFN_IN_PARAMS + nc + 3]
    cast_out = refs[N_FFN_IN_PARAMS + nc + 3:N_FFN_IN_PARAMS + 2 * nc + 3]
    act_ref = refs[-1]
    i = pl.program_id(0)
    is_meta = i == pl.num_programs(0) - 1

    for src, dst, chunks in zip(cast_in, cast_out, cast_chunks):
        @pl.when(i < chunks)
        def _():
            dst[...] = src[...].astype(BF16)

    @pl.when(jnp.logical_not(is_meta))
    def _():
        _ffn_in_tile(x_ref[...], False, *params, *outs, act_ref)

    @pl.when(is_meta)
    def _():
        _ffn_in_tile(meta_ref[...], True, *params, *outs, act_ref)


def _cast_chunks(n_rows, n_steps):
    return max(k for k in range(1, n_steps + 1)
               if n_rows % k == 0 and (n_rows // k) % 16 == 0)


def _ffn_in(x2d, meta_tile, g1, w1i, w1o, gm, whg, wq, wkt, wv, gk, gq, lbl,
            to_cast, tm):
    n_real_tiles = x2d.shape[0] // tm
    lp = x2d.shape[0] + BLOCK
    n_blk = lp // BLOCK
    tb = tm // BLOCK
    const = lambda shape: pl.BlockSpec(shape, lambda i: (0,) * len(shape),
                                       pipeline_mode=pl.Buffered(1))
    rows = lambda w: pl.BlockSpec((tm, w), lambda i: (i, 0))
    chunks = tuple(_cast_chunks(w.shape[0], n_real_tiles + 1) for w in to_cast)
    cast_specs = [
        pl.BlockSpec((w.shape[0] // k, w.shape[1]),
                     lambda i, k=k: (jnp.minimum(i, k - 1), 0))
        for w, k in zip(to_cast, chunks)]
    return pl.pallas_call(
        functools.partial(_ffn_in_kernel, cast_chunks=chunks),
        grid=(n_real_tiles + 1,),
        in_specs=[pl.BlockSpec((tm, D_MODEL),
                               lambda i: (jnp.minimum(i, n_real_tiles - 1), 0)),
                  const((BLOCK, D_MODEL)), const((1, D_MODEL)),
                  const((D_MODEL, 2 * D_FF)), const((D_FF, D_MODEL)),
                  const((1, D_MODEL)), const((D_MODEL, 4 * HG_W)),
                  const((D_MODEL, SB_W)), const((SB_W, D_MODEL)),
                  const((D_MODEL, SB_W)), const((SB_W, 1)),
                  const((1, BLOCK)), const((2, HG_W))] + cast_specs,
        out_specs=[rows(D_MODEL), rows(4 * HG_W),
                   pl.BlockSpec((tb, REC_ROWS, BLOCK), lambda i: (i, 0, 0))]
        + cast_specs,
        out_shape=[jax.ShapeDtypeStruct((lp, D_MODEL), F32),
                   jax.ShapeDtypeStruct((lp, 4 * HG_W), F32),
                   jax.ShapeDtypeStruct((n_blk, REC_ROWS, BLOCK), BF16)]
        + [jax.ShapeDtypeStruct(w.shape, BF16) for w in to_cast],
        scratch_shapes=[pltpu.VMEM((tm, D_FF), BF16)],
        compiler_params=pltpu.CompilerParams(
            dimension_semantics=("arbitrary",), vmem_limit_bytes=VMEM_LIMIT),
        name="ffn_in",
    )(x2d, meta_tile, g1, w1i, w1o, gm, whg, wq, wkt, wv, gk, gq, lbl, *to_cast)


def _hgrn_sum_matrix():
    t = np.arange(BLOCK)[:, None]
    j = np.arange(BLOCK)[None, :]
    mats = [(j <= t)]
    for lvl in range(N_LEVELS):
        c = 1 << lvl
        m = (t // (2 * c)) * (2 * c) + c
        upper = (t >= m) & (j >= m) & (j <= t)
        lower = (t < m) & (j > t) & (j <= m - 1)
        mats.append(upper | lower)
    m = np.concatenate(mats, axis=0).astype(np.float32)
    return np.concatenate([m, m], axis=1)


def _hgrn_stages(hf_ref, rec_ref, gain_ref, m_ref, o_ref, st_ref):
    heads = range(HG_HEADS)
    hs = lambda a, h: a[:, h * HG_DK:(h + 1) * HG_DK]

    q = hf_ref[:, 0:HG_W]
    k = hf_ref[:, 2 * HG_W:3 * HG_W]
    v = jnp.concatenate(
        [rec_ref[REC_HV + h * BLOCK:REC_HV + (h + 1) * BLOCK, :] for h in heads],
        axis=1)
    x = _dot(m_ref[...],
             jnp.concatenate(_split2(hf_ref[:, HG_W:2 * HG_W]), axis=0))
    yield
    bcum = x[0:BLOCK]
    b_last = bcum[BLOCK - 1:BLOCK]
    qe = (q * jnp.exp2(bcum)).astype(BF16)
    kd = (k * jnp.exp2(b_last - bcum)).astype(BF16)
    st_decay = jnp.exp2(b_last)

    row = lax.broadcasted_iota(jnp.int32, (BLOCK, BLOCK), 0)
    col = lax.broadcasted_iota(jnp.int32, (BLOCK, BLOCK), 1)
    qb, kb = q.astype(BF16), k.astype(BF16)
    diag = row == col
    attn = [jnp.where(diag, _dot_nt(hs(qb, h), hs(kb, h)), 0.0) for h in heads]
    rowf = lax.broadcasted_iota(jnp.int32, q.shape, 0)
    for lvl in range(N_LEVELS):
        if lvl % 2 == 1:
            yield
        half = 1 << lvl
        el = jnp.exp2(x[(lvl + 1) * BLOCK:(lvl + 2) * BLOCK])
        if half >= SUBLANES:
            qparts, kparts = [], []
            for b in range(BLOCK // half):
                rows = slice(b * half, (b + 1) * half)
                zero = jnp.zeros((half, q.shape[1]), F32)
                if b % 2 == 1:
                    qparts.append(q[rows] * el[rows])
                    kparts.append(zero)
                else:
                    qparts.append(zero)
                    kparts.append(k[rows] * el[rows])
            ql = jnp.concatenate(qparts, axis=0).astype(BF16)
            kl = jnp.concatenate(kparts, axis=0).astype(BF16)
        else:
            is_q = ((rowf >> lvl) & 1) == 1
            ql = jnp.where(is_q, q * el, 0.0).astype(BF16)
            kl = jnp.where(is_q, 0.0, k * el).astype(BF16)
        same = (row >> (lvl + 1)) == (col >> (lvl + 1))
        for h in heads:
            al = _dot_nt(hs(ql, h), hs(kl, h))
            if lvl + 1 < N_LEVELS:
                al = jnp.where(same, al, 0.0)
            attn[h] = attn[h] + al
    yield

    o = []
    for h in heads:
        st = st_ref[h]
        oh = _dot(attn[h].astype(BF16), hs(v, h))
        oh = oh + _dot_nt(hs(qe, h), st.astype(BF16))
        st_ref[h] = st * hs(st_decay, h) + lax.dot_general(
            hs(v, h), hs(kd, h), TN_DIMS, preferred_element_type=F32)
        o.append(oh)
    yield
    o = jnp.concatenate([_rms(o[h], hs(gain_ref[...], h)) for h in heads], axis=1)
    o_ref[:, 0:HG_W] = (o * hf_ref[:, 3 * HG_W:4 * HG_W]).astype(BF16)


def _sb_sum_matrix():
    j = np.arange(BLOCK)[:, None]
    s = np.arange(BLOCK)[None, :]
    w = np.concatenate([(j >= s), np.ones((BLOCK, BLOCK), bool)],
                       axis=1).astype(np.float32)
    return np.concatenate([w, w], axis=0)


def _softplus2(z):
    return jnp.maximum(z, 0.0) + jnp.log2(1.0 + jnp.exp2(_neg_abs(z)))


BLOCKS_PER_STEP = 2


def _mixer_kernel(hf_ref, rec_blk_ref, rec_hbm, gain_ref, m_ref, w_ref, o_ref,
                  st_ref, acc_ref, crep_ref, ring, kv_scr, sem, *, n_real_blk):
    n_blk = rec_hbm.shape[0]
    s = pl.program_id(0)
    subs = range(BLOCKS_PER_STEP)
    blk_of = lambda u: jnp.where(s == 0, 0, BLOCKS_PER_STEP * (s - 1) + 1 + u)
    rec_of = lambda u: rec_blk_ref.at[u]
    kv_of = lambda u: rec_blk_ref.at[u, REC_KT:REC_HV]
    rows_of = lambda u: slice(u * BLOCK, (u + 1) * BLOCK)

    def hgrn(u):
        return _hgrn_stages(hf_ref.at[rows_of(u)], rec_of(u), gain_ref, m_ref,
                            o_ref.at[rows_of(u)], st_ref)

    row = lax.broadcasted_iota(jnp.int32, (BLOCK, BLOCK), 0)
    col = lax.broadcasted_iota(jnp.int32, (BLOCK, BLOCK), 1)
    low = col < SB_DH
    pairs = range(SB_PAIRS)
    heads = range(SB_HEADS)

    def fold_stages(u, j_top, tiles, srcs, fresh):
        c = blk_of(u)
        rec_ref, o_u = rec_of(u), o_ref.at[rows_of(u)]
        n = len(tiles)
        nrows = [t[1] for t in tiles]
        assert all(nr == BLOCK for nr in nrows[:-1])
        masks = []
        for i, (kind, nr) in enumerate(tiles):
            assert kind is None or nr == BLOCK
            if kind == "diag":
                m = col < row
            elif kind == "general":
                kpos = (j_top - i) * BLOCK + col
                m = jnp.logical_and(kpos < c * BLOCK + row, kpos >= PAD)
            elif kind == "late_rows":
                m = row >= FAR_ROWS
            else:
                m = None
            masks.append(m)
        z = {}
        for p in pairs:
            kt = jnp.concatenate([srcs[i][_blk(p), :] for i in range(n)],
                                 axis=1)
            zz = _dot(rec_ref[REC_QN + 2 * p * BLOCK:REC_QN + 2 * (p + 1) * BLOCK, :],
                      kt)
            for a in range(2):
                for i in range(n):
                    z[2 * p + a, i] = zz[_blk(a), _blk(i)][:nrows[i]]
        yield
        order = [(hd, i) for hd in heads for i in range(n)]
        packed = []
        for hd, i in order:
            sp = _softplus2(z[hd, i])
            if masks[i] is not None:
                sp = jnp.where(masks[i], sp, 0.0)
            packed.append(jnp.concatenate(_split2(sp), axis=1))
        r = _dot(jnp.concatenate(packed, axis=0), w_ref[...])
        offs = np.cumsum([0] + [nrows[i] for _, i in order])
        yield
        w = {}
        cmax = cmax_rest = None
        for hd in heads:
            crep = None if fresh else crep_ref[u, hd]
            for i in range(n):
                nr = nrows[i]
                o0 = int(offs[hd * n + i])
                rh = r[o0:o0 + nr]
                arg = z[hd, i] - rh[:, :BLOCK]
                wh = jnp.exp2(arg if crep is None else arg + crep[:nr])
                if masks[i] is not None:
                    wh = jnp.where(masks[i], wh, 0.0)
                w[hd, i] = wh.astype(BF16)
                if crep is None:
                    crep = -rh[:, BLOCK:]
                elif nr == BLOCK:
                    crep = crep - rh[:, BLOCK:]
                else:
                    rest = crep[nr:]
                    cmax_rest = (rest if cmax_rest is None
                                 else jnp.maximum(cmax_rest, rest))
                    crep = jnp.concatenate([crep[:nr] - rh[:, BLOCK:], rest], axis=0)
            crep_ref[u, hd] = crep
            cmax = crep if cmax is None else jnp.maximum(cmax, crep)
        yield
        full = [i for i in range(n) if nrows[i] == BLOCK]
        for p in pairs:
            wp = jnp.concatenate(
                [jnp.concatenate([w[2 * p + a, i] for i in full], axis=1)
                 for a in range(2)], axis=0)
            vtile = lambda i: srcs[i][SB_W + p * BLOCK:SB_W + (p + 1) * BLOCK, :]
            pv = _dot(wp, jnp.concatenate([vtile(i) for i in full], axis=0))
            pv = jnp.where(low, pv[:BLOCK], pv[BLOCK:])
            if nrows[-1] < BLOCK:
                nr = nrows[-1]
                ps = _dot(jnp.concatenate([w[2 * p, n - 1], w[2 * p + 1, n - 1]],
                                          axis=0), vtile(n - 1))
                low_nr = lax.broadcasted_iota(jnp.int32, (nr, BLOCK), 1) < SB_DH
                ps = jnp.where(low_nr, ps[:nr], ps[nr:])
                pv = jnp.concatenate([pv[:nr] + ps, pv[nr:]], axis=0)
            if fresh:
                acc_ref[u, p] = pv
                o_u[:, HG_W + p * BLOCK:HG_W + (p + 1) * BLOCK] = pv.astype(BF16)
            else:
                acc_ref[u, p] = acc_ref[u, p] + pv
        return (jnp.max(cmax), jnp.float32(-jnp.inf) if cmax_rest is None
                else jnp.max(cmax_rest))

    n_fast = 3
    fast = s >= 2
    no_more = (jnp.float32(-jnp.inf),) * 2

    def fast_step():
        tiles = (("diag", BLOCK), (None, BLOCK), (None, FAR_ROWS))
        back = lambda u, i: kv_of(u - i) if u >= i else ring.at[u - i + 2]
        def delayed(gen, turns):
            for _ in range(turns):
                yield
            return (yield from gen)

        res = _interleave(
            *[delayed(fold_stages(u, blk_of(u), tiles,
                                  [back(u, i) for i in range(3)], True), 4 * u)
              for u in subs], *[delayed(hgrn(u), 4 * u + 2) for u in subs])
        for i in range(2):
            ring[i] = kv_of(BLOCKS_PER_STEP - 2 + i)[...]
        return sum(res[:BLOCKS_PER_STEP], ())

    def first_steps():
        @pl.when(s == 0)
        def _():
            st_ref[...] = jnp.zeros_like(st_ref)

        def start(u):
            return lambda: _interleave(
                fold_stages(u, blk_of(u), (("general", BLOCK),), [kv_of(u)], True),
                hgrn(u))[0]
        res = start(0)()
        for u in subs[1:]:
            res = res + lax.cond(s >= 1, start(u), lambda: no_more)

        @pl.when(s == 0)
        def _():
            ring[1] = kv_of(0)[...]

        @pl.when(s > 0)
        def _():
            for i in range(2):
                ring[i] = kv_of(BLOCKS_PER_STEP - 2 + i)[...]
        return res

    first = lax.cond(fast, fast_step, first_steps)

    def fold(u, j_top, kind):
        pj = lax.rem(j_top + n_real_blk, n_blk)
        cp = pltpu.make_async_copy(rec_hbm.at[pj, REC_KT:REC_HV], kv_scr, sem.at[0])
        cp.start()
        cp.wait()
        return _interleave(fold_stages(u, j_top, ((kind, BLOCK),), [kv_scr],
                                       False))[0][0]

    def walk_back(u):
        cmax0, cmax_rest = first[2 * u], first[2 * u + 1]
        c = blk_of(u)

        @pl.when(cmax0 > EXP2_ZERO_BELOW)
        def _():
            cmax1 = lax.cond(cmax_rest > EXP2_ZERO_BELOW,
                             lambda: fold(u, c - (n_fast - 1), "late_rows"),
                             lambda: cmax0)

            def cond(carry):
                j, cmax = carry
                return jnp.logical_and(j >= 1, cmax > EXP2_ZERO_BELOW)

            def body(carry):
                j, _ = carry
                return j - 1, fold(u, j, None)

            j0 = jnp.where(fast, c - n_fast, c - 1)
            j_end, cmax_end = lax.while_loop(cond, body, (j0, cmax1))

            @pl.when(jnp.logical_and(j_end == 0, cmax_end > EXP2_ZERO_BELOW))
            def _():
                fold(u, 0, "general")

            for p in pairs:
                o_ref[rows_of(u), HG_W + p * BLOCK:HG_W + (p + 1) * BLOCK] = (
                    acc_ref[u, p].astype(BF16))

    @pl.when(functools.reduce(jnp.maximum, first[0::2]) > EXP2_ZERO_BELOW)
    def _():
        for u in subs:
            walk_back(u)


def _mixer(hf, rec, out_gain, n_real_blk):
    n_blk = rec.shape[0]
    lp = n_blk * BLOCK
    assert n_real_blk % BLOCKS_PER_STEP == 0 and n_blk == n_real_blk + 1
    n_steps = n_real_blk // BLOCKS_PER_STEP + 1
    tile = lambda s: (s + n_steps - 1) % n_steps
    rows = BLOCKS_PER_STEP * BLOCK
    whole = lambda a: pl.BlockSpec(a.shape, lambda s: (0,) * a.ndim)
    msum = jnp.asarray(_hgrn_sum_matrix(), dtype=BF16)
    wsum = jnp.asarray(_sb_sum_matrix(), dtype=BF16)
    return pl.pallas_call(
        functools.partial(_mixer_kernel, n_real_blk=n_real_blk),
        grid=(n_steps,),
        in_specs=[pl.BlockSpec((rows, 4 * HG_W), lambda s: (tile(s), 0)),
                  pl.BlockSpec((BLOCKS_PER_STEP, REC_ROWS, BLOCK),
                               lambda s: (tile(s), 0, 0)),
                  pl.BlockSpec(memory_space=pl.ANY),
                  whole(out_gain), whole(msum), whole(wsum)],
        out_specs=pl.BlockSpec((rows, HG_W + SB_W), lambda s: (tile(s), 0)),
        out_shape=jax.ShapeDtypeStruct((lp, HG_W + SB_W), BF16),
        scratch_shapes=[pltpu.VMEM((HG_HEADS, HG_DV, HG_DK), F32),
                        pltpu.VMEM((BLOCKS_PER_STEP, SB_PAIRS, BLOCK, BLOCK), F32),
                        pltpu.VMEM((BLOCKS_PER_STEP, SB_HEADS, BLOCK, BLOCK), F32),
                        pltpu.VMEM((2, KV_ROWS, BLOCK), BF16),
                        pltpu.VMEM((KV_ROWS, BLOCK), BF16),
                        pltpu.SemaphoreType.DMA((1,))],
        compiler_params=pltpu.CompilerParams(
            dimension_semantics=("arbitrary",)),
        name="mixer",
    )(hf, rec, rec, out_gain, msum, wsum)


def _ffn_out_kernel(h1_ref, o_ref, wo_ref, g2_ref, w2i_ref, w2o_ref, out_ref,
                    act_ref):
    h2 = h1_ref[...] + _dot(o_ref[...], wo_ref[...])
    xn = _rms(h2, g2_ref[...]).astype(BF16)
    out_ref[...] = h2 + 0.5 * _swiglu(xn, w2i_ref, w2o_ref, act_ref)


def _ffn_out(h1, o, wo, g2, w2i, w2o, n_rows, tm):
    const = lambda shape: pl.BlockSpec(shape, lambda i: (0,) * len(shape),
                                       pipeline_mode=pl.Buffered(1))
    rows = lambda w: pl.BlockSpec((tm, w), lambda i: (i, 0))
    return pl.pallas_call(
        _ffn_out_kernel,
        grid=(n_rows // tm,),
        in_specs=[rows(D_MODEL), rows(HG_W + SB_W),
                  const((HG_W + SB_W, D_MODEL)),
                  const((1, D_MODEL)), const((D_MODEL, 2 * D_FF)),
                  const((D_FF, D_MODEL))],
        out_specs=rows(D_MODEL),
        out_shape=jax.ShapeDtypeStruct((n_rows, D_MODEL), F32),
        scratch_shapes=[pltpu.VMEM((tm, D_FF), BF16)],
        compiler_params=pltpu.CompilerParams(
            dimension_semantics=("arbitrary",), vmem_limit_bytes=VMEM_LIMIT),
        name="ffn_out",
    )(h1, o, wo, g2, w2i, w2o)


def kernel(x, meta_tokens, ffn1_norm, ffn1_w_in, ffn1_w_out, mix_norm, w_in,
           hgrn_lb_logits, hgrn_out_norm, sb_q_norm, sb_k_norm, w_out,
           ffn2_norm, ffn2_w_in, ffn2_w_out):
    b, seq, _ = x.shape
    assert b == 1 and seq % BLOCK == 0
    assert ffn1_norm.shape[0] == 1, "single layer"
    n_real_blk = seq // BLOCK
    tm = 512
    assert seq % tm == 0

    meta_tile = jnp.zeros((BLOCK, D_MODEL), x.dtype).at[PAD:].set(
        meta_tokens.astype(x.dtype))

    win = w_in[0].astype(BF16)
    o_hg, o_sq, o_sk, o_sv = 4 * HG_W, 4 * HG_W + SB_W, 4 * HG_W + 2 * SB_W, 4 * HG_W + 3 * SB_W
    whg, wq = win[:, :o_hg], win[:, o_hg:o_sq]
    wkt, wv = win[:, o_sq:o_sk].T, win[:, o_sk:o_sv]
    gk = jnp.tile(sb_k_norm[0], SB_HEADS).reshape(SB_W, 1)
    gq = jnp.tile(sb_q_norm[0], 2).reshape(1, BLOCK)

    h1, hf, rec, wo, w2i, w2o = _ffn_in(
        x[0], meta_tile, ffn1_norm, ffn1_w_in[0].astype(BF16),
        ffn1_w_out[0].astype(BF16), mix_norm, whg, wq, wkt, wv, gk, gq,
        hgrn_lb_logits, (w_out[0], ffn2_w_in[0], ffn2_w_out[0]), tm)
    o = _mixer(hf, rec, hgrn_out_norm, n_real_blk)
    out = _ffn_out(h1, o, wo, ffn2_norm, w2i, w2o, seq, 2 * tm)
    return out[None]
```

```python
import functools

import numpy as np
import jax
import jax.numpy as jnp
from jax import lax
from jax.experimental import pallas as pl
from jax.experimental.pallas import tpu as pltpu

F32 = jnp.float32
BF16 = jnp.bfloat16

D_MODEL = 1024
N_META = 16
BLOCK = 128
PAD = (-N_META) % BLOCK
HG_HEADS = 4
HG_DK = 128
HG_DV = 128
HG_W = HG_HEADS * HG_DK
SB_HEADS = 8
SB_DH = 64
SB_W = SB_HEADS * SB_DH
SB_PAIRS = SB_HEADS // 2
D_FF = 2816
RMS_EPS = 1e-6
FF_CHUNK = 256
N_LEVELS = 7
SUBLANES = 8
LOG2E = 1.4426950408889634
EXP2_ZERO_BELOW = -150.0
FAR_ROWS = 48
REC_QN = 0
REC_KT = REC_QN + SB_HEADS * BLOCK
REC_V = REC_KT + SB_W
REC_HV = REC_V + SB_PAIRS * BLOCK
REC_ROWS = REC_HV + HG_HEADS * BLOCK
KV_ROWS = REC_HV - REC_KT
VMEM_LIMIT = 56 * 1024 * 1024
NT_DIMS = (((1,), (1,)), ((), ()))
TN_DIMS = (((0,), (0,)), ((), ()))


def _dot(a, b):
    return jnp.dot(a, b, preferred_element_type=F32)


def _dot_nt(a, b):
    return lax.dot_general(a, b, NT_DIMS, preferred_element_type=F32)


def _rms(x, gain):
    ms = jnp.mean(x * x, axis=-1, keepdims=True)
    return x * lax.rsqrt(ms + RMS_EPS) * gain


def _split2(x):
    hi = x.astype(BF16)
    lo = (x - hi.astype(F32)).astype(BF16)
    return hi, lo


def _neg_abs(x):
    return -jnp.abs(x)


def _silu(x):
    return x * jax.nn.sigmoid(x)


def _blk(i):
    return slice(i * BLOCK, (i + 1) * BLOCK)


def _swiglu_stages(xn, w_in_ref, w_out_ref, act_ref):
    for c in range(D_FF // FF_CHUNK):
        lo, hi = c * FF_CHUNK, (c + 1) * FF_CHUNK
        g = _dot(xn, w_in_ref[:, lo:hi])
        u = _dot(xn, w_in_ref[:, D_FF + lo:D_FF + hi])
        act_ref[:, lo:hi] = (_silu(g) * u).astype(BF16)
        yield
    return _dot(act_ref[...], w_out_ref[...])


def _interleave(*stage_lists):
    results = [None] * len(stage_lists)
    active = dict(enumerate(stage_lists))
    while active:
        for i, g in list(active.items()):
            try:
                next(g)
            except StopIteration as stop:
                results[i] = stop.value
                del active[i]
    return results


def _swiglu(xn, w_in_ref, w_out_ref, act_ref):
    return _interleave(_swiglu_stages(xn, w_in_ref, w_out_ref, act_ref))[0]


def _ffn_in_tile(h, pads, g1_ref, w1i_ref, w1o_ref, gm_ref, whg_ref, wq_ref,
                 wk_ref, wv_ref, gk_ref, gq_ref, lbl_ref,
                 h1_ref, hf_ref, rec_ref, act_ref):
    n = h.shape[0]
    rows = slice(0, n)
    xn = _rms(h, g1_ref[...]).astype(BF16)
    h1 = h + 0.5 * _swiglu(xn, w1i_ref, w1o_ref, act_ref.at[rows])
    h1_ref[rows] = h1
    xm = _rms(h1, gm_ref[...]).astype(BF16)

    def put(base, piece, val):
        for t in range(n // BLOCK):
            lo = base + piece * BLOCK
            rec_ref[t, lo:lo + BLOCK, :] = val[_blk(t)]

    q = _dot(xm, wq_ref[...])
    low = lax.broadcasted_iota(jnp.int32, (n, BLOCK), 1) < SB_DH
    qscale = gq_ref[...] * (LOG2E / np.sqrt(np.float32(SB_DH)))
    for p in range(SB_PAIRS):
        qp = q[:, _blk(p)]
        for a in range(2):
            own = low if a == 0 else jnp.logical_not(low)
            ms = jnp.sum(jnp.where(own, qp * qp, 0.0), axis=-1,
                         keepdims=True) * (1.0 / SB_DH)
            qn = jnp.where(own, qp * lax.rsqrt(ms + RMS_EPS) * qscale, 0.0)
            put(REC_QN, 2 * p + a, qn.astype(BF16))
    v = _dot(xm, wv_ref[...]).astype(BF16)
    for p in range(SB_PAIRS):
        put(REC_V, p, v[:, _blk(p)])
    kt = lax.dot_general(wk_ref[...], xm, (((0,), (1,)), ((), ())),
                         preferred_element_type=F32)
    k3 = kt.reshape(SB_HEADS, SB_DH, n)
    ms = jnp.mean(k3 * k3, axis=1, keepdims=True)
    kn = (k3 * lax.rsqrt(ms + RMS_EPS)).reshape(SB_W, n) * gk_ref[...]
    kn = kn.astype(BF16)
    for t in range(n // BLOCK):
        rec_ref[t, REC_KT:REC_KT + SB_W, :] = kn[:, _blk(t)]

    part = lambda i: _dot(xm, whg_ref[:, i * HG_W:(i + 1) * HG_W])
    hv = part(2).astype(BF16)
    for hd in range(HG_HEADS):
        put(REC_HV, hd, hv[:, _blk(hd)])
    hf_ref[rows, 0:HG_W] = _silu(part(0))
    hf_ref[rows, 3 * HG_W:4 * HG_W] = _silu(part(3))
    lg = lbl_ref[...]
    e = jnp.exp(lg - jnp.max(lg, axis=0, keepdims=True))
    lb = e[0:1] / jnp.sum(e, axis=0, keepdims=True)
    z = part(1)
    ez = jnp.exp(_neg_abs(z))
    rz = 1.0 / (1.0 + ez)
    erz = ez * rz
    pos = z >= 0.0
    lf = jnp.log2(lb + (1.0 - lb) * jnp.where(pos, rz, erz))
    k = (1.0 - lb) * jnp.where(pos, erz, rz)
    if pads:
        valid = lax.broadcasted_iota(jnp.int32, z.shape, 0) >= PAD
        lf, k = jnp.where(valid, lf, 0.0), jnp.where(valid, k, 0.0)
    hf_ref[rows, HG_W:2 * HG_W] = lf
    hf_ref[rows, 2 * HG_W:3 * HG_W] = k


N_FFN_IN_PARAMS = 11


def _ffn_in_kernel(x_ref, meta_ref, *refs, cast_chunks):
    nc = len(cast_chunks)
    params = refs[:N_FFN_IN_PARAMS]
    cast_in = refs[N_FFN_IN_PARAMS:N_FFN_IN_PARAMS + nc]
    outs = refs[N_FFN_IN_PARAMS + nc:N_FFN_IN_PARAMS + nc + 3]
    cast_out = refs[N_FFN_IN_PARAMS + nc + 3:N_FFN_IN_PARAMS + 2 * nc + 3]
    act_ref = refs[-1]
    i = pl.program_id(0)
    is_meta = i == pl.num_programs(0) - 1

    for src, dst, chunks in zip(cast_in, cast_out, cast_chunks):
        @pl.when(i < chunks)
        def _():
            dst[...] = src[...].astype(BF16)

    @pl.when(jnp.logical_not(is_meta))
    def _():
        _ffn_in_tile(x_ref[...], False, *params, *outs, act_ref)

    @pl.when(is_meta)
    def _():
        _ffn_in_tile(meta_ref[...], True, *params, *outs, act_ref)


def _cast_chunks(n_rows, n_steps):
    return max(k for k in range(1, n_steps + 1)
               if n_rows % k == 0 and (n_rows // k) % 16 == 0)


def _ffn_in(x2d, meta_tile, g1, w1i, w1o, gm, win, gk, gq, lbl, to_cast, tm):
    n_real_tiles = x2d.shape[0] // tm
    lp = x2d.shape[0] + BLOCK
    n_blk = lp // BLOCK
    tb = tm // BLOCK
    const = lambda shape: pl.BlockSpec(shape, lambda i: (0,) * len(shape),
                                       pipeline_mode=pl.Buffered(1))
    rows = lambda w: pl.BlockSpec((tm, w), lambda i: (i, 0))
    wcols = lambda width, start: pl.BlockSpec(
        (D_MODEL, width), lambda i: (0, start // width),
        pipeline_mode=pl.Buffered(1))
    chunks = tuple(_cast_chunks(w.shape[0], n_real_tiles + 1) for w in to_cast)
    cast_specs = [
        pl.BlockSpec((w.shape[0] // k, w.shape[1]),
                     lambda i, k=k: (jnp.minimum(i, k - 1), 0))
        for w, k in zip(to_cast, chunks)]
    return pl.pallas_call(
        functools.partial(_ffn_in_kernel, cast_chunks=chunks),
        grid=(n_real_tiles + 1,),
        in_specs=[pl.BlockSpec((tm, D_MODEL),
                               lambda i: (jnp.minimum(i, n_real_tiles - 1), 0)),
                  const((BLOCK, D_MODEL)), const((1, D_MODEL)),
                  const((D_MODEL, 2 * D_FF)), const((D_FF, D_MODEL)),
                  const((1, D_MODEL)), wcols(4 * HG_W, 0),
                  wcols(SB_W, 4 * HG_W), wcols(SB_W, 4 * HG_W + SB_W),
                  wcols(SB_W, 4 * HG_W + 2 * SB_W), const((SB_W, 1)),
                  const((1, BLOCK)), const((2, HG_W))] + cast_specs,
        out_specs=[rows(D_MODEL), rows(4 * HG_W),
                   pl.BlockSpec((tb, REC_ROWS, BLOCK), lambda i: (i, 0, 0))]
        + cast_specs,
        out_shape=[jax.ShapeDtypeStruct((lp, D_MODEL), F32),
                   jax.ShapeDtypeStruct((lp, 4 * HG_W), F32),
                   jax.ShapeDtypeStruct((n_blk, REC_ROWS, BLOCK), BF16)]
        + [jax.ShapeDtypeStruct(w.shape, BF16) for w in to_cast],
        scratch_shapes=[pltpu.VMEM((tm, D_FF), BF16)],
        compiler_params=pltpu.CompilerParams(
            dimension_semantics=("arbitrary",), vmem_limit_bytes=VMEM_LIMIT),
        name="ffn_in",
    )(x2d, meta_tile, g1, w1i, w1o, gm, win, win, win, win, gk, gq, lbl, *to_cast)


def _hgrn_sum_matrix():
    t = np.arange(BLOCK)[:, None]
    j = np.arange(BLOCK)[None, :]
    mats = [(j <= t)]
    for lvl in range(N_LEVELS):
        c = 1 << lvl
        m = (t // (2 * c)) * (2 * c) + c
        upper = (t >= m) & (j >= m) & (j <= t)
        lower = (t < m) & (j > t) & (j <= m - 1)
        mats.append(upper | lower)
    m = np.concatenate(mats, axis=0).astype(np.float32)
    return np.concatenate([m, m], axis=1)


def _hgrn_stages(hf_ref, rec_ref, gain_ref, m_ref, o_ref, st_ref):
    heads = range(HG_HEADS)
    hs = lambda a, h: a[:, h * HG_DK:(h + 1) * HG_DK]

    q = hf_ref[:, 0:HG_W]
    k = hf_ref[:, 2 * HG_W:3 * HG_W]
    v = jnp.concatenate(
        [rec_ref[REC_HV + h * BLOCK:REC_HV + (h + 1) * BLOCK, :] for h in heads],
        axis=1)
    x = _dot(m_ref[...],
             jnp.concatenate(_split2(hf_ref[:, HG_W:2 * HG_W]), axis=0))
    yield
    bcum = x[0:BLOCK]
    b_last = bcum[BLOCK - 1:BLOCK]
    qe = (q * jnp.exp2(bcum)).astype(BF16)
    kd = (k * jnp.exp2(b_last - bcum)).astype(BF16)
    st_decay = jnp.exp2(b_last)

    row = lax.broadcasted_iota(jnp.int32, (BLOCK, BLOCK), 0)
    col = lax.broadcasted_iota(jnp.int32, (BLOCK, BLOCK), 1)
    qb, kb = q.astype(BF16), k.astype(BF16)
    diag = row == col
    attn = [jnp.where(diag, _dot_nt(hs(qb, h), hs(kb, h)), 0.0) for h in heads]
    rowf = lax.broadcasted_iota(jnp.int32, q.shape, 0)
    for lvl in range(N_LEVELS):
        if lvl % 2 == 1:
            yield
        half = 1 << lvl
        el = jnp.exp2(x[(lvl + 1) * BLOCK:(lvl + 2) * BLOCK])
        if half >= SUBLANES:
            qparts, kparts = [], []
            for b in range(BLOCK // half):
                rows = slice(b * half, (b + 1) * half)
                zero = jnp.zeros((half, q.shape[1]), F32)
                if b % 2 == 1:
                    qparts.append(q[rows] * el[rows])
                    kparts.append(zero)
                else:
                    qparts.append(zero)
                    kparts.append(k[rows] * el[rows])
            ql = jnp.concatenate(qparts, axis=0).astype(BF16)
            kl = jnp.concatenate(kparts, axis=0).astype(BF16)
        else:
            is_q = ((rowf >> lvl) & 1) == 1
            ql = jnp.where(is_q, q * el, 0.0).astype(BF16)
            kl = jnp.where(is_q, 0.0, k * el).astype(BF16)
        same = (row >> (lvl + 1)) == (col >> (lvl + 1))
        for h in heads:
            al = _dot_nt(hs(ql, h), hs(kl, h))
            if lvl + 1 < N_LEVELS:
                al = jnp.where(same, al, 0.0)
            attn[h] = attn[h] + al
    yield

    o = []
    for h in heads:
        st = st_ref[h]
        oh = _dot(attn[h].astype(BF16), hs(v, h))
        oh = oh + _dot_nt(hs(qe, h), st.astype(BF16))
        st_ref[h] = st * hs(st_decay, h) + lax.dot_general(
            hs(v, h), hs(kd, h), TN_DIMS, preferred_element_type=F32)
        o.append(oh)
    yield
    o = jnp.concatenate([_rms(o[h], hs(gain_ref[...], h)) for h in heads], axis=1)
    o_ref[:, 0:HG_W] = (o * hf_ref[:, 3 * HG_W:4 * HG_W]).astype(BF16)


def _sb_sum_matrix():
    j = np.arange(BLOCK)[:, None]
    s = np.arange(BLOCK)[None, :]
    w = np.concatenate([(j >= s), np.ones((BLOCK, BLOCK), bool)],
                       axis=1).astype(np.float32)
    return np.concatenate([w, w], axis=0)


def _softplus2(z):
    return jnp.maximum(z, 0.0) + jnp.log2(1.0 + jnp.exp2(_neg_abs(z)))


BLOCKS_PER_STEP = 2


def _mixer_kernel(hf_ref, rec_blk_ref, rec_hbm, gain_ref, m_ref, w_ref, o_ref,
                  st_ref, acc_ref, crep_ref, ring, kv_scr, sem, *, n_real_blk):
    n_blk = rec_hbm.shape[0]
    s = pl.program_id(0)
    subs = range(BLOCKS_PER_STEP)
    blk_of = lambda u: jnp.where(s == 0, 0, BLOCKS_PER_STEP * (s - 1) + 1 + u)
    rec_of = lambda u: rec_blk_ref.at[u]
    kv_of = lambda u: rec_blk_ref.at[u, REC_KT:REC_HV]
    rows_of = lambda u: slice(u * BLOCK, (u + 1) * BLOCK)

    def hgrn(u):
        return _hgrn_stages(hf_ref.at[rows_of(u)], rec_of(u), gain_ref, m_ref,
                            o_ref.at[rows_of(u)], st_ref)

    row = lax.broadcasted_iota(jnp.int32, (BLOCK, BLOCK), 0)
    col = lax.broadcasted_iota(jnp.int32, (BLOCK, BLOCK), 1)
    low = col < SB_DH
    pairs = range(SB_PAIRS)
    heads = range(SB_HEADS)

    def fold_stages(u, j_top, tiles, srcs, fresh):
        c = blk_of(u)
        rec_ref, o_u = rec_of(u), o_ref.at[rows_of(u)]
        n = len(tiles)
        nrows = [t[1] for t in tiles]
        assert all(nr == BLOCK for nr in nrows[:-1])
        masks = []
        for i, (kind, nr) in enumerate(tiles):
            assert kind is None or nr == BLOCK
            if kind == "diag":
                m = col < row
            elif kind == "general":
                kpos = (j_top - i) * BLOCK + col
                m = jnp.logical_and(kpos < c * BLOCK + row, kpos >= PAD)
            elif kind == "late_rows":
                m = row >= FAR_ROWS
            else:
                m = None
            masks.append(m)
        z = {}
        for p in pairs:
            kt = jnp.concatenate([srcs[i][_blk(p), :] for i in range(n)],
                                 axis=1)
            zz = _dot(rec_ref[REC_QN + 2 * p * BLOCK:REC_QN + 2 * (p + 1) * BLOCK, :],
                      kt)
            for a in range(2):
                for i in range(n):
                    z[2 * p + a, i] = zz[_blk(a), _blk(i)][:nrows[i]]
        yield
        order = [(hd, i) for hd in heads for i in range(n)]
        packed = []
        for hd, i in order:
            sp = _softplus2(z[hd, i])
            if masks[i] is not None:
                sp = jnp.where(masks[i], sp, 0.0)
            packed.append(jnp.concatenate(_split2(sp), axis=1))
        r = _dot(jnp.concatenate(packed, axis=0), w_ref[...])
        offs = np.cumsum([0] + [nrows[i] for _, i in order])
        yield
        w = {}
        cmax = cmax_rest = None
        for hd in heads:
            crep = None if fresh else crep_ref[u, hd]
            for i in range(n):
                nr = nrows[i]
                o0 = int(offs[hd * n + i])
                rh = r[o0:o0 + nr]
                arg = z[hd, i] - rh[:, :BLOCK]
                wh = jnp.exp2(arg if crep is None else arg + crep[:nr])
                if masks[i] is not None:
                    wh = jnp.where(masks[i], wh, 0.0)
                w[hd, i] = wh.astype(BF16)
                if crep is None:
                    crep = -rh[:, BLOCK:]
                elif nr == BLOCK:
                    crep = crep - rh[:, BLOCK:]
                else:
                    rest = crep[nr:]
                    cmax_rest = (rest if cmax_rest is None
                                 else jnp.maximum(cmax_rest, rest))
                    crep = jnp.concatenate([crep[:nr] - rh[:, BLOCK:], rest], axis=0)
            crep_ref[u, hd] = crep
            cmax = crep if cmax is None else jnp.maximum(cmax, crep)
        yield
        full = [i for i in range(n) if nrows[i] == BLOCK]
        for p in pairs:
            wp = jnp.concatenate(
                [jnp.concatenate([w[2 * p + a, i] for i in full], axis=1)
                 for a in range(2)], axis=0)
            vtile = lambda i: srcs[i][SB_W + p * BLOCK:SB_W + (p + 1) * BLOCK, :]
            pv = _dot(wp, jnp.concatenate([vtile(i) for i in full], axis=0))
            pv = jnp.where(low, pv[:BLOCK], pv[BLOCK:])
            if nrows[-1] < BLOCK:
                nr = nrows[-1]
                ps = _dot(jnp.concatenate([w[2 * p, n - 1], w[2 * p + 1, n - 1]],
                                          axis=0), vtile(n - 1))
                low_nr = lax.broadcasted_iota(jnp.int32, (nr, BLOCK), 1) < SB_DH
                ps = jnp.where(low_nr, ps[:nr], ps[nr:])
                pv = jnp.concatenate([pv[:nr] + ps, pv[nr:]], axis=0)
            if fresh:
                acc_ref[u, p] = pv
                o_u[:, HG_W + p * BLOCK:HG_W + (p + 1) * BLOCK] = pv.astype(BF16)
            else:
                acc_ref[u, p] = acc_ref[u, p] + pv
        return (jnp.max(cmax), jnp.float32(-jnp.inf) if cmax_rest is None
                else jnp.max(cmax_rest))

    n_fast = 3
    fast = s >= 2
    no_more = (jnp.float32(-jnp.inf),) * 2

    def fast_step():
        tiles = (("diag", BLOCK), (None, BLOCK), (None, FAR_ROWS))
        back = lambda u, i: kv_of(u - i) if u >= i else ring.at[u - i + 2]
        def delayed(gen, turns):
            for _ in range(turns):
                yield
            return (yield from gen)

        res = _interleave(
            *[delayed(fold_stages(u, blk_of(u), tiles,
                                  [back(u, i) for i in range(3)], True), 4 * u)
              for u in subs], *[delayed(hgrn(u), 4 * u + 2) for u in subs])
        for i in range(2):
            ring[i] = kv_of(BLOCKS_PER_STEP - 2 + i)[...]
        return sum(res[:BLOCKS_PER_STEP], ())

    def first_steps():
        @pl.when(s == 0)
        def _():
            st_ref[...] = jnp.zeros_like(st_ref)

        def start(u):
            return lambda: _interleave(
                fold_stages(u, blk_of(u), (("general", BLOCK),), [kv_of(u)], True),
                hgrn(u))[0]
        res = start(0)()
        for u in subs[1:]:
            res = res + lax.cond(s >= 1, start(u), lambda: no_more)

        @pl.when(s == 0)
        def _():
            ring[1] = kv_of(0)[...]

        @pl.when(s > 0)
        def _():
            for i in range(2):
                ring[i] = kv_of(BLOCKS_PER_STEP - 2 + i)[...]
        return res

    first = lax.cond(fast, fast_step, first_steps)

    def fold(u, j_top, kind):
        pj = lax.rem(j_top + n_real_blk, n_blk)
        cp = pltpu.make_async_copy(rec_hbm.at[pj, REC_KT:REC_HV], kv_scr, sem.at[0])
        cp.start()
        cp.wait()
        return _interleave(fold_stages(u, j_top, ((kind, BLOCK),), [kv_scr],
                                       False))[0][0]

    def walk_back(u):
        cmax0, cmax_rest = first[2 * u], first[2 * u + 1]
        c = blk_of(u)

        @pl.when(cmax0 > EXP2_ZERO_BELOW)
        def _():
            cmax1 = lax.cond(cmax_rest > EXP2_ZERO_BELOW,
                             lambda: fold(u, c - (n_fast - 1), "late_rows"),
                             lambda: cmax0)

            def cond(carry):
                j, cmax = carry
                return jnp.logical_and(j >= 1, cmax > EXP2_ZERO_BELOW)

            def body(carry):
                j, _ = carry
                return j - 1, fold(u, j, None)

            j0 = jnp.where(fast, c - n_fast, c - 1)
            j_end, cmax_end = lax.while_loop(cond, body, (j0, cmax1))

            @pl.when(jnp.logical_and(j_end == 0, cmax_end > EXP2_ZERO_BELOW))
            def _():
                fold(u, 0, "general")

            for p in pairs:
                o_ref[rows_of(u), HG_W + p * BLOCK:HG_W + (p + 1) * BLOCK] = (
                    acc_ref[u, p].astype(BF16))

    @pl.when(functools.reduce(jnp.maximum, first[0::2]) > EXP2_ZERO_BELOW)
    def _():
        for u in subs:
            walk_back(u)


def _mixer(hf, rec, out_gain, n_real_blk):
    n_blk = rec.shape[0]
    lp = n_blk * BLOCK
    assert n_real_blk % BLOCKS_PER_STEP == 0 and n_blk == n_real_blk + 1
    n_steps = n_real_blk // BLOCKS_PER_STEP + 1
    tile = lambda s: (s + n_steps - 1) % n_steps
    rows = BLOCKS_PER_STEP * BLOCK
    whole = lambda a: pl.BlockSpec(a.shape, lambda s: (0,) * a.ndim)
    msum = jnp.asarray(_hgrn_sum_matrix(), dtype=BF16)
    wsum = jnp.asarray(_sb_sum_matrix(), dtype=BF16)
    return pl.pallas_call(
        functools.partial(_mixer_kernel, n_real_blk=n_real_blk),
        grid=(n_steps,),
        in_specs=[pl.BlockSpec((rows, 4 * HG_W), lambda s: (tile(s), 0)),
                  pl.BlockSpec((BLOCKS_PER_STEP, REC_ROWS, BLOCK),
                               lambda s: (tile(s), 0, 0)),
                  pl.BlockSpec(memory_space=pl.ANY),
                  whole(out_gain), whole(msum), whole(wsum)],
        out_specs=pl.BlockSpec((rows, HG_W + SB_W), lambda s: (tile(s), 0)),
        out_shape=jax.ShapeDtypeStruct((lp, HG_W + SB_W), BF16),
        scratch_shapes=[pltpu.VMEM((HG_HEADS, HG_DV, HG_DK), F32),
                        pltpu.VMEM((BLOCKS_PER_STEP, SB_PAIRS, BLOCK, BLOCK), F32),
                        pltpu.VMEM((BLOCKS_PER_STEP, SB_HEADS, BLOCK, BLOCK), F32),
                        pltpu.VMEM((2, KV_ROWS, BLOCK), BF16),
                        pltpu.VMEM((KV_ROWS, BLOCK), BF16),
                        pltpu.SemaphoreType.DMA((1,))],
        compiler_params=pltpu.CompilerParams(
            dimension_semantics=("arbitrary",)),
        name="mixer",
    )(hf, rec, rec, out_gain, msum, wsum)


def _ffn_out_kernel(h1_ref, o_ref, wo_ref, g2_ref, w2i_ref, w2o_ref, out_ref,
                    act_ref):
    h2 = h1_ref[...] + _dot(o_ref[...], wo_ref[...])
    xn = _rms(h2, g2_ref[...]).astype(BF16)
    out_ref[...] = h2 + 0.5 * _swiglu(xn, w2i_ref, w2o_ref, act_ref)


def _ffn_out(h1, o, wo, g2, w2i, w2o, n_rows, tm):
    const = lambda shape: pl.BlockSpec(shape, lambda i: (0,) * len(shape),
                                       pipeline_mode=pl.Buffered(1))
    rows = lambda w: pl.BlockSpec((tm, w), lambda i: (i, 0))
    return pl.pallas_call(
        _ffn_out_kernel,
        grid=(n_rows // tm,),
        in_specs=[rows(D_MODEL), rows(HG_W + SB_W),
                  const((HG_W + SB_W, D_MODEL)),
                  const((1, D_MODEL)), const((D_MODEL, 2 * D_FF)),
                  const((D_FF, D_MODEL))],
        out_specs=rows(D_MODEL),
        out_shape=jax.ShapeDtypeStruct((n_rows, D_MODEL), F32),
        scratch_shapes=[pltpu.VMEM((tm, D_FF), BF16)],
        compiler_params=pltpu.CompilerParams(
            dimension_semantics=("arbitrary",), vmem_limit_bytes=VMEM_LIMIT),
        name="ffn_out",
    )(h1, o, wo, g2, w2i, w2o)


def kernel(x, meta_tokens, ffn1_norm, ffn1_w_in, ffn1_w_out, mix_norm, w_in,
           hgrn_lb_logits, hgrn_out_norm, sb_q_norm, sb_k_norm, w_out,
           ffn2_norm, ffn2_w_in, ffn2_w_out):
    b, seq, _ = x.shape
    assert b == 1 and seq % BLOCK == 0
    assert ffn1_norm.shape[0] == 1, "single layer"
    n_real_blk = seq // BLOCK
    tm = 512
    assert seq % tm == 0

    meta_tile = jnp.zeros((BLOCK, D_MODEL), x.dtype).at[PAD:].set(
        meta_tokens.astype(x.dtype))

    gk = jnp.tile(sb_k_norm[0], SB_HEADS).reshape(SB_W, 1)
    gq = jnp.tile(sb_q_norm[0], 2).reshape(1, BLOCK)

    h1, hf, rec, wo, w2i, w2o = _ffn_in(
        x[0], meta_tile, ffn1_norm, ffn1_w_in[0].astype(BF16),
        ffn1_w_out[0].astype(BF16), mix_norm, w_in[0].astype(BF16), gk, gq,
        hgrn_lb_logits, (w_out[0], ffn2_w_in[0], ffn2_w_out[0]), tm)
    o = _mixer(hf, rec, hgrn_out_norm, n_real_blk)
    out = _ffn_out(h1, o, wo, ffn2_norm, w2i, w2o, seq, 2 * tm)
    return out[None]
```

```python
import functools

import numpy as np
import jax
import jax.numpy as jnp
from jax import lax
from jax.experimental import pallas as pl
from jax.experimental.pallas import tpu as pltpu

F32 = jnp.float32
BF16 = jnp.bfloat16

D_MODEL = 1024
N_META = 16
BLOCK = 128
PAD = (-N_META) % BLOCK
HG_HEADS = 4
HG_DK = 128
HG_DV = 128
HG_W = HG_HEADS * HG_DK
SB_HEADS = 8
SB_DH = 64
SB_W = SB_HEADS * SB_DH
SB_PAIRS = SB_HEADS // 2
D_FF = 2816
RMS_EPS = 1e-6
FF_CHUNK = 256
N_LEVELS = 7
N_FINE_LEVELS = 3
LOG2E = 1.4426950408889634
EXP2_ZERO_BELOW = -150.0
FAR_ROWS = 48
REC_QN = 0
REC_KT = REC_QN + SB_HEADS * BLOCK
REC_V = REC_KT + SB_W
REC_HV = REC_V + SB_PAIRS * BLOCK
REC_ROWS = REC_HV + HG_HEADS * BLOCK
KV_ROWS = REC_HV - REC_KT
VMEM_LIMIT = 56 * 1024 * 1024
NT_DIMS = (((1,), (1,)), ((), ()))
TN_DIMS = (((0,), (0,)), ((), ()))


def _dot(a, b):
    return jnp.dot(a, b, preferred_element_type=F32)


def _dot_nt(a, b):
    return lax.dot_general(a, b, NT_DIMS, preferred_element_type=F32)


def _rms(x, gain):
    ms = jnp.mean(x * x, axis=-1, keepdims=True)
    return x * lax.rsqrt(ms + RMS_EPS) * gain


def _split2(x):
    hi = x.astype(BF16)
    lo = (x - hi.astype(F32)).astype(BF16)
    return hi, lo


def _neg_abs(x):
    return -jnp.abs(x)


def _silu(x):
    return x * jax.nn.sigmoid(x)


def _blk(i):
    return slice(i * BLOCK, (i + 1) * BLOCK)


def _swiglu_stages(xn, w_in_ref, w_out_ref, act_ref):
    for c in range(D_FF // FF_CHUNK):
        lo, hi = c * FF_CHUNK, (c + 1) * FF_CHUNK
        g = _dot(xn, w_in_ref[:, lo:hi])
        u = _dot(xn, w_in_ref[:, D_FF + lo:D_FF + hi])
        act_ref[:, lo:hi] = (_silu(g) * u).astype(BF16)
        yield
    return _dot(act_ref[...], w_out_ref[...])


def _interleave(*stage_lists):
    results = [None] * len(stage_lists)
    active = dict(enumerate(stage_lists))
    while active:
        for i, g in list(active.items()):
            try:
                next(g)
            except StopIteration as stop:
                results[i] = stop.value
                del active[i]
    return results


def _swiglu(xn, w_in_ref, w_out_ref, act_ref):
    return _interleave(_swiglu_stages(xn, w_in_ref, w_out_ref, act_ref))[0]


def _ffn_in_tile(h, pads, g1_ref, w1i_ref, w1o_ref, gm_ref, whg_ref, wq_ref,
                 wk_ref, wv_ref, gk_ref, gq_ref, lbl_ref,
                 h1_ref, hf_ref, rec_ref, act_ref):
    n = h.shape[0]
    rows = slice(0, n)
    xn = _rms(h, g1_ref[...]).astype(BF16)
    h1 = h + 0.5 * _swiglu(xn, w1i_ref, w1o_ref, act_ref.at[rows])
    h1_ref[rows] = h1
    xm = _rms(h1, gm_ref[...]).astype(BF16)

    def put(base, piece, val):
        for t in range(n // BLOCK):
            lo = base + piece * BLOCK
            rec_ref[t, lo:lo + BLOCK, :] = val[_blk(t)]

    q = _dot(xm, wq_ref[...])
    low = lax.broadcasted_iota(jnp.int32, (n, BLOCK), 1) < SB_DH
    qscale = gq_ref[...] * (LOG2E / np.sqrt(np.float32(SB_DH)))
    for p in range(SB_PAIRS):
        qp = q[:, _blk(p)]
        for a in range(2):
            own = low if a == 0 else jnp.logical_not(low)
            ms = jnp.sum(jnp.where(own, qp * qp, 0.0), axis=-1,
                         keepdims=True) * (1.0 / SB_DH)
            qn = jnp.where(own, qp * lax.rsqrt(ms + RMS_EPS) * qscale, 0.0)
            put(REC_QN, 2 * p + a, qn.astype(BF16))
    v = _dot(xm, wv_ref[...]).astype(BF16)
    for p in range(SB_PAIRS):
        put(REC_V, p, v[:, _blk(p)])
    kt = lax.dot_general(wk_ref[...], xm, (((0,), (1,)), ((), ())),
                         preferred_element_type=F32)
    k3 = kt.reshape(SB_HEADS, SB_DH, n)
    ms = jnp.mean(k3 * k3, axis=1, keepdims=True)
    kn = (k3 * lax.rsqrt(ms + RMS_EPS)).reshape(SB_W, n) * gk_ref[...]
    kn = kn.astype(BF16)
    for t in range(n // BLOCK):
        rec_ref[t, REC_KT:REC_KT + SB_W, :] = kn[:, _blk(t)]

    part = lambda i: _dot(xm, whg_ref[:, i * HG_W:(i + 1) * HG_W])
    hv = part(2).astype(BF16)
    for hd in range(HG_HEADS):
        put(REC_HV, hd, hv[:, _blk(hd)])
    hf_ref[rows, 0:HG_W] = _silu(part(0))
    hf_ref[rows, 3 * HG_W:4 * HG_W] = _silu(part(3))
    lg = lbl_ref[...]
    e = jnp.exp(lg - jnp.max(lg, axis=0, keepdims=True))
    lb = e[0:1] / jnp.sum(e, axis=0, keepdims=True)
    z = part(1)
    ez = jnp.exp(_neg_abs(z))
    rz = 1.0 / (1.0 + ez)
    erz = ez * rz
    pos = z >= 0.0
    lf = jnp.log2(lb + (1.0 - lb) * jnp.where(pos, rz, erz))
    k = (1.0 - lb) * jnp.where(pos, erz, rz)
    if pads:
        valid = lax.broadcasted_iota(jnp.int32, z.shape, 0) >= PAD
        lf, k = jnp.where(valid, lf, 0.0), jnp.where(valid, k, 0.0)
    hf_ref[rows, HG_W:2 * HG_W] = lf
    hf_ref[rows, 2 * HG_W:3 * HG_W] = k


N_FFN_IN_PARAMS = 11


def _ffn_in_kernel(x_ref, meta_ref, *refs, cast_chunks):
    nc = len(cast_chunks)
    params = refs[:N_FFN_IN_PARAMS]
    cast_in = refs[N_FFN_IN_PARAMS:N_FFN_IN_PARAMS + nc]
    outs = refs[N_FFN_IN_PARAMS + nc:N_FFN_IN_PARAMS + nc + 3]
    cast_out = refs[N_FFN_IN_PARAMS + nc + 3:N_FFN_IN_PARAMS + 2 * nc + 3]
    act_ref = refs[-1]
    i = pl.program_id(0)
    is_meta = i == pl.num_programs(0) - 1

    for src, dst, chunks in zip(cast_in, cast_out, cast_chunks):
        @pl.when(i < chunks)
        def _():
            dst[...] = src[...].astype(BF16)

    @pl.when(jnp.logical_not(is_meta))
    def _():
        _ffn_in_tile(x_ref[...], False, *params, *outs, act_ref)

    @pl.when(is_meta)
    def _():
        _ffn_in_tile(meta_ref[...], True, *params, *outs, act_ref)


def _cast_chunks(n_rows, n_steps):
    return max(k for k in range(1, n_steps + 1)
               if n_rows % k == 0 and (n_rows // k) % 16 == 0)


def _ffn_in(x2d, meta_tile, g1, w1i, w1o, gm, win, gk, gq, lbl, to_cast, tm):
    n_real_tiles = x2d.shape[0] // tm
    lp = x2d.shape[0] + BLOCK
    n_blk = lp // BLOCK
    tb = tm // BLOCK
    const = lambda shape: pl.BlockSpec(shape, lambda i: (0,) * len(shape),
                                       pipeline_mode=pl.Buffered(1))
    rows = lambda w: pl.BlockSpec((tm, w), lambda i: (i, 0))
    wcols = lambda width, start: pl.BlockSpec(
        (D_MODEL, width), lambda i: (0, start // width),
        pipeline_mode=pl.Buffered(1))
    chunks = tuple(_cast_chunks(w.shape[0], n_real_tiles + 1) for w in to_cast)
    cast_specs = [
        pl.BlockSpec((w.shape[0] // k, w.shape[1]),
                     lambda i, k=k: (jnp.minimum(i, k - 1), 0))
        for w, k in zip(to_cast, chunks)]
    return pl.pallas_call(
        functools.partial(_ffn_in_kernel, cast_chunks=chunks),
        grid=(n_real_tiles + 1,),
        in_specs=[pl.BlockSpec((tm, D_MODEL),
                               lambda i: (jnp.minimum(i, n_real_tiles - 1), 0)),
                  const((BLOCK, D_MODEL)), const((1, D_MODEL)),
                  const((D_MODEL, 2 * D_FF)), const((D_FF, D_MODEL)),
                  const((1, D_MODEL)), wcols(4 * HG_W, 0),
                  wcols(SB_W, 4 * HG_W), wcols(SB_W, 4 * HG_W + SB_W),
                  wcols(SB_W, 4 * HG_W + 2 * SB_W), const((SB_W, 1)),
                  const((1, BLOCK)), const((2, HG_W))] + cast_specs,
        out_specs=[rows(D_MODEL), rows(4 * HG_W),
                   pl.BlockSpec((tb, REC_ROWS, BLOCK), lambda i: (i, 0, 0))]
        + cast_specs,
        out_shape=[jax.ShapeDtypeStruct((lp, D_MODEL), F32),
                   jax.ShapeDtypeStruct((lp, 4 * HG_W), F32),
                   jax.ShapeDtypeStruct((n_blk, REC_ROWS, BLOCK), BF16)]
        + [jax.ShapeDtypeStruct(w.shape, BF16) for w in to_cast],
        scratch_shapes=[pltpu.VMEM((tm, D_FF), BF16)],
        compiler_params=pltpu.CompilerParams(
            dimension_semantics=("arbitrary",), vmem_limit_bytes=VMEM_LIMIT),
        name="ffn_in",
    )(x2d, meta_tile, g1, w1i, w1o, gm, win, win, win, win, gk, gq, lbl, *to_cast)


def _hgrn_sum_matrix():
    t = np.arange(BLOCK)[:, None]
    j = np.arange(BLOCK)[None, :]
    mats = [(j <= t)]
    for lvl in range(1, N_FINE_LEVELS):
        c = 1 << lvl
        m = (t // (2 * c)) * (2 * c) + c
        upper = (t >= m) & (j >= m) & (j <= t)
        lower = (t < m) & (j > t) & (j <= m - 1)
        mats.append(upper | lower)
    m = np.concatenate(mats, axis=0).astype(np.float32)
    return np.concatenate([m, m], axis=1)


def _hgrn_stages(hf_ref, rec_ref, gain_ref, m_ref, o_ref, st_ref):
    heads = range(HG_HEADS)
    hs = lambda a, h: a[:, h * HG_DK:(h + 1) * HG_DK]

    q = hf_ref[:, 0:HG_W]
    k = hf_ref[:, 2 * HG_W:3 * HG_W]
    v = jnp.concatenate(
        [rec_ref[REC_HV + h * BLOCK:REC_HV + (h + 1) * BLOCK, :] for h in heads],
        axis=1)
    lf = hf_ref[:, HG_W:2 * HG_W]
    x = _dot(m_ref[...], jnp.concatenate(_split2(lf), axis=0))
    yield
    bcum = x[0:BLOCK]
    b_last = bcum[BLOCK - 1:BLOCK]
    qe = (q * jnp.exp2(bcum)).astype(BF16)
    kd = (k * jnp.exp2(b_last - bcum)).astype(BF16)
    st_decay = jnp.exp2(b_last)

    row = lax.broadcasted_iota(jnp.int32, (BLOCK, BLOCK), 0)
    col = lax.broadcasted_iota(jnp.int32, (BLOCK, BLOCK), 1)
    qb, kb = q.astype(BF16), k.astype(BF16)
    diag = row == col
    attn = [jnp.where(diag, _dot_nt(hs(qb, h), hs(kb, h)), 0.0) for h in heads]
    rowf = lax.broadcasted_iota(jnp.int32, q.shape, 0)
    rowu = lax.broadcasted_iota(jnp.int32, (BLOCK // 2, BLOCK), 0)
    colu = lax.broadcasted_iota(jnp.int32, (BLOCK // 2, BLOCK), 1)
    for lvl in range(N_LEVELS):
        if lvl % 2 == 1:
            yield
        half = 1 << lvl
        if lvl >= N_FINE_LEVELS:
            n_half = BLOCK // half
            qparts, kparts = [], []
            for b in range(n_half):
                rows = slice(b * half, (b + 1) * half)
                if b % 2 == 1:
                    edge = bcum[b * half - 1:b * half]
                    qparts.append(q[rows] * jnp.exp2(bcum[rows] - edge))
                    kparts.append(jnp.zeros((half, q.shape[1]), F32))
                else:
                    edge = bcum[(b + 1) * half - 1:(b + 1) * half]
                    kparts.append(k[rows] * jnp.exp2(edge - bcum[rows]))
            ql = jnp.concatenate(qparts, axis=0).astype(BF16)
            kl = jnp.concatenate(kparts, axis=0).astype(BF16)
            t_up = ((rowu >> lvl) << (lvl + 1)) + half + (rowu & (half - 1))
            same_up = (t_up >> (lvl + 1)) == (colu >> (lvl + 1))
            zero = jnp.zeros((half, BLOCK), F32)
            for h in heads:
                al = _dot_nt(hs(ql, h), hs(kl, h))
                if lvl + 1 < N_LEVELS:
                    al = jnp.where(same_up, al, 0.0)
                attn[h] = attn[h] + jnp.concatenate(
                    [al[(b // 2) * half:(b // 2 + 1) * half] if b % 2 else zero
                     for b in range(n_half)], axis=0)
        else:
            xl = lf if lvl == 0 else x[lvl * BLOCK:(lvl + 1) * BLOCK]
            is_q = ((rowf >> lvl) & 1) == 1
            el = jnp.exp2(jnp.where(is_q, xl, 0.0) if lvl == 0 else xl)
            ql = jnp.where(is_q, q * el, 0.0).astype(BF16)
            kl = jnp.where(is_q, 0.0, k * el).astype(BF16)
            same = (row >> (lvl + 1)) == (col >> (lvl + 1))
            for h in heads:
                al = jnp.where(same, _dot_nt(hs(ql, h), hs(kl, h)), 0.0)
                attn[h] = attn[h] + al
    yield

    o = []
    for h in heads:
        st = st_ref[h]
        oh = _dot(attn[h].astype(BF16), hs(v, h))
        oh = oh + _dot_nt(hs(qe, h), st.astype(BF16))
        st_ref[h] = st * hs(st_decay, h) + lax.dot_general(
            hs(v, h), hs(kd, h), TN_DIMS, preferred_element_type=F32)
        o.append(oh)
    yield
    o = jnp.concatenate([_rms(o[h], hs(gain_ref[...], h)) for h in heads], axis=1)
    o_ref[:, 0:HG_W] = (o * hf_ref[:, 3 * HG_W:4 * HG_W]).astype(BF16)


def _sb_sum_matrix():
    j = np.arange(BLOCK)[:, None]
    s = np.arange(BLOCK)[None, :]
    w = np.concatenate([(j >= s), np.ones((BLOCK, BLOCK), bool)],
                       axis=1).astype(np.float32)
    return np.concatenate([w, w], axis=0)


def _softplus2(z):
    return jnp.maximum(z, 0.0) + jnp.log2(1.0 + jnp.exp2(_neg_abs(z)))


BLOCKS_PER_STEP = 2


def _mixer_kernel(hf_ref, rec_blk_ref, rec_hbm, gain_ref, m_ref, w_ref, o_ref,
                  st_ref, acc_ref, crep_ref, ring, kv_scr, sem, *, n_real_blk):
    n_blk = rec_hbm.shape[0]
    s = pl.program_id(0)
    subs = range(BLOCKS_PER_STEP)
    blk_of = lambda u: jnp.where(s == 0, 0, BLOCKS_PER_STEP * (s - 1) + 1 + u)
    rec_of = lambda u: rec_blk_ref.at[u]
    kv_of = lambda u: rec_blk_ref.at[u, REC_KT:REC_HV]
    rows_of = lambda u: slice(u * BLOCK, (u + 1) * BLOCK)

    def hgrn(u):
        return _hgrn_stages(hf_ref.at[rows_of(u)], rec_of(u), gain_ref, m_ref,
                            o_ref.at[rows_of(u)], st_ref)

    row = lax.broadcasted_iota(jnp.int32, (BLOCK, BLOCK), 0)
    col = lax.broadcasted_iota(jnp.int32, (BLOCK, BLOCK), 1)
    low = col < SB_DH
    pairs = range(SB_PAIRS)
    heads = range(SB_HEADS)

    def fold_stages(u, j_top, tiles, srcs, fresh):
        c = blk_of(u)
        rec_ref, o_u = rec_of(u), o_ref.at[rows_of(u)]
        n = len(tiles)
        nrows = [t[1] for t in tiles]
        assert all(nr == BLOCK for nr in nrows[:-1])
        masks = []
        for i, (kind, nr) in enumerate(tiles):
            assert kind is None or nr == BLOCK
            if kind == "diag":
                m = col < row
            elif kind == "general":
                kpos = (j_top - i) * BLOCK + col
                m = jnp.logical_and(kpos < c * BLOCK + row, kpos >= PAD)
            elif kind == "late_rows":
                m = row >= FAR_ROWS
            else:
                m = None
            masks.append(m)
        z = {}
        for p in pairs:
            kt = jnp.concatenate([srcs[i][_blk(p), :] for i in range(n)],
                                 axis=1)
            zz = _dot(rec_ref[REC_QN + 2 * p * BLOCK:REC_QN + 2 * (p + 1) * BLOCK, :],
                      kt)
            for a in range(2):
                for i in range(n):
                    z[2 * p + a, i] = zz[_blk(a), _blk(i)][:nrows[i]]
        yield
        order = [(hd, i) for hd in heads for i in range(n)]
        packed = []
        for hd, i in order:
            sp = _softplus2(z[hd, i])
            if masks[i] is not None:
                sp = jnp.where(masks[i], sp, 0.0)
            packed.append(jnp.concatenate(_split2(sp), axis=1))
        r = _dot(jnp.concatenate(packed, axis=0), w_ref[...])
        offs = np.cumsum([0] + [nrows[i] for _, i in order])
        yield
        w = {}
        cmax = cmax_rest = None
        for hd in heads:
            crep = None if fresh else crep_ref[u, hd]
            for i in range(n):
                nr = nrows[i]
                o0 = int(offs[hd * n + i])
                rh = r[o0:o0 + nr]
                arg = z[hd, i] - rh[:, :BLOCK]
                wh = jnp.exp2(arg if crep is None else arg + crep[:nr])
                if masks[i] is not None:
                    wh = jnp.where(masks[i], wh, 0.0)
                w[hd, i] = wh.astype(BF16)
                if crep is None:
                    crep = -rh[:, BLOCK:]
                elif nr == BLOCK:
                    crep = crep - rh[:, BLOCK:]
                else:
                    rest = crep[nr:]
                    cmax_rest = (rest if cmax_rest is None
                                 else jnp.maximum(cmax_rest, rest))
                    crep = jnp.concatenate([crep[:nr] - rh[:, BLOCK:], rest], axis=0)
            crep_ref[u, hd] = crep
            cmax = crep if cmax is None else jnp.maximum(cmax, crep)
        yield
        full = [i for i in range(n) if nrows[i] == BLOCK]
        for p in pairs:
            wp = jnp.concatenate(
                [jnp.concatenate([w[2 * p + a, i] for i in full], axis=1)
                 for a in range(2)], axis=0)
            vtile = lambda i: srcs[i][SB_W + p * BLOCK:SB_W + (p + 1) * BLOCK, :]
            pv = _dot(wp, jnp.concatenate([vtile(i) for i in full], axis=0))
            pv = jnp.where(low, pv[:BLOCK], pv[BLOCK:])
            if nrows[-1] < BLOCK:
                nr = nrows[-1]
                ps = _dot(jnp.concatenate([w[2 * p, n - 1], w[2 * p + 1, n - 1]],
                                          axis=0), vtile(n - 1))
                low_nr = lax.broadcasted_iota(jnp.int32, (nr, BLOCK), 1) < SB_DH
                ps = jnp.where(low_nr, ps[:nr], ps[nr:])
                pv = jnp.concatenate([pv[:nr] + ps, pv[nr:]], axis=0)
            if fresh:
                acc_ref[u, p] = pv
                o_u[:, HG_W + p * BLOCK:HG_W + (p + 1) * BLOCK] = pv.astype(BF16)
            else:
                acc_ref[u, p] = acc_ref[u, p] + pv
        return (jnp.max(cmax), jnp.float32(-jnp.inf) if cmax_rest is None
                else jnp.max(cmax_rest))

    n_fast = 3
    fast = s >= 2
    no_more = (jnp.float32(-jnp.inf),) * 2

    def fast_step():
        tiles = (("diag", BLOCK), (None, BLOCK), (None, FAR_ROWS))
        back = lambda u, i: kv_of(u - i) if u >= i else ring.at[u - i + 2]
        def delayed(gen, turns):
            for _ in range(turns):
                yield
            return (yield from gen)

        res = _interleave(
            *[delayed(fold_stages(u, blk_of(u), tiles,
                                  [back(u, i) for i in range(3)], True), 4 * u)
              for u in subs], *[delayed(hgrn(u), 4 * u + 2) for u in subs])
        for i in range(2):
            ring[i] = kv_of(BLOCKS_PER_STEP - 2 + i)[...]
        return sum(res[:BLOCKS_PER_STEP], ())

    def first_steps():
        @pl.when(s == 0)
        def _():
            st_ref[...] = jnp.zeros_like(st_ref)

        def start(u):
            return lambda: _interleave(
                fold_stages(u, blk_of(u), (("general", BLOCK),), [kv_of(u)], True),
                hgrn(u))[0]
        res = start(0)()
        for u in subs[1:]:
            res = res + lax.cond(s >= 1, start(u), lambda: no_more)

        @pl.when(s == 0)
        def _():
            ring[1] = kv_of(0)[...]

        @pl.when(s > 0)
        def _():
            for i in range(2):
                ring[i] = kv_of(BLOCKS_PER_STEP - 2 + i)[...]
        return res

    first = lax.cond(fast, fast_step, first_steps)

    def fold(u, j_top, kind):
        pj = lax.rem(j_top + n_real_blk, n_blk)
        cp = pltpu.make_async_copy(rec_hbm.at[pj, REC_KT:REC_HV], kv_scr, sem.at[0])
        cp.start()
        cp.wait()
        return _interleave(fold_stages(u, j_top, ((kind, BLOCK),), [kv_scr],
                                       False))[0][0]

    def walk_back(u):
        cmax0, cmax_rest = first[2 * u], first[2 * u + 1]
        c = blk_of(u)

        @pl.when(cmax0 > EXP2_ZERO_BELOW)
        def _():
            cmax1 = lax.cond(cmax_rest > EXP2_ZERO_BELOW,
                             lambda: fold(u, c - (n_fast - 1), "late_rows"),
                             lambda: cmax0)

            def cond(carry):
                j, cmax = carry
                return jnp.logical_and(j >= 1, cmax > EXP2_ZERO_BELOW)

            def body(carry):
                j, _ = carry
                return j - 1, fold(u, j, None)

            j0 = jnp.where(fast, c - n_fast, c - 1)
            j_end, cmax_end = lax.while_loop(cond, body, (j0, cmax1))

            @pl.when(jnp.logical_and(j_end == 0, cmax_end > EXP2_ZERO_BELOW))
            def _():
                fold(u, 0, "general")

            for p in pairs:
                o_ref[rows_of(u), HG_W + p * BLOCK:HG_W + (p + 1) * BLOCK] = (
                    acc_ref[u, p].astype(BF16))

    @pl.when(functools.reduce(jnp.maximum, first[0::2]) > EXP2_ZERO_BELOW)
    def _():
        for u in subs:
            walk_back(u)


def _mixer(hf, rec, out_gain, n_real_blk):
    n_blk = rec.shape[0]
    lp = n_blk * BLOCK
    assert n_real_blk % BLOCKS_PER_STEP == 0 and n_blk == n_real_blk + 1
    n_steps = n_real_blk // BLOCKS_PER_STEP + 1
    tile = lambda s: (s + n_steps - 1) % n_steps
    rows = BLOCKS_PER_STEP * BLOCK
    whole = lambda a: pl.BlockSpec(a.shape, lambda s: (0,) * a.ndim)
    msum = jnp.asarray(_hgrn_sum_matrix(), dtype=BF16)
    wsum = jnp.asarray(_sb_sum_matrix(), dtype=BF16)
    return pl.pallas_call(
        functools.partial(_mixer_kernel, n_real_blk=n_real_blk),
        grid=(n_steps,),
        in_specs=[pl.BlockSpec((rows, 4 * HG_W), lambda s: (tile(s), 0)),
                  pl.BlockSpec((BLOCKS_PER_STEP, REC_ROWS, BLOCK),
                               lambda s: (tile(s), 0, 0)),
                  pl.BlockSpec(memory_space=pl.ANY),
                  whole(out_gain), whole(msum), whole(wsum)],
        out_specs=pl.BlockSpec((rows, HG_W + SB_W), lambda s: (tile(s), 0)),
        out_shape=jax.ShapeDtypeStruct((lp, HG_W + SB_W), BF16),
        scratch_shapes=[pltpu.VMEM((HG_HEADS, HG_DV, HG_DK), F32),
                        pltpu.VMEM((BLOCKS_PER_STEP, SB_PAIRS, BLOCK, BLOCK), F32),
                        pltpu.VMEM((BLOCKS_PER_STEP, SB_HEADS, BLOCK, BLOCK), F32),
                        pltpu.VMEM((2, KV_ROWS, BLOCK), BF16),
                        pltpu.VMEM((KV_ROWS, BLOCK), BF16),
                        pltpu.SemaphoreType.DMA((1,))],
        compiler_params=pltpu.CompilerParams(
            dimension_semantics=("arbitrary",)),
        name="mixer",
    )(hf, rec, rec, out_gain, msum, wsum)


def _ffn_out_kernel(h1_ref, o_ref, wo_ref, g2_ref, w2i_ref, w2o_ref, out_ref,
                    act_ref):
    h2 = h1_ref[...] + _dot(o_ref[...], wo_ref[...])
    xn = _rms(h2, g2_ref[...]).astype(BF16)
    out_ref[...] = h2 + 0.5 * _swiglu(xn, w2i_ref, w2o_ref, act_ref)


def _ffn_out(h1, o, wo, g2, w2i, w2o, n_rows, tm):
    const = lambda shape: pl.BlockSpec(shape, lambda i: (0,) * len(shape),
                                       pipeline_mode=pl.Buffered(1))
    rows = lambda w: pl.BlockSpec((tm, w), lambda i: (i, 0))
    return pl.pallas_call(
        _ffn_out_kernel,
        grid=(n_rows // tm,),
        in_specs=[rows(D_MODEL), rows(HG_W + SB_W),
                  const((HG_W + SB_W, D_MODEL)),
                  const((1, D_MODEL)), const((D_MODEL, 2 * D_FF)),
                  const((D_FF, D_MODEL))],
        out_specs=rows(D_MODEL),
        out_shape=jax.ShapeDtypeStruct((n_rows, D_MODEL), F32),
        scratch_shapes=[pltpu.VMEM((tm, D_FF), BF16)],
        compiler_params=pltpu.CompilerParams(
            dimension_semantics=("arbitrary",), vmem_limit_bytes=VMEM_LIMIT),
        name="ffn_out",
    )(h1, o, wo, g2, w2i, w2o)


def kernel(x, meta_tokens, ffn1_norm, ffn1_w_in, ffn1_w_out, mix_norm, w_in,
           hgrn_lb_logits, hgrn_out_norm, sb_q_norm, sb_k_norm, w_out,
           ffn2_norm, ffn2_w_in, ffn2_w_out):
    b, seq, _ = x.shape
    assert b == 1 and seq % BLOCK == 0
    assert ffn1_norm.shape[0] == 1, "single layer"
    n_real_blk = seq // BLOCK
    tm = 512
    assert seq % tm == 0

    meta_tile = jnp.zeros((BLOCK, D_MODEL), x.dtype).at[PAD:].set(
        meta_tokens.astype(x.dtype))

    gk = jnp.tile(sb_k_norm[0], SB_HEADS).reshape(SB_W, 1)
    gq = jnp.tile(sb_q_norm[0], 2).reshape(1, BLOCK)

    h1, hf, rec, wo, w2i, w2o = _ffn_in(
        x[0], meta_tile, ffn1_norm, ffn1_w_in[0].astype(BF16),
        ffn1_w_out[0].astype(BF16), mix_norm, w_in[0].astype(BF16), gk, gq,
        hgrn_lb_logits, (w_out[0], ffn2_w_in[0], ffn2_w_out[0]), tm)
    o = _mixer(hf, rec, hgrn_out_norm, n_real_blk)
    out = _ffn_out(h1, o, wo, ffn2_norm, w2i, w2o, seq, 2 * tm)
    return out[None]
```

```python
import functools

import numpy as np
import jax
import jax.numpy as jnp
from jax import lax
from jax.experimental import pallas as pl
from jax.experimental.pallas import tpu as pltpu

F32 = jnp.float32
BF16 = jnp.bfloat16

D_MODEL = 1024
N_META = 16
BLOCK = 128
PAD = (-N_META) % BLOCK
HG_HEADS = 4
HG_DK = 128
HG_DV = 128
HG_W = HG_HEADS * HG_DK
SB_HEADS = 8
SB_DH = 64
SB_W = SB_HEADS * SB_DH
SB_PAIRS = SB_HEADS // 2
D_FF = 2816
RMS_EPS = 1e-6
FF_CHUNK = 256
N_LEVELS = 7
N_FINE_LEVELS = 3
LOG2E = 1.4426950408889634
EXP2_ZERO_BELOW = -150.0
FAR_ROWS = 48
REC_QN = 0
REC_KT = REC_QN + SB_HEADS * BLOCK
REC_V = REC_KT + SB_W
REC_HV = REC_V + SB_PAIRS * BLOCK
REC_ROWS = REC_HV + HG_HEADS * BLOCK
KV_ROWS = REC_HV - REC_KT
VMEM_LIMIT = 56 * 1024 * 1024
NT_DIMS = (((1,), (1,)), ((), ()))
TN_DIMS = (((0,), (0,)), ((), ()))


def _dot(a, b):
    return jnp.dot(a, b, preferred_element_type=F32)


def _dot_nt(a, b):
    return lax.dot_general(a, b, NT_DIMS, preferred_element_type=F32)


def _rms(x, gain):
    ms = jnp.mean(x * x, axis=-1, keepdims=True)
    return x * lax.rsqrt(ms + RMS_EPS) * gain


def _split2(x):
    hi = x.astype(BF16)
    lo = (x - hi.astype(F32)).astype(BF16)
    return hi, lo


def _neg_abs(x):
    return -jnp.abs(x)


def _silu(x):
    return x * jax.nn.sigmoid(x)


def _blk(i):
    return slice(i * BLOCK, (i + 1) * BLOCK)


def _swiglu_stages(xn, w_in_ref, w_out_ref, act_ref):
    for c in range(D_FF // FF_CHUNK):
        lo, hi = c * FF_CHUNK, (c + 1) * FF_CHUNK
        g = _dot(xn, w_in_ref[:, lo:hi])
        u = _dot(xn, w_in_ref[:, D_FF + lo:D_FF + hi])
        act_ref[:, lo:hi] = (_silu(g) * u).astype(BF16)
        yield
    return _dot(act_ref[...], w_out_ref[...])


def _interleave(*stage_lists):
    results = [None] * len(stage_lists)
    active = dict(enumerate(stage_lists))
    while active:
        for i, g in list(active.items()):
            try:
                next(g)
            except StopIteration as stop:
                results[i] = stop.value
                del active[i]
    return results


def _swiglu(xn, w_in_ref, w_out_ref, act_ref):
    return _interleave(_swiglu_stages(xn, w_in_ref, w_out_ref, act_ref))[0]


def _ffn_in_tile(h, pads, g1_ref, w1i_ref, w1o_ref, gm_ref, whg_ref, wq_ref,
                 wk_ref, wv_ref, gk_ref, gq_ref, lbl_ref,
                 h1_ref, hf_ref, rec_ref, act_ref):
    n = h.shape[0]
    rows = slice(0, n)
    xn = _rms(h, g1_ref[...]).astype(BF16)
    h1 = h + 0.5 * _swiglu(xn, w1i_ref, w1o_ref, act_ref.at[rows])
    h1_ref[rows] = h1
    xm = _rms(h1, gm_ref[...]).astype(BF16)

    def put(base, piece, val):
        for t in range(n // BLOCK):
            lo = base + piece * BLOCK
            rec_ref[t, lo:lo + BLOCK, :] = val[_blk(t)]

    q = _dot(xm, wq_ref[...])
    low = lax.broadcasted_iota(jnp.int32, (n, BLOCK), 1) < SB_DH
    qscale = gq_ref[...] * (LOG2E / np.sqrt(np.float32(SB_DH)))
    for p in range(SB_PAIRS):
        qp = q[:, _blk(p)]
        for a in range(2):
            own = low if a == 0 else jnp.logical_not(low)
            ms = jnp.sum(jnp.where(own, qp * qp, 0.0), axis=-1,
                         keepdims=True) * (1.0 / SB_DH)
            qn = jnp.where(own, qp * lax.rsqrt(ms + RMS_EPS) * qscale, 0.0)
            put(REC_QN, 2 * p + a, qn.astype(BF16))
    v = _dot(xm, wv_ref[...]).astype(BF16)
    for p in range(SB_PAIRS):
        put(REC_V, p, v[:, _blk(p)])
    kt = lax.dot_general(wk_ref[...], xm, (((0,), (1,)), ((), ())),
                         preferred_element_type=F32)
    k3 = kt.reshape(SB_HEADS, SB_DH, n)
    ms = jnp.mean(k3 * k3, axis=1, keepdims=True)
    kn = (k3 * lax.rsqrt(ms + RMS_EPS)).reshape(SB_W, n) * gk_ref[...]
    kn = kn.astype(BF16)
    for t in range(n // BLOCK):
        rec_ref[t, REC_KT:REC_KT + SB_W, :] = kn[:, _blk(t)]

    part = lambda i: _dot(xm, whg_ref[:, i * HG_W:(i + 1) * HG_W])
    hv = part(2).astype(BF16)
    for hd in range(HG_HEADS):
        put(REC_HV, hd, hv[:, _blk(hd)])
    hf_ref[rows, 0:HG_W] = _silu(part(0))
    hf_ref[rows, 3 * HG_W:4 * HG_W] = _silu(part(3))
    lg = lbl_ref[...]
    e = jnp.exp(lg - jnp.max(lg, axis=0, keepdims=True))
    lb = e[0:1] / jnp.sum(e, axis=0, keepdims=True)
    z = part(1)
    ez = jnp.exp(_neg_abs(z))
    rz = 1.0 / (1.0 + ez)
    erz = ez * rz
    pos = z >= 0.0
    lf = jnp.log2(lb + (1.0 - lb) * jnp.where(pos, rz, erz))
    k = (1.0 - lb) * jnp.where(pos, erz, rz)
    if pads:
        valid = lax.broadcasted_iota(jnp.int32, z.shape, 0) >= PAD
        lf, k = jnp.where(valid, lf, 0.0), jnp.where(valid, k, 0.0)
    hf_ref[rows, HG_W:2 * HG_W] = lf
    hf_ref[rows, 2 * HG_W:3 * HG_W] = k


N_FFN_IN_PARAMS = 11


def _ffn_in_kernel(x_ref, meta_ref, *refs, cast_chunks):
    nc = len(cast_chunks)
    params = refs[:N_FFN_IN_PARAMS]
    cast_in = refs[N_FFN_IN_PARAMS:N_FFN_IN_PARAMS + nc]
    outs = refs[N_FFN_IN_PARAMS + nc:N_FFN_IN_PARAMS + nc + 3]
    cast_out = refs[N_FFN_IN_PARAMS + nc + 3:N_FFN_IN_PARAMS + 2 * nc + 3]
    act_ref = refs[-1]
    i = pl.program_id(0)
    is_meta = i == pl.num_programs(0) - 1

    for src, dst, chunks in zip(cast_in, cast_out, cast_chunks):
        @pl.when(i < chunks)
        def _():
            dst[...] = src[...].astype(BF16)

    @pl.when(jnp.logical_not(is_meta))
    def _():
        _ffn_in_tile(x_ref[...], False, *params, *outs, act_ref)

    @pl.when(is_meta)
    def _():
        _ffn_in_tile(meta_ref[...], True, *params, *outs, act_ref)


def _cast_chunks(n_rows, n_steps):
    return max(k for k in range(1, n_steps + 1)
               if n_rows % k == 0 and (n_rows // k) % 16 == 0)


def _ffn_in(x2d, meta_tile, g1, w1i, w1o, gm, win, gk, gq, lbl, to_cast, tm):
    n_real_tiles = x2d.shape[0] // tm
    lp = x2d.shape[0] + BLOCK
    n_blk = lp // BLOCK
    tb = tm // BLOCK
    const = lambda shape: pl.BlockSpec(shape, lambda i: (0,) * len(shape),
                                       pipeline_mode=pl.Buffered(1))
    rows = lambda w: pl.BlockSpec((tm, w), lambda i: (i, 0))
    wcols = lambda width, start: pl.BlockSpec(
        (D_MODEL, width), lambda i: (0, start // width),
        pipeline_mode=pl.Buffered(1))
    chunks = tuple(_cast_chunks(w.shape[0], n_real_tiles + 1) for w in to_cast)
    cast_specs = [
        pl.BlockSpec((w.shape[0] // k, w.shape[1]),
                     lambda i, k=k: (jnp.minimum(i, k - 1), 0))
        for w, k in zip(to_cast, chunks)]
    return pl.pallas_call(
        functools.partial(_ffn_in_kernel, cast_chunks=chunks),
        grid=(n_real_tiles + 1,),
        in_specs=[pl.BlockSpec((tm, D_MODEL),
                               lambda i: (jnp.minimum(i, n_real_tiles - 1), 0)),
                  const((BLOCK, D_MODEL)), const((1, D_MODEL)),
                  const((D_MODEL, 2 * D_FF)), const((D_FF, D_MODEL)),
                  const((1, D_MODEL)), wcols(4 * HG_W, 0),
                  wcols(SB_W, 4 * HG_W), wcols(SB_W, 4 * HG_W + SB_W),
                  wcols(SB_W, 4 * HG_W + 2 * SB_W), const((SB_W, 1)),
                  const((1, BLOCK)), const((2, HG_W))] + cast_specs,
        out_specs=[rows(D_MODEL), rows(4 * HG_W),
                   pl.BlockSpec((tb, REC_ROWS, BLOCK), lambda i: (i, 0, 0))]
        + cast_specs,
        out_shape=[jax.ShapeDtypeStruct((lp, D_MODEL), F32),
                   jax.ShapeDtypeStruct((lp, 4 * HG_W), F32),
                   jax.ShapeDtypeStruct((n_blk, REC_ROWS, BLOCK), BF16)]
        + [jax.ShapeDtypeStruct(w.shape, BF16) for w in to_cast],
        scratch_shapes=[pltpu.VMEM((tm, D_FF), BF16)],
        compiler_params=pltpu.CompilerParams(
            dimension_semantics=("arbitrary",), vmem_limit_bytes=VMEM_LIMIT),
        name="ffn_in",
    )(x2d, meta_tile, g1, w1i, w1o, gm, win, win, win, win, gk, gq, lbl, *to_cast)


def _hgrn_sum_matrix():
    t = np.arange(BLOCK)[:, None]
    j = np.arange(BLOCK)[None, :]
    mats = [(j <= t)]
    for lvl in range(1, N_FINE_LEVELS):
        c = 1 << lvl
        m = (t // (2 * c)) * (2 * c) + c
        upper = (t >= m) & (j >= m) & (j <= t)
        lower = (t < m) & (j > t) & (j <= m - 1)
        mats.append(upper | lower)
    m = np.concatenate(mats, axis=0).astype(np.float32)
    return np.concatenate([m, m], axis=1)


def _hgrn_stages(hf_ref, rec_ref, gain_ref, m_ref, o_ref, st_ref):
    heads = range(HG_HEADS)
    hs = lambda a, h: a[:, h * HG_DK:(h + 1) * HG_DK]

    q = hf_ref[:, 0:HG_W]
    k = hf_ref[:, 2 * HG_W:3 * HG_W]
    v = jnp.concatenate(
        [rec_ref[REC_HV + h * BLOCK:REC_HV + (h + 1) * BLOCK, :] for h in heads],
        axis=1)
    lf = hf_ref[:, HG_W:2 * HG_W]
    x = _dot(m_ref[...], jnp.concatenate(_split2(lf), axis=0))
    yield
    bcum = x[0:BLOCK]
    b_last = bcum[BLOCK - 1:BLOCK]
    qe = (q * jnp.exp2(bcum)).astype(BF16)
    kd = (k * jnp.exp2(b_last - bcum)).astype(BF16)
    st_decay = jnp.exp2(b_last)

    row = lax.broadcasted_iota(jnp.int32, (BLOCK, BLOCK), 0)
    col = lax.broadcasted_iota(jnp.int32, (BLOCK, BLOCK), 1)
    qb, kb = q.astype(BF16), k.astype(BF16)
    diag = row == col
    attn = [jnp.where(diag, _dot_nt(hs(qb, h), hs(kb, h)), 0.0) for h in heads]
    rowf = lax.broadcasted_iota(jnp.int32, q.shape, 0)
    rowu = lax.broadcasted_iota(jnp.int32, (BLOCK // 2, BLOCK), 0)
    colu = lax.broadcasted_iota(jnp.int32, (BLOCK // 2, BLOCK), 1)
    for lvl in range(N_LEVELS):
        if lvl % 2 == 1:
            yield
        half = 1 << lvl
        if lvl >= N_FINE_LEVELS:
            n_half = BLOCK // half
            qparts, kparts = [], []
            for b in range(n_half):
                rows = slice(b * half, (b + 1) * half)
                if b % 2 == 1:
                    edge = bcum[b * half - 1:b * half]
                    qparts.append(q[rows] * jnp.exp2(bcum[rows] - edge))
                    kparts.append(jnp.zeros((half, q.shape[1]), F32))
                else:
                    edge = bcum[(b + 1) * half - 1:(b + 1) * half]
                    kparts.append(k[rows] * jnp.exp2(edge - bcum[rows]))
            ql = jnp.concatenate(qparts, axis=0).astype(BF16)
            kl = jnp.concatenate(kparts, axis=0).astype(BF16)
            t_up = ((rowu >> lvl) << (lvl + 1)) + half + (rowu & (half - 1))
            same_up = (t_up >> (lvl + 1)) == (colu >> (lvl + 1))
            zero = jnp.zeros((half, BLOCK), F32)
            for h in heads:
                al = _dot_nt(hs(ql, h), hs(kl, h))
                if lvl + 1 < N_LEVELS:
                    al = jnp.where(same_up, al, 0.0)
                attn[h] = attn[h] + jnp.concatenate(
                    [al[(b // 2) * half:(b // 2 + 1) * half] if b % 2 else zero
                     for b in range(n_half)], axis=0)
        else:
            xl = lf if lvl == 0 else x[lvl * BLOCK:(lvl + 1) * BLOCK]
            is_q = ((rowf >> lvl) & 1) == 1
            el = jnp.exp2(jnp.where(is_q, xl, 0.0) if lvl == 0 else xl)
            ql = jnp.where(is_q, q * el, 0.0).astype(BF16)
            kl = jnp.where(is_q, 0.0, k * el).astype(BF16)
            same = (row >> (lvl + 1)) == (col >> (lvl + 1))
            for h in heads:
                al = jnp.where(same, _dot_nt(hs(ql, h), hs(kl, h)), 0.0)
                attn[h] = attn[h] + al
    yield

    o = []
    for h in heads:
        st = st_ref[h]
        oh = _dot(attn[h].astype(BF16), hs(v, h))
        oh = oh + _dot_nt(hs(qe, h), st.astype(BF16))
        st_ref[h] = st * hs(st_decay, h) + lax.dot_general(
            hs(v, h), hs(kd, h), TN_DIMS, preferred_element_type=F32)
        o.append(oh)
    yield
    o = jnp.concatenate([_rms(o[h], hs(gain_ref[...], h)) for h in heads], axis=1)
    o_ref[:, 0:HG_W] = (o * hf_ref[:, 3 * HG_W:4 * HG_W]).astype(BF16)


def _sb_sum_matrix():
    j = np.arange(BLOCK)[:, None]
    s = np.arange(BLOCK)[None, :]
    w = np.concatenate([(j >= s), np.ones((BLOCK, BLOCK), bool)],
                       axis=1).astype(np.float32)
    return np.concatenate([w, w], axis=0)


def _softplus2(z):
    return jnp.maximum(z, 0.0) + jnp.log2(1.0 + jnp.exp2(_neg_abs(z)))


BLOCKS_PER_STEP = 4


def _mixer_kernel(hf_ref, rec_blk_ref, rec_hbm, gain_ref, m_ref, w_ref, o_ref,
                  st_ref, acc_ref, crep_ref, ring, kv_scr, sem, *, n_real_blk):
    n_blk = rec_hbm.shape[0]
    s = pl.program_id(0)
    subs = range(BLOCKS_PER_STEP)
    blk_of = lambda u: jnp.where(s == 0, 0, BLOCKS_PER_STEP * (s - 1) + 1 + u)
    rec_of = lambda u: rec_blk_ref.at[u]
    kv_of = lambda u: rec_blk_ref.at[u, REC_KT:REC_HV]
    rows_of = lambda u: slice(u * BLOCK, (u + 1) * BLOCK)

    def hgrn(u):
        return _hgrn_stages(hf_ref.at[rows_of(u)], rec_of(u), gain_ref, m_ref,
                            o_ref.at[rows_of(u)], st_ref)

    row = lax.broadcasted_iota(jnp.int32, (BLOCK, BLOCK), 0)
    col = lax.broadcasted_iota(jnp.int32, (BLOCK, BLOCK), 1)
    low = col < SB_DH
    pairs = range(SB_PAIRS)
    heads = range(SB_HEADS)

    def fold_stages(u, j_top, tiles, srcs, fresh):
        c = blk_of(u)
        rec_ref, o_u = rec_of(u), o_ref.at[rows_of(u)]
        n = len(tiles)
        nrows = [t[1] for t in tiles]
        assert all(nr == BLOCK for nr in nrows[:-1])
        masks = []
        for i, (kind, nr) in enumerate(tiles):
            assert kind is None or nr == BLOCK
            if kind == "diag":
                m = col < row
            elif kind == "general":
                kpos = (j_top - i) * BLOCK + col
                m = jnp.logical_and(kpos < c * BLOCK + row, kpos >= PAD)
            elif kind == "late_rows":
                m = row >= FAR_ROWS
            else:
                m = None
            masks.append(m)
        z = {}
        for p in pairs:
            kt = jnp.concatenate([srcs[i][_blk(p), :] for i in range(n)],
                                 axis=1)
            zz = _dot(rec_ref[REC_QN + 2 * p * BLOCK:REC_QN + 2 * (p + 1) * BLOCK, :],
                      kt)
            for a in range(2):
                for i in range(n):
                    z[2 * p + a, i] = zz[_blk(a), _blk(i)][:nrows[i]]
        yield
        order = [(hd, i) for hd in heads for i in range(n)]
        packed = []
        for hd, i in order:
            sp = _softplus2(z[hd, i])
            if masks[i] is not None:
                sp = jnp.where(masks[i], sp, 0.0)
            packed.append(jnp.concatenate(_split2(sp), axis=1))
        r = _dot(jnp.concatenate(packed, axis=0), w_ref[...])
        offs = np.cumsum([0] + [nrows[i] for _, i in order])
        yield
        w = {}
        cmax = cmax_rest = None
        for hd in heads:
            crep = None if fresh else crep_ref[u, hd]
            for i in range(n):
                nr = nrows[i]
                o0 = int(offs[hd * n + i])
                rh = r[o0:o0 + nr]
                arg = z[hd, i] - rh[:, :BLOCK]
                wh = jnp.exp2(arg if crep is None else arg + crep[:nr])
                if masks[i] is not None:
                    wh = jnp.where(masks[i], wh, 0.0)
                w[hd, i] = wh.astype(BF16)
                if crep is None:
                    crep = -rh[:, BLOCK:]
                elif nr == BLOCK:
                    crep = crep - rh[:, BLOCK:]
                else:
                    rest = crep[nr:]
                    cmax_rest = (rest if cmax_rest is None
                                 else jnp.maximum(cmax_rest, rest))
                    crep = jnp.concatenate([crep[:nr] - rh[:, BLOCK:], rest], axis=0)
            crep_ref[u, hd] = crep
            cmax = crep if cmax is None else jnp.maximum(cmax, crep)
        yield
        full = [i for i in range(n) if nrows[i] == BLOCK]
        for p in pairs:
            wp = jnp.concatenate(
                [jnp.concatenate([w[2 * p + a, i] for i in full], axis=1)
                 for a in range(2)], axis=0)
            vtile = lambda i: srcs[i][SB_W + p * BLOCK:SB_W + (p + 1) * BLOCK, :]
            pv = _dot(wp, jnp.concatenate([vtile(i) for i in full], axis=0))
            pv = jnp.where(low, pv[:BLOCK], pv[BLOCK:])
            if nrows[-1] < BLOCK:
                nr = nrows[-1]
                ps = _dot(jnp.concatenate([w[2 * p, n - 1], w[2 * p + 1, n - 1]],
                                          axis=0), vtile(n - 1))
                low_nr = lax.broadcasted_iota(jnp.int32, (nr, BLOCK), 1) < SB_DH
                ps = jnp.where(low_nr, ps[:nr], ps[nr:])
                pv = jnp.concatenate([pv[:nr] + ps, pv[nr:]], axis=0)
            if fresh:
                acc_ref[u, p] = pv
                o_u[:, HG_W + p * BLOCK:HG_W + (p + 1) * BLOCK] = pv.astype(BF16)
            else:
                acc_ref[u, p] = acc_ref[u, p] + pv
        return (jnp.max(cmax), jnp.float32(-jnp.inf) if cmax_rest is None
                else jnp.max(cmax_rest))

    n_fast = 3
    fast = s >= 2
    no_more = (jnp.float32(-jnp.inf),) * 2

    fast_tiles = (("diag", BLOCK), (None, BLOCK), (None, FAR_ROWS))

    def fast_step():
        tiles = fast_tiles
        back = lambda u, i: kv_of(u - i) if u >= i else ring.at[u - i + 2]
        def delayed(gen, turns):
            for _ in range(turns):
                yield
            return (yield from gen)

        res = _interleave(
            *[delayed(fold_stages(u, blk_of(u), tiles,
                                  [back(u, i) for i in range(3)], True), 4 * u)
              for u in subs], *[delayed(hgrn(u), 4 * u + 2) for u in subs])
        for i in range(2):
            ring[i] = kv_of(BLOCKS_PER_STEP - 2 + i)[...]
        return sum(res[:BLOCKS_PER_STEP], ())

    def first_steps():
        @pl.when(s == 0)
        def _():
            st_ref[...] = jnp.zeros_like(st_ref)

        def start(u):
            if u >= n_fast - 1:
                tiles, srcs = fast_tiles, [kv_of(u - i) for i in range(3)]
            else:
                tiles, srcs = (("general", BLOCK),), [kv_of(u)]
            return lambda: _interleave(
                fold_stages(u, blk_of(u), tiles, srcs, True), hgrn(u))[0]
        res = start(0)()
        for u in subs[1:]:
            res = res + lax.cond(s >= 1, start(u), lambda: no_more)

        @pl.when(s == 0)
        def _():
            ring[1] = kv_of(0)[...]

        @pl.when(s > 0)
        def _():
            for i in range(2):
                ring[i] = kv_of(BLOCKS_PER_STEP - 2 + i)[...]
        return res

    first = lax.cond(fast, fast_step, first_steps)

    def fold(u, j_top, kind):
        pj = lax.rem(j_top + n_real_blk, n_blk)
        cp = pltpu.make_async_copy(rec_hbm.at[pj, REC_KT:REC_HV], kv_scr, sem.at[0])
        cp.start()
        cp.wait()
        return _interleave(fold_stages(u, j_top, ((kind, BLOCK),), [kv_scr],
                                       False))[0][0]

    def walk_back(u):
        cmax0, cmax_rest = first[2 * u], first[2 * u + 1]
        c = blk_of(u)

        @pl.when(cmax0 > EXP2_ZERO_BELOW)
        def _():
            cmax1 = lax.cond(cmax_rest > EXP2_ZERO_BELOW,
                             lambda: fold(u, c - (n_fast - 1), "late_rows"),
                             lambda: cmax0)

            def cond(carry):
                j, cmax = carry
                return jnp.logical_and(j >= 1, cmax > EXP2_ZERO_BELOW)

            def body(carry):
                j, _ = carry
                return j - 1, fold(u, j, None)

            j0 = jnp.where(c >= n_fast, c - n_fast, c - 1)
            j_end, cmax_end = lax.while_loop(cond, body, (j0, cmax1))

            @pl.when(jnp.logical_and(j_end == 0, cmax_end > EXP2_ZERO_BELOW))
            def _():
                fold(u, 0, "general")

            for p in pairs:
                o_ref[rows_of(u), HG_W + p * BLOCK:HG_W + (p + 1) * BLOCK] = (
                    acc_ref[u, p].astype(BF16))

    @pl.when(functools.reduce(jnp.maximum, first[0::2]) > EXP2_ZERO_BELOW)
    def _():
        for u in subs:
            walk_back(u)


def _mixer(hf, rec, out_gain, n_real_blk):
    n_blk = rec.shape[0]
    lp = n_blk * BLOCK
    assert n_real_blk % BLOCKS_PER_STEP == 0 and n_blk == n_real_blk + 1
    n_steps = n_real_blk // BLOCKS_PER_STEP + 1
    tile = lambda s: (s + n_steps - 1) % n_steps
    rows = BLOCKS_PER_STEP * BLOCK
    whole = lambda a: pl.BlockSpec(a.shape, lambda s: (0,) * a.ndim)
    msum = jnp.asarray(_hgrn_sum_matrix(), dtype=BF16)
    wsum = jnp.asarray(_sb_sum_matrix(), dtype=BF16)
    return pl.pallas_call(
        functools.partial(_mixer_kernel, n_real_blk=n_real_blk),
        grid=(n_steps,),
        in_specs=[pl.BlockSpec((rows, 4 * HG_W), lambda s: (tile(s), 0)),
                  pl.BlockSpec((BLOCKS_PER_STEP, REC_ROWS, BLOCK),
                               lambda s: (tile(s), 0, 0)),
                  pl.BlockSpec(memory_space=pl.ANY),
                  whole(out_gain), whole(msum), whole(wsum)],
        out_specs=pl.BlockSpec((rows, HG_W + SB_W), lambda s: (tile(s), 0)),
        out_shape=jax.ShapeDtypeStruct((lp, HG_W + SB_W), BF16),
        scratch_shapes=[pltpu.VMEM((HG_HEADS, HG_DV, HG_DK), F32),
                        pltpu.VMEM((BLOCKS_PER_STEP, SB_PAIRS, BLOCK, BLOCK), F32),
                        pltpu.VMEM((BLOCKS_PER_STEP, SB_HEADS, BLOCK, BLOCK), F32),
                        pltpu.VMEM((2, KV_ROWS, BLOCK), BF16),
                        pltpu.VMEM((KV_ROWS, BLOCK), BF16),
                        pltpu.SemaphoreType.DMA((1,))],
        compiler_params=pltpu.CompilerParams(
            dimension_semantics=("arbitrary",)),
        name="mixer",
    )(hf, rec, rec, out_gain, msum, wsum)


def _ffn_out_kernel(h1_ref, o_ref, wo_ref, g2_ref, w2i_ref, w2o_ref, out_ref,
                    act_ref):
    h2 = h1_ref[...] + _dot(o_ref[...], wo_ref[...])
    xn = _rms(h2, g2_ref[...]).astype(BF16)
    out_ref[...] = h2 + 0.5 * _swiglu(xn, w2i_ref, w2o_ref, act_ref)


def _ffn_out(h1, o, wo, g2, w2i, w2o, n_rows, tm):
    const = lambda shape: pl.BlockSpec(shape, lambda i: (0,) * len(shape),
                                       pipeline_mode=pl.Buffered(1))
    rows = lambda w: pl.BlockSpec((tm, w), lambda i: (i, 0))
    return pl.pallas_call(
        _ffn_out_kernel,
        grid=(n_rows // tm,),
        in_specs=[rows(D_MODEL), rows(HG_W + SB_W),
                  const((HG_W + SB_W, D_MODEL)),
                  const((1, D_MODEL)), const((D_MODEL, 2 * D_FF)),
                  const((D_FF, D_MODEL))],
        out_specs=rows(D_MODEL),
        out_shape=jax.ShapeDtypeStruct((n_rows, D_MODEL), F32),
        scratch_shapes=[pltpu.VMEM((tm, D_FF), BF16)],
        compiler_params=pltpu.CompilerParams(
            dimension_semantics=("arbitrary",), vmem_limit_bytes=VMEM_LIMIT),
        name="ffn_out",
    )(h1, o, wo, g2, w2i, w2o)


def kernel(x, meta_tokens, ffn1_norm, ffn1_w_in, ffn1_w_out, mix_norm, w_in,
           hgrn_lb_logits, hgrn_out_norm, sb_q_norm, sb_k_norm, w_out,
           ffn2_norm, ffn2_w_in, ffn2_w_out):
    b, seq, _ = x.shape
    assert b == 1 and seq % BLOCK == 0
    assert ffn1_norm.shape[0] == 1, "single layer"
    n_real_blk = seq // BLOCK
    tm = 512
    assert seq % tm == 0

    meta_tile = jnp.zeros((BLOCK, D_MODEL), x.dtype).at[PAD:].set(
        meta_tokens.astype(x.dtype))

    gk = jnp.tile(sb_k_norm[0], SB_HEADS).reshape(SB_W, 1)
    gq = jnp.tile(sb_q_norm[0], 2).reshape(1, BLOCK)

    h1, hf, rec, wo, w2i, w2o = _ffn_in(
        x[0], meta_tile, ffn1_norm, ffn1_w_in[0].astype(BF16),
        ffn1_w_out[0].astype(BF16), mix_norm, w_in[0].astype(BF16), gk, gq,
        hgrn_lb_logits, (w_out[0], ffn2_w_in[0], ffn2_w_out[0]), tm)
    o = _mixer(hf, rec, hgrn_out_norm, n_real_blk)
    out = _ffn_out(h1, o, wo, ffn2_norm, w2i, w2o, seq, 2 * tm)
    return out[None]
```

```python
import functools

import numpy as np
import jax
import jax.numpy as jnp
from jax import lax
from jax.experimental import pallas as pl
from jax.experimental.pallas import tpu as pltpu

F32 = jnp.float32
BF16 = jnp.bfloat16

D_MODEL = 1024
N_META = 16
BLOCK = 128
PAD = (-N_META) % BLOCK
HG_HEADS = 4
HG_DK = 128
HG_DV = 128
HG_W = HG_HEADS * HG_DK
SB_HEADS = 8
SB_DH = 64
SB_W = SB_HEADS * SB_DH
SB_PAIRS = SB_HEADS // 2
D_FF = 2816
RMS_EPS = 1e-6
FF_CHUNK = 256
N_LEVELS = 7
N_FINE_LEVELS = 3
LOG2E = 1.4426950408889634
EXP2_ZERO_BELOW = -150.0
FAR_ROWS = 48
REC_QN = 0
REC_KT = REC_QN + SB_HEADS * BLOCK
REC_V = REC_KT + SB_W
REC_HV = REC_V + SB_PAIRS * BLOCK
REC_ROWS = REC_HV + HG_HEADS * BLOCK
KV_ROWS = REC_HV - REC_KT
VMEM_LIMIT = 56 * 1024 * 1024
NT_DIMS = (((1,), (1,)), ((), ()))
TN_DIMS = (((0,), (0,)), ((), ()))


def _dot(a, b):
    return jnp.dot(a, b, preferred_element_type=F32)


def _dot_nt(a, b):
    return lax.dot_general(a, b, NT_DIMS, preferred_element_type=F32)


def _rms(x, gain):
    ms = jnp.mean(x * x, axis=-1, keepdims=True)
    return x * lax.rsqrt(ms + RMS_EPS) * gain


def _split2(x):
    hi = x.astype(BF16)
    lo = (x - hi.astype(F32)).astype(BF16)
    return hi, lo


def _neg_abs(x):
    return -jnp.abs(x)


def _silu(x):
    return x * jax.nn.sigmoid(x)


def _blk(i):
    return slice(i * BLOCK, (i + 1) * BLOCK)


def _swiglu_stages(xn, w_in_ref, w_out_ref, act_ref):
    for c in range(D_FF // FF_CHUNK):
        lo, hi = c * FF_CHUNK, (c + 1) * FF_CHUNK
        g = _dot(xn, w_in_ref[:, lo:hi])
        u = _dot(xn, w_in_ref[:, D_FF + lo:D_FF + hi])
        act_ref[:, lo:hi] = (_silu(g) * u).astype(BF16)
        yield
    return _dot(act_ref[...], w_out_ref[...])


def _interleave(*stage_lists):
    results = [None] * len(stage_lists)
    active = dict(enumerate(stage_lists))
    while active:
        for i, g in list(active.items()):
            try:
                next(g)
            except StopIteration as stop:
                results[i] = stop.value
                del active[i]
    return results


def _swiglu(xn, w_in_ref, w_out_ref, act_ref):
    return _interleave(_swiglu_stages(xn, w_in_ref, w_out_ref, act_ref))[0]


def _ffn_in_tile(h, pads, g1_ref, w1i_ref, w1o_ref, gm_ref, whg_ref, wq_ref,
                 wk_ref, wv_ref, gk_ref, gq_ref, lbl_ref,
                 h1_ref, hf_ref, rec_ref, act_ref):
    n = h.shape[0]
    rows = slice(0, n)
    xn = _rms(h, g1_ref[...]).astype(BF16)
    h1 = h + 0.5 * _swiglu(xn, w1i_ref, w1o_ref, act_ref.at[rows])
    h1_ref[rows] = h1
    xm = _rms(h1, gm_ref[...]).astype(BF16)

    def put(base, piece, val):
        for t in range(n // BLOCK):
            lo = base + piece * BLOCK
            rec_ref[t, lo:lo + BLOCK, :] = val[_blk(t)]

    q = _dot(xm, wq_ref[...])
    low = lax.broadcasted_iota(jnp.int32, (n, BLOCK), 1) < SB_DH
    qscale = gq_ref[...] * (LOG2E / np.sqrt(np.float32(SB_DH)))
    for p in range(SB_PAIRS):
        qp = q[:, _blk(p)]
        for a in range(2):
            own = low if a == 0 else jnp.logical_not(low)
            ms = jnp.sum(jnp.where(own, qp * qp, 0.0), axis=-1,
                         keepdims=True) * (1.0 / SB_DH)
            qn = jnp.where(own, qp * lax.rsqrt(ms + RMS_EPS) * qscale, 0.0)
            put(REC_QN, 2 * p + a, qn.astype(BF16))
    v = _dot(xm, wv_ref[...]).astype(BF16)
    for p in range(SB_PAIRS):
        put(REC_V, p, v[:, _blk(p)])
    kt = lax.dot_general(wk_ref[...], xm, (((0,), (1,)), ((), ())),
                         preferred_element_type=F32)
    k3 = kt.reshape(SB_HEADS, SB_DH, n)
    ms = jnp.mean(k3 * k3, axis=1, keepdims=True)
    kn = (k3 * lax.rsqrt(ms + RMS_EPS)).reshape(SB_W, n) * gk_ref[...]
    kn = kn.astype(BF16)
    for t in range(n // BLOCK):
        rec_ref[t, REC_KT:REC_KT + SB_W, :] = kn[:, _blk(t)]

    part = lambda i: _dot(xm, whg_ref[:, i * HG_W:(i + 1) * HG_W])
    hv = part(2).astype(BF16)
    for hd in range(HG_HEADS):
        put(REC_HV, hd, hv[:, _blk(hd)])
    hf_ref[rows, 0:HG_W] = _silu(part(0))
    hf_ref[rows, 3 * HG_W:4 * HG_W] = _silu(part(3))
    lg = lbl_ref[...]
    e = jnp.exp(lg - jnp.max(lg, axis=0, keepdims=True))
    lb = e[0:1] / jnp.sum(e, axis=0, keepdims=True)
    z = part(1)
    ez = jnp.exp(_neg_abs(z))
    rz = 1.0 / (1.0 + ez)
    erz = ez * rz
    pos = z >= 0.0
    lf = jnp.log2(lb + (1.0 - lb) * jnp.where(pos, rz, erz))
    k = (1.0 - lb) * jnp.where(pos, erz, rz)
    if pads:
        valid = lax.broadcasted_iota(jnp.int32, z.shape, 0) >= PAD
        lf, k = jnp.where(valid, lf, 0.0), jnp.where(valid, k, 0.0)
    hf_ref[rows, HG_W:2 * HG_W] = lf
    hf_ref[rows, 2 * HG_W:3 * HG_W] = k


N_FFN_IN_PARAMS = 11


def _ffn_in_kernel(x_ref, meta_ref, *refs, cast_chunks):
    nc = len(cast_chunks)
    params = refs[:N_FFN_IN_PARAMS]
    cast_in = refs[N_FFN_IN_PARAMS:N_FFN_IN_PARAMS + nc]
    outs = refs[N_FFN_IN_PARAMS + nc:N_FFN_IN_PARAMS + nc + 3]
    cast_out = refs[N_FFN_IN_PARAMS + nc + 3:N_FFN_IN_PARAMS + 2 * nc + 3]
    act_ref = refs[-1]
    i = pl.program_id(0)
    is_meta = i == pl.num_programs(0) - 1

    for src, dst, chunks in zip(cast_in, cast_out, cast_chunks):
        @pl.when(i < chunks)
        def _():
            dst[...] = src[...].astype(BF16)

    @pl.when(jnp.logical_not(is_meta))
    def _():
        _ffn_in_tile(x_ref[...], False, *params, *outs, act_ref)

    @pl.when(is_meta)
    def _():
        _ffn_in_tile(meta_ref[...], True, *params, *outs, act_ref)


def _cast_chunks(n_rows, n_steps):
    return max(k for k in range(1, n_steps + 1)
               if n_rows % k == 0 and (n_rows // k) % 16 == 0)


def _ffn_in(x2d, meta_tile, g1, w1i, w1o, gm, win, gk, gq, lbl, to_cast, tm):
    n_real_tiles = x2d.shape[0] // tm
    lp = x2d.shape[0] + BLOCK
    n_blk = lp // BLOCK
    tb = tm // BLOCK
    const = lambda shape: pl.BlockSpec(shape, lambda i: (0,) * len(shape),
                                       pipeline_mode=pl.Buffered(1))
    rows = lambda w: pl.BlockSpec((tm, w), lambda i: (i, 0))
    wcols = lambda width, start: pl.BlockSpec(
        (D_MODEL, width), lambda i: (0, start // width),
        pipeline_mode=pl.Buffered(1))
    chunks = tuple(_cast_chunks(w.shape[0], n_real_tiles + 1) for w in to_cast)
    cast_specs = [
        pl.BlockSpec((w.shape[0] // k, w.shape[1]),
                     lambda i, k=k: (jnp.minimum(i, k - 1), 0))
        for w, k in zip(to_cast, chunks)]
    return pl.pallas_call(
        functools.partial(_ffn_in_kernel, cast_chunks=chunks),
        grid=(n_real_tiles + 1,),
        in_specs=[pl.BlockSpec((tm, D_MODEL),
                               lambda i: (jnp.minimum(i, n_real_tiles - 1), 0)),
                  const((BLOCK, D_MODEL)), const((1, D_MODEL)),
                  const((D_MODEL, 2 * D_FF)), const((D_FF, D_MODEL)),
                  const((1, D_MODEL)), wcols(4 * HG_W, 0),
                  wcols(SB_W, 4 * HG_W), wcols(SB_W, 4 * HG_W + SB_W),
                  wcols(SB_W, 4 * HG_W + 2 * SB_W), const((SB_W, 1)),
                  const((1, BLOCK)), const((2, HG_W))] + cast_specs,
        out_specs=[rows(D_MODEL), rows(4 * HG_W),
                   pl.BlockSpec((tb, REC_ROWS, BLOCK), lambda i: (i, 0, 0))]
        + cast_specs,
        out_shape=[jax.ShapeDtypeStruct((lp, D_MODEL), F32),
                   jax.ShapeDtypeStruct((lp, 4 * HG_W), F32),
                   jax.ShapeDtypeStruct((n_blk, REC_ROWS, BLOCK), BF16)]
        + [jax.ShapeDtypeStruct(w.shape, BF16) for w in to_cast],
        scratch_shapes=[pltpu.VMEM((tm, D_FF), BF16)],
        compiler_params=pltpu.CompilerParams(
            dimension_semantics=("arbitrary",), vmem_limit_bytes=VMEM_LIMIT),
        name="ffn_in",
    )(x2d, meta_tile, g1, w1i, w1o, gm, win, win, win, win, gk, gq, lbl, *to_cast)


def _hgrn_sum_matrix():
    t = np.arange(BLOCK)[:, None]
    j = np.arange(BLOCK)[None, :]
    mats = [(j <= t)]
    for lvl in range(1, N_FINE_LEVELS):
        c = 1 << lvl
        m = (t // (2 * c)) * (2 * c) + c
        upper = (t >= m) & (j >= m) & (j <= t)
        lower = (t < m) & (j > t) & (j <= m - 1)
        mats.append(upper | lower)
    m = np.concatenate(mats, axis=0).astype(np.float32)
    return np.concatenate([m, m], axis=1)


def _hgrn_stages(hf_ref, rec_ref, gain_ref, m_ref, o_ref, st_ref):
    heads = range(HG_HEADS)
    hs = lambda a, h: a[:, h * HG_DK:(h + 1) * HG_DK]

    q = hf_ref[:, 0:HG_W]
    k = hf_ref[:, 2 * HG_W:3 * HG_W]
    v = jnp.concatenate(
        [rec_ref[REC_HV + h * BLOCK:REC_HV + (h + 1) * BLOCK, :] for h in heads],
        axis=1)
    lf = hf_ref[:, HG_W:2 * HG_W]
    x = _dot(m_ref[...], jnp.concatenate(_split2(lf), axis=0))
    yield
    bcum = x[0:BLOCK]
    b_last = bcum[BLOCK - 1:BLOCK]
    qe = (q * jnp.exp2(bcum)).astype(BF16)
    kd = (k * jnp.exp2(b_last - bcum)).astype(BF16)
    st_decay = jnp.exp2(b_last)

    row = lax.broadcasted_iota(jnp.int32, (BLOCK, BLOCK), 0)
    col = lax.broadcasted_iota(jnp.int32, (BLOCK, BLOCK), 1)
    qb, kb = q.astype(BF16), k.astype(BF16)
    diag = row == col
    attn = [jnp.where(diag, _dot_nt(hs(qb, h), hs(kb, h)), 0.0) for h in heads]
    rowf = lax.broadcasted_iota(jnp.int32, q.shape, 0)
    rowu = lax.broadcasted_iota(jnp.int32, (BLOCK // 2, BLOCK), 0)
    colu = lax.broadcasted_iota(jnp.int32, (BLOCK // 2, BLOCK), 1)
    for lvl in range(N_LEVELS):
        if lvl % 2 == 1:
            yield
        half = 1 << lvl
        if lvl >= N_FINE_LEVELS:
            n_half = BLOCK // half
            qparts, kparts = [], []
            for b in range(n_half):
                rows = slice(b * half, (b + 1) * half)
                if b % 2 == 1:
                    edge = bcum[b * half - 1:b * half]
                    qparts.append(q[rows] * jnp.exp2(bcum[rows] - edge))
                    kparts.append(jnp.zeros((half, q.shape[1]), F32))
                else:
                    edge = bcum[(b + 1) * half - 1:(b + 1) * half]
                    kparts.append(k[rows] * jnp.exp2(edge - bcum[rows]))
            ql = jnp.concatenate(qparts, axis=0).astype(BF16)
            kl = jnp.concatenate(kparts, axis=0).astype(BF16)
            t_up = ((rowu >> lvl) << (lvl + 1)) + half + (rowu & (half - 1))
            same_up = (t_up >> (lvl + 1)) == (colu >> (lvl + 1))
            zero = jnp.zeros((half, BLOCK), F32)
            for h in heads:
                al = _dot_nt(hs(ql, h), hs(kl, h))
                if lvl + 1 < N_LEVELS:
                    al = jnp.where(same_up, al, 0.0)
                attn[h] = attn[h] + jnp.concatenate(
                    [al[(b // 2) * half:(b // 2 + 1) * half] if b % 2 else zero
                     for b in range(n_half)], axis=0)
        else:
            xl = lf if lvl == 0 else x[lvl * BLOCK:(lvl + 1) * BLOCK]
            is_q = ((rowf >> lvl) & 1) == 1
            el = jnp.exp2(jnp.where(is_q, xl, 0.0) if lvl == 0 else xl)
            ql = jnp.where(is_q, q * el, 0.0).astype(BF16)
            kl = jnp.where(is_q, 0.0, k * el).astype(BF16)
            same = (row >> (lvl + 1)) == (col >> (lvl + 1))
            for h in heads:
                al = jnp.where(same, _dot_nt(hs(ql, h), hs(kl, h)), 0.0)
                attn[h] = attn[h] + al
    yield

    o = []
    for h in heads:
        st = st_ref[h]
        oh = _dot(attn[h].astype(BF16), hs(v, h))
        oh = oh + _dot_nt(hs(qe, h), st.astype(BF16))
        st_ref[h] = st * hs(st_decay, h) + lax.dot_general(
            hs(v, h), hs(kd, h), TN_DIMS, preferred_element_type=F32)
        o.append(oh)
    yield
    o = jnp.concatenate([_rms(o[h], hs(gain_ref[...], h)) for h in heads], axis=1)
    o_ref[:, 0:HG_W] = (o * hf_ref[:, 3 * HG_W:4 * HG_W]).astype(BF16)


def _sb_sum_matrix():
    j = np.arange(BLOCK)[:, None]
    s = np.arange(BLOCK)[None, :]
    w = np.concatenate([(j >= s), np.ones((BLOCK, BLOCK), bool)],
                       axis=1).astype(np.float32)
    return np.concatenate([w, w], axis=0)


def _softplus2(z):
    return jnp.maximum(z, 0.0) + jnp.log2(1.0 + jnp.exp2(_neg_abs(z)))


BLOCKS_PER_STEP = 4


def _mixer_kernel(hf_ref, rec_blk_ref, rec_hbm, gain_ref, m_ref, w_ref, o_ref,
                  st_ref, acc_ref, crep_ref, ring, kv_scr, sem, *, n_real_blk):
    n_blk = rec_hbm.shape[0]
    s = pl.program_id(0)
    subs = range(BLOCKS_PER_STEP)
    blk_of = lambda u: jnp.where(s == 0, 0, BLOCKS_PER_STEP * (s - 1) + 1 + u)
    rec_of = lambda u: rec_blk_ref.at[u]
    kv_of = lambda u: rec_blk_ref.at[u, REC_KT:REC_HV]
    rows_of = lambda u: slice(u * BLOCK, (u + 1) * BLOCK)

    def hgrn(u):
        return _hgrn_stages(hf_ref.at[rows_of(u)], rec_of(u), gain_ref, m_ref,
                            o_ref.at[rows_of(u)], st_ref)

    row = lax.broadcasted_iota(jnp.int32, (BLOCK, BLOCK), 0)
    col = lax.broadcasted_iota(jnp.int32, (BLOCK, BLOCK), 1)
    low = col < SB_DH
    pairs = range(SB_PAIRS)
    heads = range(SB_HEADS)

    def fold_stages(u, j_top, tiles, srcs, fresh, keep=True):
        c = blk_of(u)
        rec_ref, o_u = rec_of(u), o_ref.at[rows_of(u)]
        n = len(tiles)
        nrows = [t[1] for t in tiles]
        assert all(nr == BLOCK for nr in nrows[:-1])
        masks = []
        for i, (kind, nr) in enumerate(tiles):
            assert kind is None or nr == BLOCK
            if kind == "diag":
                m = col < row
            elif kind == "general":
                kpos = (j_top - i) * BLOCK + col
                m = jnp.logical_and(kpos < c * BLOCK + row, kpos >= PAD)
            else:
                m = None
            masks.append(m)
        z = {}
        for p in pairs:
            kt = jnp.concatenate([srcs[i][_blk(p), :] for i in range(n)],
                                 axis=1)
            zz = _dot(rec_ref[REC_QN + 2 * p * BLOCK:REC_QN + 2 * (p + 1) * BLOCK, :],
                      kt)
            for a in range(2):
                for i in range(n):
                    z[2 * p + a, i] = zz[_blk(a), _blk(i)][:nrows[i]]
        yield
        order = [(hd, i) for hd in heads for i in range(n)]
        packed = []
        for hd, i in order:
            sp = _softplus2(z[hd, i])
            if masks[i] is not None:
                sp = jnp.where(masks[i], sp, 0.0)
            packed.append(jnp.concatenate(_split2(sp), axis=1))
        r = _dot(jnp.concatenate(packed, axis=0), w_ref[...])
        offs = np.cumsum([0] + [nrows[i] for _, i in order])
        yield
        w = {}
        cmax = None
        for hd in heads:
            crep = None if fresh else crep_ref[u, hd]
            for i in range(n):
                nr = nrows[i]
                o0 = int(offs[hd * n + i])
                rh = r[o0:o0 + nr]
                arg = z[hd, i] - rh[:, :BLOCK]
                wh = jnp.exp2(arg if crep is None else arg + crep[:nr])
                if masks[i] is not None:
                    wh = jnp.where(masks[i], wh, 0.0)
                w[hd, i] = wh.astype(BF16)
                if crep is None:
                    crep = -rh[:, BLOCK:]
                elif nr == BLOCK:
                    crep = crep - rh[:, BLOCK:]
                else:
                    crep = jnp.concatenate(
                        [crep[:nr] - rh[:, BLOCK:], crep[nr:]], axis=0)
            if keep:
                crep_ref[u, hd] = crep
            cmax = crep if cmax is None else jnp.maximum(cmax, crep)
        yield
        full = [i for i in range(n) if nrows[i] == BLOCK]
        for p in pairs:
            wp = jnp.concatenate(
                [jnp.concatenate([w[2 * p + a, i] for i in full], axis=1)
                 for a in range(2)], axis=0)
            vtile = lambda i: srcs[i][SB_W + p * BLOCK:SB_W + (p + 1) * BLOCK, :]
            pv = _dot(wp, jnp.concatenate([vtile(i) for i in full], axis=0))
            pv = jnp.where(low, pv[:BLOCK], pv[BLOCK:])
            if nrows[-1] < BLOCK:
                nr = nrows[-1]
                ps = _dot(jnp.concatenate([w[2 * p, n - 1], w[2 * p + 1, n - 1]],
                                          axis=0), vtile(n - 1))
                low_nr = lax.broadcasted_iota(jnp.int32, (nr, BLOCK), 1) < SB_DH
                ps = jnp.where(low_nr, ps[:nr], ps[nr:])
                pv = jnp.concatenate([pv[:nr] + ps, pv[nr:]], axis=0)
            if fresh:
                o_u[:, HG_W + p * BLOCK:HG_W + (p + 1) * BLOCK] = pv.astype(BF16)
                if keep:
                    acc_ref[u, p] = pv
            else:
                acc_ref[u, p] = acc_ref[u, p] + pv
        return jnp.max(cmax)

    n_fast = 3
    fast = s >= 2
    no_more = (jnp.float32(-jnp.inf),)

    fast_tiles = (("diag", BLOCK), (None, BLOCK), (None, FAR_ROWS))

    def fast_step():
        tiles = fast_tiles
        back = lambda u, i: kv_of(u - i) if u >= i else ring.at[u - i + 2]
        def delayed(gen, turns):
            for _ in range(turns):
                yield
            return (yield from gen)

        res = _interleave(
            *[delayed(fold_stages(u, blk_of(u), tiles,
                                  [back(u, i) for i in range(3)], True, keep=False),
                      4 * u)
              for u in subs], *[delayed(hgrn(u), 4 * u + 2) for u in subs])
        for i in range(2):
            ring[i] = kv_of(BLOCKS_PER_STEP - 2 + i)[...]
        return tuple(res[:BLOCKS_PER_STEP])

    def first_steps():
        @pl.when(s == 0)
        def _():
            st_ref[...] = jnp.zeros_like(st_ref)

        def start(u):
            if u >= n_fast - 1:
                tiles, srcs = fast_tiles, [kv_of(u - i) for i in range(3)]
            else:
                tiles, srcs = (("general", BLOCK),), [kv_of(u)]
            return lambda: (_interleave(
                fold_stages(u, blk_of(u), tiles, srcs, True,
                            keep=u < n_fast - 1), hgrn(u))[0],)
        res = start(0)()
        for u in subs[1:]:
            res = res + lax.cond(s >= 1, start(u), lambda: no_more)

        @pl.when(s == 0)
        def _():
            ring[1] = kv_of(0)[...]

        @pl.when(s > 0)
        def _():
            for i in range(2):
                ring[i] = kv_of(BLOCKS_PER_STEP - 2 + i)[...]
        return res

    first = lax.cond(fast, fast_step, first_steps)

    def fold(u, j_top, kind, fresh=False):
        pj = lax.rem(j_top + n_real_blk, n_blk)
        cp = pltpu.make_async_copy(rec_hbm.at[pj, REC_KT:REC_HV], kv_scr, sem.at[0])
        cp.start()
        cp.wait()
        return _interleave(fold_stages(u, j_top, ((kind, BLOCK),), [kv_scr],
                                       fresh))[0]

    def walk_back(u):
        cmax0 = first[u]
        c = blk_of(u)

        @pl.when(cmax0 > EXP2_ZERO_BELOW)
        def _():
            cmax1 = lax.cond(c >= n_fast,
                             lambda: fold(u, c, "general", fresh=True),
                             lambda: cmax0)

            def cond(carry):
                j, cmax = carry
                return jnp.logical_and(j >= 1, cmax > EXP2_ZERO_BELOW)

            def body(carry):
                j, _ = carry
                return j - 1, fold(u, j, None)

            j_end, cmax_end = lax.while_loop(cond, body, (c - 1, cmax1))

            @pl.when(jnp.logical_and(j_end == 0, cmax_end > EXP2_ZERO_BELOW))
            def _():
                fold(u, 0, "general")

            for p in pairs:
                o_ref[rows_of(u), HG_W + p * BLOCK:HG_W + (p + 1) * BLOCK] = (
                    acc_ref[u, p].astype(BF16))

    @pl.when(functools.reduce(jnp.maximum, first) > EXP2_ZERO_BELOW)
    def _():
        for u in subs:
            walk_back(u)


def _mixer(hf, rec, out_gain, n_real_blk):
    n_blk = rec.shape[0]
    lp = n_blk * BLOCK
    assert n_real_blk % BLOCKS_PER_STEP == 0 and n_blk == n_real_blk + 1
    n_steps = n_real_blk // BLOCKS_PER_STEP + 1
    tile = lambda s: (s + n_steps - 1) % n_steps
    rows = BLOCKS_PER_STEP * BLOCK
    whole = lambda a: pl.BlockSpec(a.shape, lambda s: (0,) * a.ndim)
    msum = jnp.asarray(_hgrn_sum_matrix(), dtype=BF16)
    wsum = jnp.asarray(_sb_sum_matrix(), dtype=BF16)
    return pl.pallas_call(
        functools.partial(_mixer_kernel, n_real_blk=n_real_blk),
        grid=(n_steps,),
        in_specs=[pl.BlockSpec((rows, 4 * HG_W), lambda s: (tile(s), 0)),
                  pl.BlockSpec((BLOCKS_PER_STEP, REC_ROWS, BLOCK),
                               lambda s: (tile(s), 0, 0)),
                  pl.BlockSpec(memory_space=pl.ANY),
                  whole(out_gain), whole(msum), whole(wsum)],
        out_specs=pl.BlockSpec((rows, HG_W + SB_W), lambda s: (tile(s), 0)),
        out_shape=jax.ShapeDtypeStruct((lp, HG_W + SB_W), BF16),
        scratch_shapes=[pltpu.VMEM((HG_HEADS, HG_DV, HG_DK), F32),
                        pltpu.VMEM((BLOCKS_PER_STEP, SB_PAIRS, BLOCK, BLOCK), F32),
                        pltpu.VMEM((BLOCKS_PER_STEP, SB_HEADS, BLOCK, BLOCK), F32),
                        pltpu.VMEM((2, KV_ROWS, BLOCK), BF16),
                        pltpu.VMEM((KV_ROWS, BLOCK), BF16),
                        pltpu.SemaphoreType.DMA((1,))],
        compiler_params=pltpu.CompilerParams(
            dimension_semantics=("arbitrary",)),
        name="mixer",
    )(hf, rec, rec, out_gain, msum, wsum)


def _ffn_out_kernel(h1_ref, o_ref, wo_ref, g2_ref, w2i_ref, w2o_ref, out_ref,
                    act_ref):
    h2 = h1_ref[...] + _dot(o_ref[...], wo_ref[...])
    xn = _rms(h2, g2_ref[...]).astype(BF16)
    out_ref[...] = h2 + 0.5 * _swiglu(xn, w2i_ref, w2o_ref, act_ref)


def _ffn_out(h1, o, wo, g2, w2i, w2o, n_rows, tm):
    const = lambda shape: pl.BlockSpec(shape, lambda i: (0,) * len(shape),
                                       pipeline_mode=pl.Buffered(1))
    rows = lambda w: pl.BlockSpec((tm, w), lambda i: (i, 0))
    return pl.pallas_call(
        _ffn_out_kernel,
        grid=(n_rows // tm,),
        in_specs=[rows(D_MODEL), rows(HG_W + SB_W),
                  const((HG_W + SB_W, D_MODEL)),
                  const((1, D_MODEL)), const((D_MODEL, 2 * D_FF)),
                  const((D_FF, D_MODEL))],
        out_specs=rows(D_MODEL),
        out_shape=jax.ShapeDtypeStruct((n_rows, D_MODEL), F32),
        scratch_shapes=[pltpu.VMEM((tm, D_FF), BF16)],
        compiler_params=pltpu.CompilerParams(
            dimension_semantics=("arbitrary",), vmem_limit_bytes=VMEM_LIMIT),
        name="ffn_out",
    )(h1, o, wo, g2, w2i, w2o)


def kernel(x, meta_tokens, ffn1_norm, ffn1_w_in, ffn1_w_out, mix_norm, w_in,
           hgrn_lb_logits, hgrn_out_norm, sb_q_norm, sb_k_norm, w_out,
           ffn2_norm, ffn2_w_in, ffn2_w_out):
    b, seq, _ = x.shape
    assert b == 1 and seq % BLOCK == 0
    assert ffn1_norm.shape[0] == 1, "single layer"
    n_real_blk = seq // BLOCK
    tm = 512
    assert seq % tm == 0

    meta_tile = jnp.zeros((BLOCK, D_MODEL), x.dtype).at[PAD:].set(
        meta_tokens.astype(x.dtype))

    gk = jnp.tile(sb_k_norm[0], SB_HEADS).reshape(SB_W, 1)
    gq = jnp.tile(sb_q_norm[0], 2).reshape(1, BLOCK)

    h1, hf, rec, wo, w2i, w2o = _ffn_in(
        x[0], meta_tile, ffn1_norm, ffn1_w_in[0].astype(BF16),
        ffn1_w_out[0].astype(BF16), mix_norm, w_in[0].astype(BF16), gk, gq,
        hgrn_lb_logits, (w_out[0], ffn2_w_in[0], ffn2_w_out[0]), tm)
    o = _mixer(hf, rec, hgrn_out_norm, n_real_blk)
    out = _ffn_out(h1, o, wo, ffn2_norm, w2i, w2o, seq, 2 * tm)
    return out[None]
```

```python
import functools

import numpy as np
import jax
import jax.numpy as jnp
from jax import lax
from jax.experimental import pallas as pl
from jax.experimental.pallas import tpu as pltpu

F32 = jnp.float32
BF16 = jnp.bfloat16

D_MODEL = 1024
N_META = 16
BLOCK = 128
PAD = (-N_META) % BLOCK
HG_HEADS = 4
HG_DK = 128
HG_DV = 128
HG_W = HG_HEADS * HG_DK
SB_HEADS = 8
SB_DH = 64
SB_W = SB_HEADS * SB_DH
SB_PAIRS = SB_HEADS // 2
D_FF = 2816
RMS_EPS = 1e-6
FF_CHUNK = 256
N_LEVELS = 7
N_FINE_LEVELS = 3
LOG2E = 1.4426950408889634
EXP2_ZERO_BELOW = -150.0
FAR_ROWS = 48
REC_QN = 0
REC_KT = REC_QN + SB_HEADS * BLOCK
REC_V = REC_KT + SB_W
REC_HV = REC_V + SB_PAIRS * BLOCK
REC_ROWS = REC_HV + HG_HEADS * BLOCK
KV_ROWS = REC_HV - REC_KT
VMEM_LIMIT = 56 * 1024 * 1024
NT_DIMS = (((1,), (1,)), ((), ()))
TN_DIMS = (((0,), (0,)), ((), ()))


def _dot(a, b):
    return jnp.dot(a, b, preferred_element_type=F32)


def _dot_nt(a, b):
    return lax.dot_general(a, b, NT_DIMS, preferred_element_type=F32)


def _rms(x, gain):
    ms = jnp.mean(x * x, axis=-1, keepdims=True)
    return x * lax.rsqrt(ms + RMS_EPS) * gain


def _split2(x):
    hi = x.astype(BF16)
    lo = (x - hi.astype(F32)).astype(BF16)
    return hi, lo


def _neg_abs(x):
    return -jnp.abs(x)


def _silu(x):
    return x * jax.nn.sigmoid(x)


def _blk(i):
    return slice(i * BLOCK, (i + 1) * BLOCK)


def _swiglu_stages(xn, w_in_ref, w_out_ref, act_ref):
    for c in range(D_FF // FF_CHUNK):
        lo, hi = c * FF_CHUNK, (c + 1) * FF_CHUNK
        g = _dot(xn, w_in_ref[:, lo:hi])
        u = _dot(xn, w_in_ref[:, D_FF + lo:D_FF + hi])
        act_ref[:, lo:hi] = (_silu(g) * u).astype(BF16)
        yield
    return _dot(act_ref[...], w_out_ref[...])


def _interleave(*stage_lists):
    results = [None] * len(stage_lists)
    active = dict(enumerate(stage_lists))
    while active:
        for i, g in list(active.items()):
            try:
                next(g)
            except StopIteration as stop:
                results[i] = stop.value
                del active[i]
    return results


def _swiglu(xn, w_in_ref, w_out_ref, act_ref):
    return _interleave(_swiglu_stages(xn, w_in_ref, w_out_ref, act_ref))[0]


def _ffn_in_tile(h, pads, g1_ref, w1i_ref, w1o_ref, gm_ref, whg_ref, wq_ref,
                 wk_ref, wv_ref, gk_ref, gq_ref, lbl_ref,
                 h1_ref, hf_ref, rec_ref, act_ref):
    n = h.shape[0]
    rows = slice(0, n)
    xn = _rms(h, g1_ref[...]).astype(BF16)
    h1 = h + 0.5 * _swiglu(xn, w1i_ref, w1o_ref, act_ref.at[rows])
    h1_ref[rows] = h1
    xm = _rms(h1, gm_ref[...]).astype(BF16)

    def put(base, piece, val):
        for t in range(n // BLOCK):
            lo = base + piece * BLOCK
            rec_ref[t, lo:lo + BLOCK, :] = val[_blk(t)]

    q = _dot(xm, wq_ref[...])
    low = lax.broadcasted_iota(jnp.int32, (n, BLOCK), 1) < SB_DH
    qscale = gq_ref[...] * (LOG2E / np.sqrt(np.float32(SB_DH)))
    for p in range(SB_PAIRS):
        qp = q[:, _blk(p)]
        for a in range(2):
            own = low if a == 0 else jnp.logical_not(low)
            ms = jnp.sum(jnp.where(own, qp * qp, 0.0), axis=-1,
                         keepdims=True) * (1.0 / SB_DH)
            qn = jnp.where(own, qp * lax.rsqrt(ms + RMS_EPS) * qscale, 0.0)
            put(REC_QN, 2 * p + a, qn.astype(BF16))
    v = _dot(xm, wv_ref[...]).astype(BF16)
    for p in range(SB_PAIRS):
        put(REC_V, p, v[:, _blk(p)])
    kt = lax.dot_general(wk_ref[...], xm, (((0,), (1,)), ((), ())),
                         preferred_element_type=F32)
    k3 = kt.reshape(SB_HEADS, SB_DH, n)
    ms = jnp.mean(k3 * k3, axis=1, keepdims=True)
    kn = (k3 * lax.rsqrt(ms + RMS_EPS)).reshape(SB_W, n) * gk_ref[...]
    kn = kn.astype(BF16)
    for t in range(n // BLOCK):
        rec_ref[t, REC_KT:REC_KT + SB_W, :] = kn[:, _blk(t)]

    part = lambda i: _dot(xm, whg_ref[:, i * HG_W:(i + 1) * HG_W])
    hv = part(2).astype(BF16)
    for hd in range(HG_HEADS):
        put(REC_HV, hd, hv[:, _blk(hd)])
    hf_ref[rows, 0:HG_W] = _silu(part(0))
    hf_ref[rows, 3 * HG_W:4 * HG_W] = _silu(part(3))
    lg = lbl_ref[...]
    e = jnp.exp(lg - jnp.max(lg, axis=0, keepdims=True))
    lb = e[0:1] / jnp.sum(e, axis=0, keepdims=True)
    z = part(1)
    ez = jnp.exp(_neg_abs(z))
    rz = 1.0 / (1.0 + ez)
    erz = ez * rz
    pos = z >= 0.0
    lf = jnp.log2(lb + (1.0 - lb) * jnp.where(pos, rz, erz))
    k = (1.0 - lb) * jnp.where(pos, erz, rz)
    if pads:
        valid = lax.broadcasted_iota(jnp.int32, z.shape, 0) >= PAD
        lf, k = jnp.where(valid, lf, 0.0), jnp.where(valid, k, 0.0)
    hf_ref[rows, HG_W:2 * HG_W] = lf
    hf_ref[rows, 2 * HG_W:3 * HG_W] = k


N_FFN_IN_PARAMS = 11


def _ffn_in_kernel(x_ref, meta_ref, *refs, cast_chunks):
    nc = len(cast_chunks)
    params = refs[:N_FFN_IN_PARAMS]
    cast_in = refs[N_FFN_IN_PARAMS:N_FFN_IN_PARAMS + nc]
    outs = refs[N_FFN_IN_PARAMS + nc:N_FFN_IN_PARAMS + nc + 3]
    cast_out = refs[N_FFN_IN_PARAMS + nc + 3:N_FFN_IN_PARAMS + 2 * nc + 3]
    act_ref = refs[-1]
    i = pl.program_id(0)
    is_meta = i == pl.num_programs(0) - 1

    for src, dst, chunks in zip(cast_in, cast_out, cast_chunks):
        @pl.when(i < chunks)
        def _():
            dst[...] = src[...].astype(BF16)

    @pl.when(jnp.logical_not(is_meta))
    def _():
        _ffn_in_tile(x_ref[...], False, *params, *outs, act_ref)

    @pl.when(is_meta)
    def _():
        _ffn_in_tile(meta_ref[...], True, *params, *outs, act_ref)


def _cast_chunks(n_rows, n_steps):
    return max(k for k in range(1, n_steps + 1)
               if n_rows % k == 0 and (n_rows // k) % 16 == 0)


def _ffn_in(x2d, meta_tile, g1, w1i, w1o, gm, win, gk, gq, lbl, to_cast, tm):
    n_real_tiles = x2d.shape[0] // tm
    lp = x2d.shape[0] + BLOCK
    n_blk = lp // BLOCK
    tb = tm // BLOCK
    const = lambda shape: pl.BlockSpec(shape, lambda i: (0,) * len(shape),
                                       pipeline_mode=pl.Buffered(1))
    rows = lambda w: pl.BlockSpec((tm, w), lambda i: (i, 0))
    wcols = lambda width, start: pl.BlockSpec(
        (D_MODEL, width), lambda i: (0, start // width),
        pipeline_mode=pl.Buffered(1))
    chunks = tuple(_cast_chunks(w.shape[0], n_real_tiles + 1) for w in to_cast)
    cast_specs = [
        pl.BlockSpec((w.shape[0] // k, w.shape[1]),
                     lambda i, k=k: (jnp.minimum(i, k - 1), 0))
        for w, k in zip(to_cast, chunks)]
    return pl.pallas_call(
        functools.partial(_ffn_in_kernel, cast_chunks=chunks),
        grid=(n_real_tiles + 1,),
        in_specs=[pl.BlockSpec((tm, D_MODEL),
                               lambda i: (jnp.minimum(i, n_real_tiles - 1), 0)),
                  const((BLOCK, D_MODEL)), const((1, D_MODEL)),
                  const((D_MODEL, 2 * D_FF)), const((D_FF, D_MODEL)),
                  const((1, D_MODEL)), wcols(4 * HG_W, 0),
                  wcols(SB_W, 4 * HG_W), wcols(SB_W, 4 * HG_W + SB_W),
                  wcols(SB_W, 4 * HG_W + 2 * SB_W), const((SB_W, 1)),
                  const((1, BLOCK)), const((2, HG_W))] + cast_specs,
        out_specs=[rows(D_MODEL), rows(4 * HG_W),
                   pl.BlockSpec((tb, REC_ROWS, BLOCK), lambda i: (i, 0, 0))]
        + cast_specs,
        out_shape=[jax.ShapeDtypeStruct((lp, D_MODEL), F32),
                   jax.ShapeDtypeStruct((lp, 4 * HG_W), F32),
                   jax.ShapeDtypeStruct((n_blk, REC_ROWS, BLOCK), BF16)]
        + [jax.ShapeDtypeStruct(w.shape, BF16) for w in to_cast],
        scratch_shapes=[pltpu.VMEM((tm, D_FF), BF16)],
        compiler_params=pltpu.CompilerParams(
            dimension_semantics=("arbitrary",), vmem_limit_bytes=VMEM_LIMIT),
        name="ffn_in",
    )(x2d, meta_tile, g1, w1i, w1o, gm, win, win, win, win, gk, gq, lbl, *to_cast)


def _hgrn_sum_matrix():
    t = np.arange(BLOCK)[:, None]
    j = np.arange(BLOCK)[None, :]
    mats = [(j <= t)]
    for lvl in range(1, N_FINE_LEVELS):
        c = 1 << lvl
        m = (t // (2 * c)) * (2 * c) + c
        upper = (t >= m) & (j >= m) & (j <= t)
        lower = (t < m) & (j > t) & (j <= m - 1)
        mats.append(upper | lower)
    m = np.concatenate(mats, axis=0).astype(np.float32)
    return np.concatenate([m, m], axis=1)


def _hgrn_stages(hf_ref, rec_ref, gain_ref, m_ref, o_ref, st_ref):
    heads = range(HG_HEADS)
    hs = lambda a, h: a[:, h * HG_DK:(h + 1) * HG_DK]

    q = hf_ref[:, 0:HG_W]
    k = hf_ref[:, 2 * HG_W:3 * HG_W]
    v = jnp.concatenate(
        [rec_ref[REC_HV + h * BLOCK:REC_HV + (h + 1) * BLOCK, :] for h in heads],
        axis=1)
    lf = hf_ref[:, HG_W:2 * HG_W]
    x = _dot(m_ref[...], jnp.concatenate(_split2(lf), axis=0))
    yield
    bcum = x[0:BLOCK]
    b_last = bcum[BLOCK - 1:BLOCK]
    qe = (q * jnp.exp2(bcum)).astype(BF16)
    kd = (k * jnp.exp2(b_last - bcum)).astype(BF16)
    st_decay = jnp.exp2(b_last)

    row = lax.broadcasted_iota(jnp.int32, (BLOCK, BLOCK), 0)
    col = lax.broadcasted_iota(jnp.int32, (BLOCK, BLOCK), 1)
    qb, kb = q.astype(BF16), k.astype(BF16)
    diag = row == col
    attn = [jnp.where(diag, _dot_nt(hs(qb, h), hs(kb, h)), 0.0) for h in heads]
    rowf = lax.broadcasted_iota(jnp.int32, q.shape, 0)
    rowu = lax.broadcasted_iota(jnp.int32, (BLOCK // 2, BLOCK), 0)
    colu = lax.broadcasted_iota(jnp.int32, (BLOCK // 2, BLOCK), 1)
    for lvl in range(N_LEVELS):
        if lvl % 2 == 1:
            yield
        half = 1 << lvl
        if lvl >= N_FINE_LEVELS:
            n_half = BLOCK // half
            qparts, kparts = [], []
            for b in range(n_half):
                rows = slice(b * half, (b + 1) * half)
                if b % 2 == 1:
                    edge = bcum[b * half - 1:b * half]
                    qparts.append(q[rows] * jnp.exp2(bcum[rows] - edge))
                    kparts.append(jnp.zeros((half, q.shape[1]), F32))
                else:
                    edge = bcum[(b + 1) * half - 1:(b + 1) * half]
                    kparts.append(k[rows] * jnp.exp2(edge - bcum[rows]))
            ql = jnp.concatenate(qparts, axis=0).astype(BF16)
            kl = jnp.concatenate(kparts, axis=0).astype(BF16)
            t_up = ((rowu >> lvl) << (lvl + 1)) + half + (rowu & (half - 1))
            same_up = (t_up >> (lvl + 1)) == (colu >> (lvl + 1))
            zero = jnp.zeros((half, BLOCK), F32)
            for h in heads:
                al = _dot_nt(hs(ql, h), hs(kl, h))
                if lvl + 1 < N_LEVELS:
                    al = jnp.where(same_up, al, 0.0)
                attn[h] = attn[h] + jnp.concatenate(
                    [al[(b // 2) * half:(b // 2 + 1) * half] if b % 2 else zero
                     for b in range(n_half)], axis=0)
        else:
            xl = lf if lvl == 0 else x[lvl * BLOCK:(lvl + 1) * BLOCK]
            is_q = ((rowf >> lvl) & 1) == 1
            el = jnp.exp2(jnp.where(is_q, xl, 0.0) if lvl == 0 else xl)
            ql = jnp.where(is_q, q * el, 0.0).astype(BF16)
            kl = jnp.where(is_q, 0.0, k * el).astype(BF16)
            same = (row >> (lvl + 1)) == (col >> (lvl + 1))
            for h in heads:
                al = jnp.where(same, _dot_nt(hs(ql, h), hs(kl, h)), 0.0)
                attn[h] = attn[h] + al
    yield

    o = []
    for h in heads:
        st = st_ref[h]
        oh = _dot(attn[h].astype(BF16), hs(v, h))
        oh = oh + _dot_nt(hs(qe, h), st.astype(BF16))
        st_ref[h] = st * hs(st_decay, h) + lax.dot_general(
            hs(v, h), hs(kd, h), TN_DIMS, preferred_element_type=F32)
        o.append(oh)
    yield
    o = jnp.concatenate([_rms(o[h], hs(gain_ref[...], h)) for h in heads], axis=1)
    o_ref[:, 0:HG_W] = (o * hf_ref[:, 3 * HG_W:4 * HG_W]).astype(BF16)


def _sb_sum_matrix():
    j = np.arange(BLOCK)[:, None]
    s = np.arange(BLOCK)[None, :]
    w = np.concatenate([(j >= s), np.ones((BLOCK, BLOCK), bool)],
                       axis=1).astype(np.float32)
    return np.concatenate([w, w], axis=0)


def _softplus2(z):
    return jnp.maximum(z, 0.0) + jnp.log2(1.0 + jnp.exp2(_neg_abs(z)))


BLOCKS_PER_STEP = 4


def _mixer_kernel(hf_ref, rec_blk_ref, rec_hbm, gain_ref, m_ref, w_ref, o_ref,
                  st_ref, acc_ref, crep_ref, ring, kv_scr, sem, *, n_real_blk):
    n_blk = rec_hbm.shape[0]
    s = pl.program_id(0)
    subs = range(BLOCKS_PER_STEP)
    blk_of = lambda u: jnp.where(s == 0, 0, BLOCKS_PER_STEP * (s - 1) + 1 + u)
    rec_of = lambda u: rec_blk_ref.at[u]
    kv_of = lambda u: rec_blk_ref.at[u, REC_KT:REC_HV]
    rows_of = lambda u: slice(u * BLOCK, (u + 1) * BLOCK)

    def hgrn(u):
        return _hgrn_stages(hf_ref.at[rows_of(u)], rec_of(u), gain_ref, m_ref,
                            o_ref.at[rows_of(u)], st_ref)

    row = lax.broadcasted_iota(jnp.int32, (BLOCK, BLOCK), 0)
    col = lax.broadcasted_iota(jnp.int32, (BLOCK, BLOCK), 1)
    low = col < SB_DH
    pairs = range(SB_PAIRS)
    heads = range(SB_HEADS)

    def fold_stages(u, j_top, tiles, srcs, fresh, keep=True):
        c = blk_of(u)
        rec_ref, o_u = rec_of(u), o_ref.at[rows_of(u)]
        n = len(tiles)
        nrows = [t[1] for t in tiles]
        assert all(nr == BLOCK for nr in nrows[:-1])
        masks = []
        for i, (kind, nr) in enumerate(tiles):
            assert kind is None or nr == BLOCK
            if kind == "diag":
                m = col < row
            elif kind == "general":
                kpos = (j_top - i) * BLOCK + col
                m = jnp.logical_and(kpos < c * BLOCK + row, kpos >= PAD)
            else:
                m = None
            masks.append(m)
        z = {}
        for p in pairs:
            kt = jnp.concatenate([srcs[i][_blk(p), :] for i in range(n)],
                                 axis=1)
            zz = _dot(rec_ref[REC_QN + 2 * p * BLOCK:REC_QN + 2 * (p + 1) * BLOCK, :],
                      kt)
            for a in range(2):
                for i in range(n):
                    z[2 * p + a, i] = zz[_blk(a), _blk(i)][:nrows[i]]
        yield
        r = {}
        for i in range(n):
            packed = []
            for hd in heads:
                sp = _softplus2(z[hd, i])
                if masks[i] is not None:
                    sp = jnp.where(masks[i], sp, 0.0)
                packed.append(jnp.concatenate(_split2(sp), axis=1))
            r[i] = _dot(jnp.concatenate(packed, axis=0), w_ref[...])
            yield
        w = {}
        cmax = None
        for hd in heads:
            crep = None if fresh else crep_ref[u, hd]
            for i in range(n):
                nr = nrows[i]
                rh = r[i][hd * nr:(hd + 1) * nr]
                arg = z[hd, i] - rh[:, :BLOCK]
                wh = jnp.exp2(arg if crep is None else arg + crep[:nr])
                if masks[i] is not None:
                    wh = jnp.where(masks[i], wh, 0.0)
                w[hd, i] = wh.astype(BF16)
                if crep is None:
                    crep = -rh[:, BLOCK:]
                elif nr == BLOCK:
                    crep = crep - rh[:, BLOCK:]
                else:
                    crep = jnp.concatenate(
                        [crep[:nr] - rh[:, BLOCK:], crep[nr:]], axis=0)
            if keep:
                crep_ref[u, hd] = crep
            cmax = crep if cmax is None else jnp.maximum(cmax, crep)
        yield
        full = [i for i in range(n) if nrows[i] == BLOCK]
        for p in pairs:
            wp = jnp.concatenate(
                [jnp.concatenate([w[2 * p + a, i] for i in full], axis=1)
                 for a in range(2)], axis=0)
            vtile = lambda i: srcs[i][SB_W + p * BLOCK:SB_W + (p + 1) * BLOCK, :]
            pv = _dot(wp, jnp.concatenate([vtile(i) for i in full], axis=0))
            pv = jnp.where(low, pv[:BLOCK], pv[BLOCK:])
            if nrows[-1] < BLOCK:
                nr = nrows[-1]
                ps = _dot(jnp.concatenate([w[2 * p, n - 1], w[2 * p + 1, n - 1]],
                                          axis=0), vtile(n - 1))
                low_nr = lax.broadcasted_iota(jnp.int32, (nr, BLOCK), 1) < SB_DH
                ps = jnp.where(low_nr, ps[:nr], ps[nr:])
                pv = jnp.concatenate([pv[:nr] + ps, pv[nr:]], axis=0)
            if fresh:
                o_u[:, HG_W + p * BLOCK:HG_W + (p + 1) * BLOCK] = pv.astype(BF16)
                if keep:
                    acc_ref[u, p] = pv
            else:
                acc_ref[u, p] = acc_ref[u, p] + pv
        return jnp.max(cmax)

    n_fast = 3
    fast = s >= 2
    no_more = (jnp.float32(-jnp.inf),)

    fast_tiles = (("diag", BLOCK), (None, BLOCK), (None, FAR_ROWS))

    def fast_step():
        tiles = fast_tiles
        back = lambda u, i: kv_of(u - i) if u >= i else ring.at[u - i + 2]
        def delayed(gen, turns):
            for _ in range(turns):
                yield
            return (yield from gen)

        res = _interleave(
            *[delayed(fold_stages(u, blk_of(u), tiles,
                                  [back(u, i) for i in range(3)], True, keep=False),
                      4 * u)
              for u in subs], *[delayed(hgrn(u), 4 * u + 2) for u in subs])
        for i in range(2):
            ring[i] = kv_of(BLOCKS_PER_STEP - 2 + i)[...]
        return tuple(res[:BLOCKS_PER_STEP])

    def first_steps():
        @pl.when(s == 0)
        def _():
            st_ref[...] = jnp.zeros_like(st_ref)

        def start(u):
            if u >= n_fast - 1:
                tiles, srcs = fast_tiles, [kv_of(u - i) for i in range(3)]
            else:
                tiles, srcs = (("general", BLOCK),), [kv_of(u)]
            return lambda: (_interleave(
                fold_stages(u, blk_of(u), tiles, srcs, True,
                            keep=u < n_fast - 1), hgrn(u))[0],)
        res = start(0)()
        for u in subs[1:]:
            res = res + lax.cond(s >= 1, start(u), lambda: no_more)

        @pl.when(s == 0)
        def _():
            ring[1] = kv_of(0)[...]

        @pl.when(s > 0)
        def _():
            for i in range(2):
                ring[i] = kv_of(BLOCKS_PER_STEP - 2 + i)[...]
        return res

    first = lax.cond(fast, fast_step, first_steps)

    def fold(u, j_top, kind, fresh=False):
        pj = lax.rem(j_top + n_real_blk, n_blk)
        cp = pltpu.make_async_copy(rec_hbm.at[pj, REC_KT:REC_HV], kv_scr, sem.at[0])
        cp.start()
        cp.wait()
        return _interleave(fold_stages(u, j_top, ((kind, BLOCK),), [kv_scr],
                                       fresh))[0]

    def walk_back(u):
        cmax0 = first[u]
        c = blk_of(u)

        @pl.when(cmax0 > EXP2_ZERO_BELOW)
        def _():
            cmax1 = lax.cond(c >= n_fast,
                             lambda: fold(u, c, "general", fresh=True),
                             lambda: cmax0)

            def cond(carry):
                j, cmax = carry
                return jnp.logical_and(j >= 1, cmax > EXP2_ZERO_BELOW)

            def body(carry):
                j, _ = carry
                return j - 1, fold(u, j, None)

            j_end, cmax_end = lax.while_loop(cond, body, (c - 1, cmax1))

            @pl.when(jnp.logical_and(j_end == 0, cmax_end > EXP2_ZERO_BELOW))
            def _():
                fold(u, 0, "general")

            for p in pairs:
                o_ref[rows_of(u), HG_W + p * BLOCK:HG_W + (p + 1) * BLOCK] = (
                    acc_ref[u, p].astype(BF16))

    @pl.when(functools.reduce(jnp.maximum, first) > EXP2_ZERO_BELOW)
    def _():
        for u in subs:
            walk_back(u)


def _mixer(hf, rec, out_gain, n_real_blk):
    n_blk = rec.shape[0]
    lp = n_blk * BLOCK
    assert n_real_blk % BLOCKS_PER_STEP == 0 and n_blk == n_real_blk + 1
    n_steps = n_real_blk // BLOCKS_PER_STEP + 1
    tile = lambda s: (s + n_steps - 1) % n_steps
    rows = BLOCKS_PER_STEP * BLOCK
    whole = lambda a: pl.BlockSpec(a.shape, lambda s: (0,) * a.ndim)
    msum = jnp.asarray(_hgrn_sum_matrix(), dtype=BF16)
    wsum = jnp.asarray(_sb_sum_matrix(), dtype=BF16)
    return pl.pallas_call(
        functools.partial(_mixer_kernel, n_real_blk=n_real_blk),
        grid=(n_steps,),
        in_specs=[pl.BlockSpec((rows, 4 * HG_W), lambda s: (tile(s), 0)),
                  pl.BlockSpec((BLOCKS_PER_STEP, REC_ROWS, BLOCK),
                               lambda s: (tile(s), 0, 0)),
                  pl.BlockSpec(memory_space=pl.ANY),
                  whole(out_gain), whole(msum), whole(wsum)],
        out_specs=pl.BlockSpec((rows, HG_W + SB_W), lambda s: (tile(s), 0)),
        out_shape=jax.ShapeDtypeStruct((lp, HG_W + SB_W), BF16),
        scratch_shapes=[pltpu.VMEM((HG_HEADS, HG_DV, HG_DK), F32),
                        pltpu.VMEM((BLOCKS_PER_STEP, SB_PAIRS, BLOCK, BLOCK), F32),
                        pltpu.VMEM((BLOCKS_PER_STEP, SB_HEADS, BLOCK, BLOCK), F32),
                        pltpu.VMEM((2, KV_ROWS, BLOCK), BF16),
                        pltpu.VMEM((KV_ROWS, BLOCK), BF16),
                        pltpu.SemaphoreType.DMA((1,))],
        compiler_params=pltpu.CompilerParams(
            dimension_semantics=("arbitrary",)),
        name="mixer",
    )(hf, rec, rec, out_gain, msum, wsum)


def _ffn_out_kernel(h1_ref, o_ref, wo_ref, g2_ref, w2i_ref, w2o_ref, out_ref,
                    act_ref):
    h2 = h1_ref[...] + _dot(o_ref[...], wo_ref[...])
    xn = _rms(h2, g2_ref[...]).astype(BF16)
    out_ref[...] = h2 + 0.5 * _swiglu(xn, w2i_ref, w2o_ref, act_ref)


def _ffn_out(h1, o, wo, g2, w2i, w2o, n_rows, tm):
    const = lambda shape: pl.BlockSpec(shape, lambda i: (0,) * len(shape),
                                       pipeline_mode=pl.Buffered(1))
    rows = lambda w: pl.BlockSpec((tm, w), lambda i: (i, 0))
    return pl.pallas_call(
        _ffn_out_kernel,
        grid=(n_rows // tm,),
        in_specs=[rows(D_MODEL), rows(HG_W + SB_W),
                  const((HG_W + SB_W, D_MODEL)),
                  const((1, D_MODEL)), const((D_MODEL, 2 * D_FF)),
                  const((D_FF, D_MODEL))],
        out_specs=rows(D_MODEL),
        out_shape=jax.ShapeDtypeStruct((n_rows, D_MODEL), F32),
        scratch_shapes=[pltpu.VMEM((tm, D_FF), BF16)],
        compiler_params=pltpu.CompilerParams(
            dimension_semantics=("arbitrary",), vmem_limit_bytes=VMEM_LIMIT),
        name="ffn_out",
    )(h1, o, wo, g2, w2i, w2o)


def kernel(x, meta_tokens, ffn1_norm, ffn1_w_in, ffn1_w_out, mix_norm, w_in,
           hgrn_lb_logits, hgrn_out_norm, sb_q_norm, sb_k_norm, w_out,
           ffn2_norm, ffn2_w_in, ffn2_w_out):
    b, seq, _ = x.shape
    assert b == 1 and seq % BLOCK == 0
    assert ffn1_norm.shape[0] == 1, "single layer"
    n_real_blk = seq // BLOCK
    tm = 512
    assert seq % tm == 0

    meta_tile = jnp.zeros((BLOCK, D_MODEL), x.dtype).at[PAD:].set(
        meta_tokens.astype(x.dtype))

    gk = jnp.tile(sb_k_norm[0], SB_HEADS).reshape(SB_W, 1)
    gq = jnp.tile(sb_q_norm[0], 2).reshape(1, BLOCK)

    h1, hf, rec, wo, w2i, w2o = _ffn_in(
        x[0], meta_tile, ffn1_norm, ffn1_w_in[0].astype(BF16),
        ffn1_w_out[0].astype(BF16), mix_norm, w_in[0].astype(BF16), gk, gq,
        hgrn_lb_logits, (w_out[0], ffn2_w_in[0], ffn2_w_out[0]), tm)
    o = _mixer(hf, rec, hgrn_out_norm, n_real_blk)
    out = _ffn_out(h1, o, wo, ffn2_norm, w2i, w2o, seq, 2 * tm)
    return out[None]
```

```python
import functools

import numpy as np
import jax
import jax.numpy as jnp
from jax import lax
from jax.experimental import pallas as pl
from jax.experimental.pallas import tpu as pltpu

F32 = jnp.float32
BF16 = jnp.bfloat16

D_MODEL = 1024
N_META = 16
BLOCK = 128
PAD = (-N_META) % BLOCK
HG_HEADS = 4
HG_DK = 128
HG_DV = 128
HG_W = HG_HEADS * HG_DK
SB_HEADS = 8
SB_DH = 64
SB_W = SB_HEADS * SB_DH
SB_PAIRS = SB_HEADS // 2
D_FF = 2816
RMS_EPS = 1e-6
FF_CHUNK = 256
N_LEVELS = 7
N_FINE_LEVELS = 3
LOG2E = 1.4426950408889634
EXP2_ZERO_BELOW = -150.0
FAR_ROWS = 48
REC_QN = 0
REC_KT = REC_QN + SB_HEADS * BLOCK
REC_V = REC_KT + SB_W
REC_HV = REC_V + SB_PAIRS * BLOCK
REC_ROWS = REC_HV + HG_HEADS * BLOCK
KV_ROWS = REC_HV - REC_KT
VMEM_LIMIT = 56 * 1024 * 1024
NT_DIMS = (((1,), (1,)), ((), ()))
TN_DIMS = (((0,), (0,)), ((), ()))


def _dot(a, b):
    return jnp.dot(a, b, preferred_element_type=F32)


def _dot_nt(a, b):
    return lax.dot_general(a, b, NT_DIMS, preferred_element_type=F32)


def _rms(x, gain):
    ms = jnp.mean(x * x, axis=-1, keepdims=True)
    return x * lax.rsqrt(ms + RMS_EPS) * gain


def _split2(x):
    hi = x.astype(BF16)
    lo = (x - hi.astype(F32)).astype(BF16)
    return hi, lo


def _neg_abs(x):
    return -jnp.abs(x)


def _silu(x):
    return x * jax.nn.sigmoid(x)


def _blk(i):
    return slice(i * BLOCK, (i + 1) * BLOCK)


def _swiglu_stages(xn, w_in_ref, w_out_ref, act_ref):
    for c in range(D_FF // FF_CHUNK):
        lo, hi = c * FF_CHUNK, (c + 1) * FF_CHUNK
        g = _dot(xn, w_in_ref[:, lo:hi])
        u = _dot(xn, w_in_ref[:, D_FF + lo:D_FF + hi])
        act_ref[:, lo:hi] = (_silu(g) * u).astype(BF16)
        yield
    return _dot(act_ref[...], w_out_ref[...])


def _interleave(*stage_lists):
    results = [None] * len(stage_lists)
    active = dict(enumerate(stage_lists))
    while active:
        for i, g in list(active.items()):
            try:
                next(g)
            except StopIteration as stop:
                results[i] = stop.value
                del active[i]
    return results


def _swiglu(xn, w_in_ref, w_out_ref, act_ref):
    return _interleave(_swiglu_stages(xn, w_in_ref, w_out_ref, act_ref))[0]


def _ffn_in_tile(h, pads, g1_ref, w1i_ref, w1o_ref, gm_ref, whg_ref, wq_ref,
                 wk_ref, wv_ref, gk_ref, gq_ref, lbl_ref,
                 h1_ref, hf_ref, rec_ref, act_ref):
    n = h.shape[0]
    rows = slice(0, n)
    xn = _rms(h, g1_ref[...]).astype(BF16)
    h1 = h + 0.5 * _swiglu(xn, w1i_ref, w1o_ref, act_ref.at[rows])
    h1_ref[rows] = h1
    xm = _rms(h1, gm_ref[...]).astype(BF16)

    def put(base, piece, val):
        for t in range(n // BLOCK):
            lo = base + piece * BLOCK
            rec_ref[t, lo:lo + BLOCK, :] = val[_blk(t)]

    q = _dot(xm, wq_ref[...])
    low = lax.broadcasted_iota(jnp.int32, (n, BLOCK), 1) < SB_DH
    qscale = gq_ref[...] * (LOG2E / np.sqrt(np.float32(SB_DH)))
    for p in range(SB_PAIRS):
        qp = q[:, _blk(p)]
        for a in range(2):
            own = low if a == 0 else jnp.logical_not(low)
            ms = jnp.sum(jnp.where(own, qp * qp, 0.0), axis=-1,
                         keepdims=True) * (1.0 / SB_DH)
            qn = jnp.where(own, qp * lax.rsqrt(ms + RMS_EPS) * qscale, 0.0)
            put(REC_QN, 2 * p + a, qn.astype(BF16))
    v = _dot(xm, wv_ref[...]).astype(BF16)
    for p in range(SB_PAIRS):
        put(REC_V, p, v[:, _blk(p)])
    kt = lax.dot_general(wk_ref[...], xm, (((0,), (1,)), ((), ())),
                         preferred_element_type=F32)
    k3 = kt.reshape(SB_HEADS, SB_DH, n)
    ms = jnp.mean(k3 * k3, axis=1, keepdims=True)
    kn = (k3 * lax.rsqrt(ms + RMS_EPS)).reshape(SB_W, n) * gk_ref[...]
    kn = kn.astype(BF16)
    for t in range(n // BLOCK):
        rec_ref[t, REC_KT:REC_KT + SB_W, :] = kn[:, _blk(t)]

    part = lambda i: _dot(xm, whg_ref[:, i * HG_W:(i + 1) * HG_W])
    hv = part(2).astype(BF16)
    for hd in range(HG_HEADS):
        put(REC_HV, hd, hv[:, _blk(hd)])
    hf_ref[rows, 0:HG_W] = _silu(part(0))
    hf_ref[rows, 3 * HG_W:4 * HG_W] = _silu(part(3))
    lg = lbl_ref[...]
    e = jnp.exp(lg - jnp.max(lg, axis=0, keepdims=True))
    lb = e[0:1] / jnp.sum(e, axis=0, keepdims=True)
    z = part(1)
    ez = jnp.exp(_neg_abs(z))
    rz = 1.0 / (1.0 + ez)
    erz = ez * rz
    pos = z >= 0.0
    lf = jnp.log2(lb + (1.0 - lb) * jnp.where(pos, rz, erz))
    k = (1.0 - lb) * jnp.where(pos, erz, rz)
    if pads:
        valid = lax.broadcasted_iota(jnp.int32, z.shape, 0) >= PAD
        lf, k = jnp.where(valid, lf, 0.0), jnp.where(valid, k, 0.0)
    hf_ref[rows, HG_W:2 * HG_W] = lf
    hf_ref[rows, 2 * HG_W:3 * HG_W] = k


N_FFN_IN_PARAMS = 11


def _ffn_in_kernel(x_ref, meta_ref, *refs, cast_chunks):
    nc = len(cast_chunks)
    params = refs[:N_FFN_IN_PARAMS]
    cast_in = refs[N_FFN_IN_PARAMS:N_FFN_IN_PARAMS + nc]
    outs = refs[N_FFN_IN_PARAMS + nc:N_FFN_IN_PARAMS + nc + 3]
    cast_out = refs[N_FFN_IN_PARAMS + nc + 3:N_FFN_IN_PARAMS + 2 * nc + 3]
    act_ref = refs[-1]
    i = pl.program_id(0)
    is_meta = i == pl.num_programs(0) - 1

    for src, dst, chunks in zip(cast_in, cast_out, cast_chunks):
        @pl.when(i < chunks)
        def _():
            dst[...] = src[...].astype(BF16)

    @pl.when(jnp.logical_not(is_meta))
    def _():
        _ffn_in_tile(x_ref[...], False, *params, *outs, act_ref)

    @pl.when(is_meta)
    def _():
        _ffn_in_tile(meta_ref[...], True, *params, *outs, act_ref)


def _cast_chunks(n_rows, n_steps):
    return max(k for k in range(1, n_steps + 1)
               if n_rows % k == 0 and (n_rows // k) % 16 == 0)


def _ffn_in(x2d, meta_tile, g1, w1i, w1o, gm, win, gk, gq, lbl, to_cast, tm):
    n_real_tiles = x2d.shape[0] // tm
    lp = x2d.shape[0] + BLOCK
    n_blk = lp // BLOCK
    tb = tm // BLOCK
    const = lambda shape: pl.BlockSpec(shape, lambda i: (0,) * len(shape),
                                       pipeline_mode=pl.Buffered(1))
    rows = lambda w: pl.BlockSpec((tm, w), lambda i: (i, 0))
    wcols = lambda width, start: pl.BlockSpec(
        (D_MODEL, width), lambda i: (0, start // width),
        pipeline_mode=pl.Buffered(1))
    chunks = tuple(_cast_chunks(w.shape[0], n_real_tiles + 1) for w in to_cast)
    cast_specs = [
        pl.BlockSpec((w.shape[0] // k, w.shape[1]),
                     lambda i, k=k: (jnp.minimum(i, k - 1), 0))
        for w, k in zip(to_cast, chunks)]
    return pl.pallas_call(
        functools.partial(_ffn_in_kernel, cast_chunks=chunks),
        grid=(n_real_tiles + 1,),
        in_specs=[pl.BlockSpec((tm, D_MODEL),
                               lambda i: (jnp.minimum(i, n_real_tiles - 1), 0)),
                  const((BLOCK, D_MODEL)), const((1, D_MODEL)),
                  const((D_MODEL, 2 * D_FF)), const((D_FF, D_MODEL)),
                  const((1, D_MODEL)), wcols(4 * HG_W, 0),
                  wcols(SB_W, 4 * HG_W), wcols(SB_W, 4 * HG_W + SB_W),
                  wcols(SB_W, 4 * HG_W + 2 * SB_W), const((SB_W, 1)),
                  const((1, BLOCK)), const((2, HG_W))] + cast_specs,
        out_specs=[rows(D_MODEL), rows(4 * HG_W),
                   pl.BlockSpec((tb, REC_ROWS, BLOCK), lambda i: (i, 0, 0))]
        + cast_specs,
        out_shape=[jax.ShapeDtypeStruct((lp, D_MODEL), F32),
                   jax.ShapeDtypeStruct((lp, 4 * HG_W), F32),
                   jax.ShapeDtypeStruct((n_blk, REC_ROWS, BLOCK), BF16)]
        + [jax.ShapeDtypeStruct(w.shape, BF16) for w in to_cast],
        scratch_shapes=[pltpu.VMEM((tm, D_FF), BF16)],
        compiler_params=pltpu.CompilerParams(
            dimension_semantics=("arbitrary",), vmem_limit_bytes=VMEM_LIMIT),
        name="ffn_in",
    )(x2d, meta_tile, g1, w1i, w1o, gm, win, win, win, win, gk, gq, lbl, *to_cast)


def _hgrn_sum_matrix():
    t = np.arange(BLOCK)[:, None]
    j = np.arange(BLOCK)[None, :]
    mats = [(j <= t)]
    for lvl in range(1, N_FINE_LEVELS):
        c = 1 << lvl
        m = (t // (2 * c)) * (2 * c) + c
        upper = (t >= m) & (j >= m) & (j <= t)
        lower = (t < m) & (j > t) & (j <= m - 1)
        mats.append(upper | lower)
    m = np.concatenate(mats, axis=0).astype(np.float32)
    return np.concatenate([m, m], axis=1)


def _hgrn_stages(hf_ref, rec_ref, gain_ref, m_ref, o_ref, st_ref):
    heads = range(HG_HEADS)
    hs = lambda a, h: a[:, h * HG_DK:(h + 1) * HG_DK]

    q = hf_ref[:, 0:HG_W]
    k = hf_ref[:, 2 * HG_W:3 * HG_W]
    v = jnp.concatenate(
        [rec_ref[REC_HV + h * BLOCK:REC_HV + (h + 1) * BLOCK, :] for h in heads],
        axis=1)
    lf = hf_ref[:, HG_W:2 * HG_W]
    lf2 = jnp.concatenate(_split2(lf), axis=0)
    x = [_dot(m_ref[i * BLOCK:(i + 1) * BLOCK, :], lf2)
         for i in range(N_FINE_LEVELS)]
    yield
    bcum = x[0]
    b_last = bcum[BLOCK - 1:BLOCK]
    qe = (q * jnp.exp2(bcum)).astype(BF16)
    kd = (k * jnp.exp2(b_last - bcum)).astype(BF16)
    st_decay = jnp.exp2(b_last)

    row = lax.broadcasted_iota(jnp.int32, (BLOCK, BLOCK), 0)
    col = lax.broadcasted_iota(jnp.int32, (BLOCK, BLOCK), 1)
    qb, kb = q.astype(BF16), k.astype(BF16)
    diag = row == col
    attn = [jnp.where(diag, _dot_nt(hs(qb, h), hs(kb, h)), 0.0) for h in heads]
    rowf = lax.broadcasted_iota(jnp.int32, q.shape, 0)
    rowu = lax.broadcasted_iota(jnp.int32, (BLOCK // 2, BLOCK), 0)
    colu = lax.broadcasted_iota(jnp.int32, (BLOCK // 2, BLOCK), 1)
    for lvl in range(N_LEVELS):
        if lvl % 2 == 1:
            yield
        half = 1 << lvl
        if lvl >= N_FINE_LEVELS:
            n_half = BLOCK // half
            qparts, kparts = [], []
            for b in range(n_half):
                rows = slice(b * half, (b + 1) * half)
                if b % 2 == 1:
                    edge = bcum[b * half - 1:b * half]
                    qparts.append(q[rows] * jnp.exp2(bcum[rows] - edge))
                    kparts.append(jnp.zeros((half, q.shape[1]), F32))
                else:
                    edge = bcum[(b + 1) * half - 1:(b + 1) * half]
                    kparts.append(k[rows] * jnp.exp2(edge - bcum[rows]))
            ql = jnp.concatenate(qparts, axis=0).astype(BF16)
            kl = jnp.concatenate(kparts, axis=0).astype(BF16)
            t_up = ((rowu >> lvl) << (lvl + 1)) + half + (rowu & (half - 1))
            same_up = (t_up >> (lvl + 1)) == (colu >> (lvl + 1))
            zero = jnp.zeros((half, BLOCK), F32)
            for h in heads:
                al = _dot_nt(hs(ql, h), hs(kl, h))
                if lvl + 1 < N_LEVELS:
                    al = jnp.where(same_up, al, 0.0)
                attn[h] = attn[h] + jnp.concatenate(
                    [al[(b // 2) * half:(b // 2 + 1) * half] if b % 2 else zero
                     for b in range(n_half)], axis=0)
        else:
            xl = lf if lvl == 0 else x[lvl]
            is_q = ((rowf >> lvl) & 1) == 1
            el = jnp.exp2(jnp.where(is_q, xl, 0.0) if lvl == 0 else xl)
            ql = jnp.where(is_q, q * el, 0.0).astype(BF16)
            kl = jnp.where(is_q, 0.0, k * el).astype(BF16)
            same = (row >> (lvl + 1)) == (col >> (lvl + 1))
            for h in heads:
                al = jnp.where(same, _dot_nt(hs(ql, h), hs(kl, h)), 0.0)
                attn[h] = attn[h] + al
    yield

    o = []
    for h in heads:
        st = st_ref[h]
        oh = _dot(attn[h].astype(BF16), hs(v, h))
        oh = oh + _dot_nt(hs(qe, h), st.astype(BF16))
        st_ref[h] = st * hs(st_decay, h) + lax.dot_general(
            hs(v, h), hs(kd, h), TN_DIMS, preferred_element_type=F32)
        o.append(oh)
    yield
    o = jnp.concatenate([_rms(o[h], hs(gain_ref[...], h)) for h in heads], axis=1)
    o_ref[:, 0:HG_W] = (o * hf_ref[:, 3 * HG_W:4 * HG_W]).astype(BF16)


def _sb_sum_matrix():
    j = np.arange(BLOCK)[:, None]
    s = np.arange(BLOCK)[None, :]
    w = np.concatenate([(j >= s), np.ones((BLOCK, BLOCK), bool)],
                       axis=1).astype(np.float32)
    return np.concatenate([w, w], axis=0)


def _softplus2(z):
    return jnp.maximum(z, 0.0) + jnp.log2(1.0 + jnp.exp2(_neg_abs(z)))


BLOCKS_PER_STEP = 4


def _mixer_kernel(hf_ref, rec_blk_ref, rec_hbm, gain_ref, m_ref, w_ref, o_ref,
                  st_ref, acc_ref, crep_ref, ring, kv_scr, sem, *, n_real_blk):
    n_blk = rec_hbm.shape[0]
    s = pl.program_id(0)
    subs = range(BLOCKS_PER_STEP)
    blk_of = lambda u: jnp.where(s == 0, 0, BLOCKS_PER_STEP * (s - 1) + 1 + u)
    rec_of = lambda u: rec_blk_ref.at[u]
    kv_of = lambda u: rec_blk_ref.at[u, REC_KT:REC_HV]
    rows_of = lambda u: slice(u * BLOCK, (u + 1) * BLOCK)

    def hgrn(u):
        return _hgrn_stages(hf_ref.at[rows_of(u)], rec_of(u), gain_ref, m_ref,
                            o_ref.at[rows_of(u)], st_ref)

    row = lax.broadcasted_iota(jnp.int32, (BLOCK, BLOCK), 0)
    col = lax.broadcasted_iota(jnp.int32, (BLOCK, BLOCK), 1)
    low = col < SB_DH
    pairs = range(SB_PAIRS)
    heads = range(SB_HEADS)

    def fold_stages(u, j_top, tiles, srcs, fresh, keep=True):
        c = blk_of(u)
        rec_ref, o_u = rec_of(u), o_ref.at[rows_of(u)]
        n = len(tiles)
        nrows = [t[1] for t in tiles]
        assert all(nr == BLOCK for nr in nrows[:-1])
        masks = []
        for i, (kind, nr) in enumerate(tiles):
            assert kind is None or nr == BLOCK
            if kind == "diag":
                m = col < row
            elif kind == "general":
                kpos = (j_top - i) * BLOCK + col
                m = jnp.logical_and(kpos < c * BLOCK + row, kpos >= PAD)
            else:
                m = None
            masks.append(m)
        z = {}
        for p in pairs:
            kt = jnp.concatenate([srcs[i][_blk(p), :] for i in range(n)],
                                 axis=1)
            zz = _dot(rec_ref[REC_QN + 2 * p * BLOCK:REC_QN + 2 * (p + 1) * BLOCK, :],
                      kt)
            for a in range(2):
                for i in range(n):
                    z[2 * p + a, i] = zz[_blk(a), _blk(i)][:nrows[i]]
        yield
        r = {}
        for i in range(n):
            packed = []
            for hd in heads:
                sp = _softplus2(z[hd, i])
                if masks[i] is not None:
                    sp = jnp.where(masks[i], sp, 0.0)
                packed.append(jnp.concatenate(_split2(sp), axis=1))
            r[i] = _dot(jnp.concatenate(packed, axis=0), w_ref[...])
            yield
        w = {}
        cmax = None
        for hd in heads:
            crep = None if fresh else crep_ref[u, hd]
            for i in range(n):
                nr = nrows[i]
                rh = r[i][hd * nr:(hd + 1) * nr]
                arg = z[hd, i] - rh[:, :BLOCK]
                wh = jnp.exp2(arg if crep is None else arg + crep[:nr])
                if masks[i] is not None:
                    wh = jnp.where(masks[i], wh, 0.0)
                w[hd, i] = wh.astype(BF16)
                if crep is None:
                    crep = -rh[:, BLOCK:]
                elif nr == BLOCK:
                    crep = crep - rh[:, BLOCK:]
                else:
                    crep = jnp.concatenate(
                        [crep[:nr] - rh[:, BLOCK:], crep[nr:]], axis=0)
            if keep:
                crep_ref[u, hd] = crep
            cmax = crep if cmax is None else jnp.maximum(cmax, crep)
        yield
        full = [i for i in range(n) if nrows[i] == BLOCK]
        for p in pairs:
            wp = jnp.concatenate(
                [jnp.concatenate([w[2 * p + a, i] for i in full], axis=1)
                 for a in range(2)], axis=0)
            vtile = lambda i: srcs[i][SB_W + p * BLOCK:SB_W + (p + 1) * BLOCK, :]
            pv = _dot(wp, jnp.concatenate([vtile(i) for i in full], axis=0))
            pv = jnp.where(low, pv[:BLOCK], pv[BLOCK:])
            if nrows[-1] < BLOCK:
                nr = nrows[-1]
                ps = _dot(jnp.concatenate([w[2 * p, n - 1], w[2 * p + 1, n - 1]],
                                          axis=0), vtile(n - 1))
                low_nr = lax.broadcasted_iota(jnp.int32, (nr, BLOCK), 1) < SB_DH
                ps = jnp.where(low_nr, ps[:nr], ps[nr:])
                pv = jnp.concatenate([pv[:nr] + ps, pv[nr:]], axis=0)
            if fresh:
                o_u[:, HG_W + p * BLOCK:HG_W + (p + 1) * BLOCK] = pv.astype(BF16)
                if keep:
                    acc_ref[u, p] = pv
            else:
                acc_ref[u, p] = acc_ref[u, p] + pv
        return jnp.max(cmax)

    n_fast = 3
    fast = s >= 2
    no_more = (jnp.float32(-jnp.inf),)

    fast_tiles = (("diag", BLOCK), (None, BLOCK), (None, FAR_ROWS))

    def fast_step():
        tiles = fast_tiles
        back = lambda u, i: kv_of(u - i) if u >= i else ring.at[u - i + 2]
        def delayed(gen, turns):
            for _ in range(turns):
                yield
            return (yield from gen)

        res = _interleave(
            *[delayed(fold_stages(u, blk_of(u), tiles,
                                  [back(u, i) for i in range(3)], True, keep=False),
                      4 * u)
              for u in subs], *[delayed(hgrn(u), 4 * u + 2) for u in subs])
        for i in range(2):
            ring[i] = kv_of(BLOCKS_PER_STEP - 2 + i)[...]
        return tuple(res[:BLOCKS_PER_STEP])

    def first_steps():
        @pl.when(s == 0)
        def _():
            st_ref[...] = jnp.zeros_like(st_ref)

        def start(u):
            if u >= n_fast - 1:
                tiles, srcs = fast_tiles, [kv_of(u - i) for i in range(3)]
            else:
                tiles, srcs = (("general", BLOCK),), [kv_of(u)]
            return lambda: (_interleave(
                fold_stages(u, blk_of(u), tiles, srcs, True,
                            keep=u < n_fast - 1), hgrn(u))[0],)
        res = start(0)()
        for u in subs[1:]:
            res = res + lax.cond(s >= 1, start(u), lambda: no_more)

        @pl.when(s == 0)
        def _():
            ring[1] = kv_of(0)[...]

        @pl.when(s > 0)
        def _():
            for i in range(2):
                ring[i] = kv_of(BLOCKS_PER_STEP - 2 + i)[...]
        return res

    first = lax.cond(fast, fast_step, first_steps)

    def fold(u, j_top, kind, fresh=False):
        pj = lax.rem(j_top + n_real_blk, n_blk)
        cp = pltpu.make_async_copy(rec_hbm.at[pj, REC_KT:REC_HV], kv_scr, sem.at[0])
        cp.start()
        cp.wait()
        return _interleave(fold_stages(u, j_top, ((kind, BLOCK),), [kv_scr],
                                       fresh))[0]

    def walk_back(u):
        cmax0 = first[u]
        c = blk_of(u)

        @pl.when(cmax0 > EXP2_ZERO_BELOW)
        def _():
            cmax1 = lax.cond(c >= n_fast,
                             lambda: fold(u, c, "general", fresh=True),
                             lambda: cmax0)

            def cond(carry):
                j, cmax = carry
                return jnp.logical_and(j >= 1, cmax > EXP2_ZERO_BELOW)

            def body(carry):
                j, _ = carry
                return j - 1, fold(u, j, None)

            j_end, cmax_end = lax.while_loop(cond, body, (c - 1, cmax1))

            @pl.when(jnp.logical_and(j_end == 0, cmax_end > EXP2_ZERO_BELOW))
            def _():
                fold(u, 0, "general")

            for p in pairs:
                o_ref[rows_of(u), HG_W + p * BLOCK:HG_W + (p + 1) * BLOCK] = (
                    acc_ref[u, p].astype(BF16))

    @pl.when(functools.reduce(jnp.maximum, first) > EXP2_ZERO_BELOW)
    def _():
        for u in subs:
            walk_back(u)


def _mixer(hf, rec, out_gain, n_real_blk):
    n_blk = rec.shape[0]
    lp = n_blk * BLOCK
    assert n_real_blk % BLOCKS_PER_STEP == 0 and n_blk == n_real_blk + 1
    n_steps = n_real_blk // BLOCKS_PER_STEP + 1
    tile = lambda s: (s + n_steps - 1) % n_steps
    rows = BLOCKS_PER_STEP * BLOCK
    whole = lambda a: pl.BlockSpec(a.shape, lambda s: (0,) * a.ndim)
    msum = jnp.asarray(_hgrn_sum_matrix(), dtype=BF16)
    wsum = jnp.asarray(_sb_sum_matrix(), dtype=BF16)
    return pl.pallas_call(
        functools.partial(_mixer_kernel, n_real_blk=n_real_blk),
        grid=(n_steps,),
        in_specs=[pl.BlockSpec((rows, 4 * HG_W), lambda s: (tile(s), 0)),
                  pl.BlockSpec((BLOCKS_PER_STEP, REC_ROWS, BLOCK),
                               lambda s: (tile(s), 0, 0)),
                  pl.BlockSpec(memory_space=pl.ANY),
                  whole(out_gain), whole(msum), whole(wsum)],
        out_specs=pl.BlockSpec((rows, HG_W + SB_W), lambda s: (tile(s), 0)),
        out_shape=jax.ShapeDtypeStruct((lp, HG_W + SB_W), BF16),
        scratch_shapes=[pltpu.VMEM((HG_HEADS, HG_DV, HG_DK), F32),
                        pltpu.VMEM((BLOCKS_PER_STEP, SB_PAIRS, BLOCK, BLOCK), F32),
                        pltpu.VMEM((BLOCKS_PER_STEP, SB_HEADS, BLOCK, BLOCK), F32),
                        pltpu.VMEM((2, KV_ROWS, BLOCK), BF16),
                        pltpu.VMEM((KV_ROWS, BLOCK), BF16),
                        pltpu.SemaphoreType.DMA((1,))],
        compiler_params=pltpu.CompilerParams(
            dimension_semantics=("arbitrary",)),
        name="mixer",
    )(hf, rec, rec, out_gain, msum, wsum)


def _ffn_out_kernel(h1_ref, o_ref, wo_ref, g2_ref, w2i_ref, w2o_ref, out_ref,
                    act_ref):
    h2 = h1_ref[...] + _dot(o_ref[...], wo_ref[...])
    xn = _rms(h2, g2_ref[...]).astype(BF16)
    out_ref[...] = h2 + 0.5 * _swiglu(xn, w2i_ref, w2o_ref, act_ref)


def _ffn_out(h1, o, wo, g2, w2i, w2o, n_rows, tm):
    const = lambda shape: pl.BlockSpec(shape, lambda i: (0,) * len(shape),
                                       pipeline_mode=pl.Buffered(1))
    rows = lambda w: pl.BlockSpec((tm, w), lambda i: (i, 0))
    return pl.pallas_call(
        _ffn_out_kernel,
        grid=(n_rows // tm,),
        in_specs=[rows(D_MODEL), rows(HG_W + SB_W),
                  const((HG_W + SB_W, D_MODEL)),
                  const((1, D_MODEL)), const((D_MODEL, 2 * D_FF)),
                  const((D_FF, D_MODEL))],
        out_specs=rows(D_MODEL),
        out_shape=jax.ShapeDtypeStruct((n_rows, D_MODEL), F32),
        scratch_shapes=[pltpu.VMEM((tm, D_FF), BF16)],
        compiler_params=pltpu.CompilerParams(
            dimension_semantics=("arbitrary",), vmem_limit_bytes=VMEM_LIMIT),
        name="ffn_out",
    )(h1, o, wo, g2, w2i, w2o)


def kernel(x, meta_tokens, ffn1_norm, ffn1_w_in, ffn1_w_out, mix_norm, w_in,
           hgrn_lb_logits, hgrn_out_norm, sb_q_norm, sb_k_norm, w_out,
           ffn2_norm, ffn2_w_in, ffn2_w_out):
    b, seq, _ = x.shape
    assert b == 1 and seq % BLOCK == 0
    assert ffn1_norm.shape[0] == 1, "single layer"
    n_real_blk = seq // BLOCK
    tm = 512
    assert seq % tm == 0

    meta_tile = jnp.zeros((BLOCK, D_MODEL), x.dtype).at[PAD:].set(
        meta_tokens.astype(x.dtype))

    gk = jnp.tile(sb_k_norm[0], SB_HEADS).reshape(SB_W, 1)
    gq = jnp.tile(sb_q_norm[0], 2).reshape(1, BLOCK)

    h1, hf, rec, wo, w2i, w2o = _ffn_in(
        x[0], meta_tile, ffn1_norm, ffn1_w_in[0].astype(BF16),
        ffn1_w_out[0].astype(BF16), mix_norm, w_in[0].astype(BF16), gk, gq,
        hgrn_lb_logits, (w_out[0], ffn2_w_in[0], ffn2_w_out[0]), tm)
    o = _mixer(hf, rec, hgrn_out_norm, n_real_blk)
    out = _ffn_out(h1, o, wo, ffn2_norm, w2i, w2o, seq, 2 * tm)
    return out[None]
```

```python
import functools

import numpy as np
import jax
import jax.numpy as jnp
from jax import lax
from jax.experimental import pallas as pl
from jax.experimental.pallas import tpu as pltpu

F32 = jnp.float32
BF16 = jnp.bfloat16

D_MODEL = 1024
N_META = 16
BLOCK = 128
PAD = (-N_META) % BLOCK
HG_HEADS = 4
HG_DK = 128
HG_DV = 128
HG_W = HG_HEADS * HG_DK
SB_HEADS = 8
SB_DH = 64
SB_W = SB_HEADS * SB_DH
SB_PAIRS = SB_HEADS // 2
D_FF = 2816
RMS_EPS = 1e-6
FF_CHUNK = 256
N_LEVELS = 7
N_FINE_LEVELS = 3
LOG2E = 1.4426950408889634
EXP2_ZERO_BELOW = -150.0
FAR_ROWS = 48
REC_QN = 0
REC_KT = REC_QN + SB_HEADS * BLOCK
REC_V = REC_KT + SB_W
REC_HV = REC_V + SB_PAIRS * BLOCK
REC_ROWS = REC_HV + HG_HEADS * BLOCK
KV_ROWS = REC_HV - REC_KT
VMEM_LIMIT = 56 * 1024 * 1024
NT_DIMS = (((1,), (1,)), ((), ()))
TN_DIMS = (((0,), (0,)), ((), ()))


def _dot(a, b):
    return jnp.dot(a, b, preferred_element_type=F32)


def _dot_nt(a, b):
    return lax.dot_general(a, b, NT_DIMS, preferred_element_type=F32)


def _rms(x, gain):
    ms = jnp.mean(x * x, axis=-1, keepdims=True)
    return x * lax.rsqrt(ms + RMS_EPS) * gain


def _split2(x):
    hi = x.astype(BF16)
    lo = (x - hi.astype(F32)).astype(BF16)
    return hi, lo


def _neg_abs(x):
    return -jnp.abs(x)


def _silu(x):
    return x * jax.nn.sigmoid(x)


def _blk(i):
    return slice(i * BLOCK, (i + 1) * BLOCK)


def _swiglu_stages(xn, w_in_ref, w_out_ref, act_ref):
    for c in range(D_FF // FF_CHUNK):
        lo, hi = c * FF_CHUNK, (c + 1) * FF_CHUNK
        g = _dot(xn, w_in_ref[:, lo:hi])
        u = _dot(xn, w_in_ref[:, D_FF + lo:D_FF + hi])
        act_ref[:, lo:hi] = (_silu(g) * u).astype(BF16)
        yield
    return _dot(act_ref[...], w_out_ref[...])


def _interleave(*stage_lists):
    results = [None] * len(stage_lists)
    active = dict(enumerate(stage_lists))
    while active:
        for i, g in list(active.items()):
            try:
                next(g)
            except StopIteration as stop:
                results[i] = stop.value
                del active[i]
    return results


def _swiglu(xn, w_in_ref, w_out_ref, act_ref):
    return _interleave(_swiglu_stages(xn, w_in_ref, w_out_ref, act_ref))[0]


def _ffn_in_tile(h, pads, g1_ref, w1i_ref, w1o_ref, gm_ref, whg_ref, wq_ref,
                 wk_ref, wv_ref, gk_ref, gq_ref, lbl_ref,
                 h1_ref, hf_ref, rec_ref, act_ref):
    n = h.shape[0]
    rows = slice(0, n)
    xn = _rms(h, g1_ref[...]).astype(BF16)
    h1 = h + 0.5 * _swiglu(xn, w1i_ref, w1o_ref, act_ref.at[rows])
    h1_ref[rows] = h1
    xm = _rms(h1, gm_ref[...]).astype(BF16)

    def put(base, piece, val):
        for t in range(n // BLOCK):
            lo = base + piece * BLOCK
            rec_ref[t, lo:lo + BLOCK, :] = val[_blk(t)]

    q = _dot(xm, wq_ref[...])
    low = lax.broadcasted_iota(jnp.int32, (n, BLOCK), 1) < SB_DH
    qscale = gq_ref[...] * (LOG2E / np.sqrt(np.float32(SB_DH)))
    for p in range(SB_PAIRS):
        qp = q[:, _blk(p)]
        for a in range(2):
            own = low if a == 0 else jnp.logical_not(low)
            ms = jnp.sum(jnp.where(own, qp * qp, 0.0), axis=-1,
                         keepdims=True) * (1.0 / SB_DH)
            qn = jnp.where(own, qp * lax.rsqrt(ms + RMS_EPS) * qscale, 0.0)
            put(REC_QN, 2 * p + a, qn.astype(BF16))
    v = _dot(xm, wv_ref[...]).astype(BF16)
    for p in range(SB_PAIRS):
        put(REC_V, p, v[:, _blk(p)])
    kt = lax.dot_general(wk_ref[...], xm, (((0,), (1,)), ((), ())),
                         preferred_element_type=F32)
    k3 = kt.reshape(SB_HEADS, SB_DH, n)
    ms = jnp.mean(k3 * k3, axis=1, keepdims=True)
    kn = (k3 * lax.rsqrt(ms + RMS_EPS)).reshape(SB_W, n) * gk_ref[...]
    kn = kn.astype(BF16)
    for t in range(n // BLOCK):
        rec_ref[t, REC_KT:REC_KT + SB_W, :] = kn[:, _blk(t)]

    part = lambda i: _dot(xm, whg_ref[:, i * HG_W:(i + 1) * HG_W])
    hv = part(2).astype(BF16)
    for hd in range(HG_HEADS):
        put(REC_HV, hd, hv[:, _blk(hd)])
    hf_ref[rows, 0:HG_W] = _silu(part(0))
    hf_ref[rows, 3 * HG_W:4 * HG_W] = _silu(part(3))
    lg = lbl_ref[...]
    e = jnp.exp(lg - jnp.max(lg, axis=0, keepdims=True))
    lb = e[0:1] / jnp.sum(e, axis=0, keepdims=True)
    z = part(1)
    ez = jnp.exp(_neg_abs(z))
    rz = 1.0 / (1.0 + ez)
    erz = ez * rz
    pos = z >= 0.0
    lf = jnp.log2(lb + (1.0 - lb) * jnp.where(pos, rz, erz))
    k = (1.0 - lb) * jnp.where(pos, erz, rz)
    if pads:
        valid = lax.broadcasted_iota(jnp.int32, z.shape, 0) >= PAD
        lf, k = jnp.where(valid, lf, 0.0), jnp.where(valid, k, 0.0)
    hf_ref[rows, HG_W:2 * HG_W] = lf
    hf_ref[rows, 2 * HG_W:3 * HG_W] = k


N_FFN_IN_PARAMS = 11


def _ffn_in_kernel(x_ref, meta_ref, *refs, cast_chunks):
    nc = len(cast_chunks)
    params = refs[:N_FFN_IN_PARAMS]
    cast_in = refs[N_FFN_IN_PARAMS:N_FFN_IN_PARAMS + nc]
    outs = refs[N_FFN_IN_PARAMS + nc:N_FFN_IN_PARAMS + nc + 3]
    cast_out = refs[N_FFN_IN_PARAMS + nc + 3:N_FFN_IN_PARAMS + 2 * nc + 3]
    act_ref = refs[-1]
    i = pl.program_id(0)
    is_meta = i == pl.num_programs(0) - 1

    for src, dst, chunks in zip(cast_in, cast_out, cast_chunks):
        @pl.when(i < chunks)
        def _():
            dst[...] = src[...].astype(BF16)

    @pl.when(jnp.logical_not(is_meta))
    def _():
        _ffn_in_tile(x_ref[...], False, *params, *outs, act_ref)

    @pl.when(is_meta)
    def _():
        _ffn_in_tile(meta_ref[...], True, *params, *outs, act_ref)


def _cast_chunks(n_rows, n_steps):
    return max(k for k in range(1, n_steps + 1)
               if n_rows % k == 0 and (n_rows // k) % 16 == 0)


def _ffn_in(x2d, meta_tile, g1, w1i, w1o, gm, win, gk, gq, lbl, to_cast, tm):
    n_real_tiles = x2d.shape[0] // tm
    lp = x2d.shape[0] + BLOCK
    n_blk = lp // BLOCK
    tb = tm // BLOCK
    const = lambda shape: pl.BlockSpec(shape, lambda i: (0,) * len(shape),
                                       pipeline_mode=pl.Buffered(1))
    rows = lambda w: pl.BlockSpec((tm, w), lambda i: (i, 0))
    wcols = lambda width, start: pl.BlockSpec(
        (D_MODEL, width), lambda i: (0, start // width),
        pipeline_mode=pl.Buffered(1))
    chunks = tuple(_cast_chunks(w.shape[0], n_real_tiles + 1) for w in to_cast)
    cast_specs = [
        pl.BlockSpec((w.shape[0] // k, w.shape[1]),
                     lambda i, k=k: (jnp.minimum(i, k - 1), 0))
        for w, k in zip(to_cast, chunks)]
    return pl.pallas_call(
        functools.partial(_ffn_in_kernel, cast_chunks=chunks),
        grid=(n_real_tiles + 1,),
        in_specs=[pl.BlockSpec((tm, D_MODEL),
                               lambda i: (jnp.minimum(i, n_real_tiles - 1), 0)),
                  const((BLOCK, D_MODEL)), const((1, D_MODEL)),
                  const((D_MODEL, 2 * D_FF)), const((D_FF, D_MODEL)),
                  const((1, D_MODEL)), wcols(4 * HG_W, 0),
                  wcols(SB_W, 4 * HG_W), wcols(SB_W, 4 * HG_W + SB_W),
                  wcols(SB_W, 4 * HG_W + 2 * SB_W), const((SB_W, 1)),
                  const((1, BLOCK)), const((2, HG_W))] + cast_specs,
        out_specs=[rows(D_MODEL), rows(4 * HG_W),
                   pl.BlockSpec((tb, REC_ROWS, BLOCK), lambda i: (i, 0, 0))]
        + cast_specs,
        out_shape=[jax.ShapeDtypeStruct((lp, D_MODEL), F32),
                   jax.ShapeDtypeStruct((lp, 4 * HG_W), F32),
                   jax.ShapeDtypeStruct((n_blk, REC_ROWS, BLOCK), BF16)]
        + [jax.ShapeDtypeStruct(w.shape, BF16) for w in to_cast],
        scratch_shapes=[pltpu.VMEM((tm, D_FF), BF16)],
        compiler_params=pltpu.CompilerParams(
            dimension_semantics=("arbitrary",), vmem_limit_bytes=VMEM_LIMIT),
        name="ffn_in",
    )(x2d, meta_tile, g1, w1i, w1o, gm, win, win, win, win, gk, gq, lbl, *to_cast)


def _hgrn_sum_matrix():
    t = np.arange(BLOCK)[:, None]
    j = np.arange(BLOCK)[None, :]
    mats = [(j <= t)]
    for lvl in range(1, N_FINE_LEVELS):
        c = 1 << lvl
        m = (t // (2 * c)) * (2 * c) + c
        upper = (t >= m) & (j >= m) & (j <= t)
        lower = (t < m) & (j > t) & (j <= m - 1)
        mats.append(upper | lower)
    m = np.concatenate(mats, axis=0).astype(np.float32)
    return np.concatenate([m, m], axis=1)


def _hgrn_stages(hf_ref, rec_ref, gain_ref, m_ref, o_ref, st_ref):
    heads = range(HG_HEADS)
    hs = lambda a, h: a[:, h * HG_DK:(h + 1) * HG_DK]

    qf = lambda r=slice(None): hf_ref[r, 0:HG_W]
    kf = lambda r=slice(None): hf_ref[r, 2 * HG_W:3 * HG_W]
    vh = lambda h: rec_ref[REC_HV + h * BLOCK:REC_HV + (h + 1) * BLOCK, :]
    lf = hf_ref[:, HG_W:2 * HG_W]
    x = _dot(m_ref[...], jnp.concatenate(_split2(lf), axis=0))
    yield
    bcum = x[0:BLOCK]
    b_last = bcum[BLOCK - 1:BLOCK]
    qe = (qf() * jnp.exp2(bcum)).astype(BF16)
    kd = (kf() * jnp.exp2(b_last - bcum)).astype(BF16)
    st_decay = jnp.exp2(b_last)

    row = lax.broadcasted_iota(jnp.int32, (BLOCK, BLOCK), 0)
    col = lax.broadcasted_iota(jnp.int32, (BLOCK, BLOCK), 1)
    qb, kb = qf().astype(BF16), kf().astype(BF16)
    diag = row == col
    attn = [jnp.where(diag, _dot_nt(hs(qb, h), hs(kb, h)), 0.0) for h in heads]
    rowf = lax.broadcasted_iota(jnp.int32, (BLOCK, HG_W), 0)
    rowu = lax.broadcasted_iota(jnp.int32, (BLOCK // 2, BLOCK), 0)
    colu = lax.broadcasted_iota(jnp.int32, (BLOCK // 2, BLOCK), 1)
    for lvl in range(N_LEVELS):
        if lvl % 2 == 1:
            yield
        half = 1 << lvl
        if lvl >= N_FINE_LEVELS:
            n_half = BLOCK // half
            qparts, kparts = [], []
            for b in range(n_half):
                rows = slice(b * half, (b + 1) * half)
                if b % 2 == 1:
                    edge = bcum[b * half - 1:b * half]
                    qparts.append(qf(rows) * jnp.exp2(bcum[rows] - edge))
                    kparts.append(jnp.zeros((half, HG_W), F32))
                else:
                    edge = bcum[(b + 1) * half - 1:(b + 1) * half]
                    kparts.append(kf(rows) * jnp.exp2(edge - bcum[rows]))
            ql = jnp.concatenate(qparts, axis=0).astype(BF16)
            kl = jnp.concatenate(kparts, axis=0).astype(BF16)
            t_up = ((rowu >> lvl) << (lvl + 1)) + half + (rowu & (half - 1))
            same_up = (t_up >> (lvl + 1)) == (colu >> (lvl + 1))
            zero = jnp.zeros((half, BLOCK), F32)
            for h in heads:
                al = _dot_nt(hs(ql, h), hs(kl, h))
                if lvl + 1 < N_LEVELS:
                    al = jnp.where(same_up, al, 0.0)
                attn[h] = attn[h] + jnp.concatenate(
                    [al[(b // 2) * half:(b // 2 + 1) * half] if b % 2 else zero
                     for b in range(n_half)], axis=0)
        else:
            xl = lf if lvl == 0 else x[lvl * BLOCK:(lvl + 1) * BLOCK]
            is_q = ((rowf >> lvl) & 1) == 1
            el = jnp.exp2(jnp.where(is_q, xl, 0.0) if lvl == 0 else xl)
            ql = jnp.where(is_q, qf() * el, 0.0).astype(BF16)
            kl = jnp.where(is_q, 0.0, kf() * el).astype(BF16)
            same = (row >> (lvl + 1)) == (col >> (lvl + 1))
            for h in heads:
                al = jnp.where(same, _dot_nt(hs(ql, h), hs(kl, h)), 0.0)
                attn[h] = attn[h] + al
    yield

    o = []
    for h in heads:
        st = st_ref[h]
        oh = _dot(attn[h].astype(BF16), vh(h))
        oh = oh + _dot_nt(hs(qe, h), st.astype(BF16))
        st_ref[h] = st * hs(st_decay, h) + lax.dot_general(
            vh(h), hs(kd, h), TN_DIMS, preferred_element_type=F32)
        o.append(oh)
    yield
    o = jnp.concatenate([_rms(o[h], hs(gain_ref[...], h)) for h in heads], axis=1)
    o_ref[:, 0:HG_W] = (o * hf_ref[:, 3 * HG_W:4 * HG_W]).astype(BF16)


def _sb_sum_matrix():
    j = np.arange(BLOCK)[:, None]
    s = np.arange(BLOCK)[None, :]
    w = np.concatenate([(j >= s), np.ones((BLOCK, BLOCK), bool)],
                       axis=1).astype(np.float32)
    return np.concatenate([w, w], axis=0)


def _softplus2(z):
    return jnp.maximum(z, 0.0) + jnp.log2(1.0 + jnp.exp2(_neg_abs(z)))


BLOCKS_PER_STEP = 4


def _mixer_kernel(hf_ref, rec_blk_ref, rec_hbm, gain_ref, m_ref, w_ref, o_ref,
                  st_ref, acc_ref, crep_ref, ring, kv_scr, sem, *, n_real_blk):
    n_blk = rec_hbm.shape[0]
    s = pl.program_id(0)
    subs = range(BLOCKS_PER_STEP)
    blk_of = lambda u: jnp.where(s == 0, 0, BLOCKS_PER_STEP * (s - 1) + 1 + u)
    rec_of = lambda u: rec_blk_ref.at[u]
    kv_of = lambda u: rec_blk_ref.at[u, REC_KT:REC_HV]
    rows_of = lambda u: slice(u * BLOCK, (u + 1) * BLOCK)

    def hgrn(u):
        return _hgrn_stages(hf_ref.at[rows_of(u)], rec_of(u), gain_ref, m_ref,
                            o_ref.at[rows_of(u)], st_ref)

    row = lax.broadcasted_iota(jnp.int32, (BLOCK, BLOCK), 0)
    col = lax.broadcasted_iota(jnp.int32, (BLOCK, BLOCK), 1)
    low = col < SB_DH
    pairs = range(SB_PAIRS)
    heads = range(SB_HEADS)

    def fold_stages(u, j_top, tiles, srcs, fresh, keep=True):
        c = blk_of(u)
        rec_ref, o_u = rec_of(u), o_ref.at[rows_of(u)]
        n = len(tiles)
        nrows = [t[1] for t in tiles]
        assert all(nr == BLOCK for nr in nrows[:-1])
        masks = []
        for i, (kind, nr) in enumerate(tiles):
            assert kind is None or nr == BLOCK
            if kind == "diag":
                m = col < row
            elif kind == "general":
                kpos = (j_top - i) * BLOCK + col
                m = jnp.logical_and(kpos < c * BLOCK + row, kpos >= PAD)
            else:
                m = None
            masks.append(m)
        z = {}
        for p in pairs:
            kt = jnp.concatenate([srcs[i][_blk(p), :] for i in range(n)],
                                 axis=1)
            zz = _dot(rec_ref[REC_QN + 2 * p * BLOCK:REC_QN + 2 * (p + 1) * BLOCK, :],
                      kt)
            for a in range(2):
                for i in range(n):
                    z[2 * p + a, i] = zz[_blk(a), _blk(i)][:nrows[i]]
        yield
        r = {}
        for i in range(n):
            packed = []
            for hd in heads:
                sp = _softplus2(z[hd, i])
                if masks[i] is not None:
                    sp = jnp.where(masks[i], sp, 0.0)
                packed.append(jnp.concatenate(_split2(sp), axis=1))
            r[i] = _dot(jnp.concatenate(packed, axis=0), w_ref[...])
            yield
        w = {}
        cmax = None
        for hd in heads:
            crep = None if fresh else crep_ref[u, hd]
            for i in range(n):
                nr = nrows[i]
                rh = r[i][hd * nr:(hd + 1) * nr]
                arg = z[hd, i] - rh[:, :BLOCK]
                wh = jnp.exp2(arg if crep is None else arg + crep[:nr])
                if masks[i] is not None:
                    wh = jnp.where(masks[i], wh, 0.0)
                w[hd, i] = wh.astype(BF16)
                if crep is None:
                    crep = -rh[:, BLOCK:]
                elif nr == BLOCK:
                    crep = crep - rh[:, BLOCK:]
                else:
                    crep = jnp.concatenate(
                        [crep[:nr] - rh[:, BLOCK:], crep[nr:]], axis=0)
            if keep:
                crep_ref[u, hd] = crep
            cmax = crep if cmax is None else jnp.maximum(cmax, crep)
        yield
        full = [i for i in range(n) if nrows[i] == BLOCK]
        for p in pairs:
            wp = jnp.concatenate(
                [jnp.concatenate([w[2 * p + a, i] for i in full], axis=1)
                 for a in range(2)], axis=0)
            vtile = lambda i: srcs[i][SB_W + p * BLOCK:SB_W + (p + 1) * BLOCK, :]
            pv = _dot(wp, jnp.concatenate([vtile(i) for i in full], axis=0))
            pv = jnp.where(low, pv[:BLOCK], pv[BLOCK:])
            if nrows[-1] < BLOCK:
                nr = nrows[-1]
                ps = _dot(jnp.concatenate([w[2 * p, n - 1], w[2 * p + 1, n - 1]],
                                          axis=0), vtile(n - 1))
                low_nr = lax.broadcasted_iota(jnp.int32, (nr, BLOCK), 1) < SB_DH
                ps = jnp.where(low_nr, ps[:nr], ps[nr:])
                pv = jnp.concatenate([pv[:nr] + ps, pv[nr:]], axis=0)
            if fresh:
                o_u[:, HG_W + p * BLOCK:HG_W + (p + 1) * BLOCK] = pv.astype(BF16)
                if keep:
                    acc_ref[u, p] = pv
            else:
                acc_ref[u, p] = acc_ref[u, p] + pv
        return jnp.max(cmax)

    n_fast = 3
    fast = s >= 2
    no_more = (jnp.float32(-jnp.inf),)

    fast_tiles = (("diag", BLOCK), (None, BLOCK), (None, FAR_ROWS))

    def fast_step():
        tiles = fast_tiles
        back = lambda u, i: kv_of(u - i) if u >= i else ring.at[u - i + 2]
        def delayed(gen, turns):
            for _ in range(turns):
                yield
            return (yield from gen)

        res = _interleave(
            *[delayed(fold_stages(u, blk_of(u), tiles,
                                  [back(u, i) for i in range(3)], True, keep=False),
                      4 * u)
              for u in subs], *[delayed(hgrn(u), 4 * u + 2) for u in subs])
        for i in range(2):
            ring[i] = kv_of(BLOCKS_PER_STEP - 2 + i)[...]
        return tuple(res[:BLOCKS_PER_STEP])

    def first_steps():
        @pl.when(s == 0)
        def _():
            st_ref[...] = jnp.zeros_like(st_ref)

        def start(u):
            if u >= n_fast - 1:
                tiles, srcs = fast_tiles, [kv_of(u - i) for i in range(3)]
            else:
                tiles, srcs = (("general", BLOCK),), [kv_of(u)]
            return lambda: (_interleave(
                fold_stages(u, blk_of(u), tiles, srcs, True,
                            keep=u < n_fast - 1), hgrn(u))[0],)
        res = start(0)()
        for u in subs[1:]:
            res = res + lax.cond(s >= 1, start(u), lambda: no_more)

        @pl.when(s == 0)
        def _():
            ring[1] = kv_of(0)[...]

        @pl.when(s > 0)
        def _():
            for i in range(2):
                ring[i] = kv_of(BLOCKS_PER_STEP - 2 + i)[...]
        return res

    first = lax.cond(fast, fast_step, first_steps)

    def fold(u, j_top, kind, fresh=False):
        pj = lax.rem(j_top + n_real_blk, n_blk)
        cp = pltpu.make_async_copy(rec_hbm.at[pj, REC_KT:REC_HV], kv_scr, sem.at[0])
        cp.start()
        cp.wait()
        return _interleave(fold_stages(u, j_top, ((kind, BLOCK),), [kv_scr],
                                       fresh))[0]

    def walk_back(u):
        cmax0 = first[u]
        c = blk_of(u)

        @pl.when(cmax0 > EXP2_ZERO_BELOW)
        def _():
            cmax1 = lax.cond(c >= n_fast,
                             lambda: fold(u, c, "general", fresh=True),
                             lambda: cmax0)

            def cond(carry):
                j, cmax = carry
                return jnp.logical_and(j >= 1, cmax > EXP2_ZERO_BELOW)

            def body(carry):
                j, _ = carry
                return j - 1, fold(u, j, None)

            j_end, cmax_end = lax.while_loop(cond, body, (c - 1, cmax1))

            @pl.when(jnp.logical_and(j_end == 0, cmax_end > EXP2_ZERO_BELOW))
            def _():
                fold(u, 0, "general")

            for p in pairs:
                o_ref[rows_of(u), HG_W + p * BLOCK:HG_W + (p + 1) * BLOCK] = (
                    acc_ref[u, p].astype(BF16))

    @pl.when(functools.reduce(jnp.maximum, first) > EXP2_ZERO_BELOW)
    def _():
        for u in subs:
            walk_back(u)


def _mixer(hf, rec, out_gain, n_real_blk):
    n_blk = rec.shape[0]
    lp = n_blk * BLOCK
    assert n_real_blk % BLOCKS_PER_STEP == 0 and n_blk == n_real_blk + 1
    n_steps = n_real_blk // BLOCKS_PER_STEP + 1
    tile = lambda s: (s + n_steps - 1) % n_steps
    rows = BLOCKS_PER_STEP * BLOCK
    whole = lambda a: pl.BlockSpec(a.shape, lambda s: (0,) * a.ndim)
    msum = jnp.asarray(_hgrn_sum_matrix(), dtype=BF16)
    wsum = jnp.asarray(_sb_sum_matrix(), dtype=BF16)
    return pl.pallas_call(
        functools.partial(_mixer_kernel, n_real_blk=n_real_blk),
        grid=(n_steps,),
        in_specs=[pl.BlockSpec((rows, 4 * HG_W), lambda s: (tile(s), 0)),
                  pl.BlockSpec((BLOCKS_PER_STEP, REC_ROWS, BLOCK),
                               lambda s: (tile(s), 0, 0)),
                  pl.BlockSpec(memory_space=pl.ANY),
                  whole(out_gain), whole(msum), whole(wsum)],
        out_specs=pl.BlockSpec((rows, HG_W + SB_W), lambda s: (tile(s), 0)),
        out_shape=jax.ShapeDtypeStruct((lp, HG_W + SB_W), BF16),
        scratch_shapes=[pltpu.VMEM((HG_HEADS, HG_DV, HG_DK), F32),
                        pltpu.VMEM((BLOCKS_PER_STEP, SB_PAIRS, BLOCK, BLOCK), F32),
                        pltpu.VMEM((BLOCKS_PER_STEP, SB_HEADS, BLOCK, BLOCK), F32),
                        pltpu.VMEM((2, KV_ROWS, BLOCK), BF16),
                        pltpu.VMEM((KV_ROWS, BLOCK), BF16),
                        pltpu.SemaphoreType.DMA((1,))],
        compiler_params=pltpu.CompilerParams(
            dimension_semantics=("arbitrary",)),
        name="mixer",
    )(hf, rec, rec, out_gain, msum, wsum)


def _ffn_out_kernel(h1_ref, o_ref, wo_ref, g2_ref, w2i_ref, w2o_ref, out_ref,
                    act_ref):
    h2 = h1_ref[...] + _dot(o_ref[...], wo_ref[...])
    xn = _rms(h2, g2_ref[...]).astype(BF16)
    out_ref[...] = h2 + 0.5 * _swiglu(xn, w2i_ref, w2o_ref, act_ref)


def _ffn_out(h1, o, wo, g2, w2i, w2o, n_rows, tm):
    const = lambda shape: pl.BlockSpec(shape, lambda i: (0,) * len(shape),
                                       pipeline_mode=pl.Buffered(1))
    rows = lambda w: pl.BlockSpec((tm, w), lambda i: (i, 0))
    return pl.pallas_call(
        _ffn_out_kernel,
        grid=(n_rows // tm,),
        in_specs=[rows(D_MODEL), rows(HG_W + SB_W),
                  const((HG_W + SB_W, D_MODEL)),
                  const((1, D_MODEL)), const((D_MODEL, 2 * D_FF)),
                  const((D_FF, D_MODEL))],
        out_specs=rows(D_MODEL),
        out_shape=jax.ShapeDtypeStruct((n_rows, D_MODEL), F32),
        scratch_shapes=[pltpu.VMEM((tm, D_FF), BF16)],
        compiler_params=pltpu.CompilerParams(
            dimension_semantics=("arbitrary",), vmem_limit_bytes=VMEM_LIMIT),
        name="ffn_out",
    )(h1, o, wo, g2, w2i, w2o)


def kernel(x, meta_tokens, ffn1_norm, ffn1_w_in, ffn1_w_out, mix_norm, w_in,
           hgrn_lb_logits, hgrn_out_norm, sb_q_norm, sb_k_norm, w_out,
           ffn2_norm, ffn2_w_in, ffn2_w_out):
    b, seq, _ = x.shape
    assert b == 1 and seq % BLOCK == 0
    assert ffn1_norm.shape[0] == 1, "single layer"
    n_real_blk = seq // BLOCK
    tm = 512
    assert seq % tm == 0

    meta_tile = jnp.zeros((BLOCK, D_MODEL), x.dtype).at[PAD:].set(
        meta_tokens.astype(x.dtype))

    gk = jnp.tile(sb_k_norm[0], SB_HEADS).reshape(SB_W, 1)
    gq = jnp.tile(sb_q_norm[0], 2).reshape(1, BLOCK)

    h1, hf, rec, wo, w2i, w2o = _ffn_in(
        x[0], meta_tile, ffn1_norm, ffn1_w_in[0].astype(BF16),
        ffn1_w_out[0].astype(BF16), mix_norm, w_in[0].astype(BF16), gk, gq,
        hgrn_lb_logits, (w_out[0], ffn2_w_in[0], ffn2_w_out[0]), tm)
    o = _mixer(hf, rec, hgrn_out_norm, n_real_blk)
    out = _ffn_out(h1, o, wo, ffn2_norm, w2i, w2o, seq, 2 * tm)
    return out[None]
```

```python
import functools

import numpy as np
import jax
import jax.numpy as jnp
from jax import lax
from jax.experimental import pallas as pl
from jax.experimental.pallas import tpu as pltpu

F32 = jnp.float32
BF16 = jnp.bfloat16

D_MODEL = 1024
N_META = 16
BLOCK = 128
PAD = (-N_META) % BLOCK
HG_HEADS = 4
HG_DK = 128
HG_DV = 128
HG_W = HG_HEADS * HG_DK
SB_HEADS = 8
SB_DH = 64
SB_W = SB_HEADS * SB_DH
SB_PAIRS = SB_HEADS // 2
D_FF = 2816
RMS_EPS = 1e-6
FF_CHUNK = 256
N_LEVELS = 7
N_FINE_LEVELS = 3
LOG2E = 1.4426950408889634
EXP2_ZERO_BELOW = -150.0
FAR_ROWS = 48
REC_QN = 0
REC_KT = REC_QN + SB_HEADS * BLOCK
REC_V = REC_KT + SB_W
REC_HV = REC_V + SB_PAIRS * BLOCK
REC_ROWS = REC_HV + HG_HEADS * BLOCK
KV_ROWS = REC_HV - REC_KT
VMEM_LIMIT = 56 * 1024 * 1024
NT_DIMS = (((1,), (1,)), ((), ()))
TN_DIMS = (((0,), (0,)), ((), ()))


def _dot(a, b):
    return jnp.dot(a, b, preferred_element_type=F32)


def _dot_nt(a, b):
    return lax.dot_general(a, b, NT_DIMS, preferred_element_type=F32)


def _rms(x, gain):
    ms = jnp.mean(x * x, axis=-1, keepdims=True)
    return x * lax.rsqrt(ms + RMS_EPS) * gain


def _split2(x):
    hi = x.astype(BF16)
    lo = (x - hi.astype(F32)).astype(BF16)
    return hi, lo


def _neg_abs(x):
    return -jnp.abs(x)


def _silu(x):
    return x * jax.nn.sigmoid(x)


def _blk(i):
    return slice(i * BLOCK, (i + 1) * BLOCK)


def _swiglu_stages(xn, w_in_ref, w_out_ref, act_ref):
    for c in range(D_FF // FF_CHUNK):
        lo, hi = c * FF_CHUNK, (c + 1) * FF_CHUNK
        g = _dot(xn, w_in_ref[:, lo:hi])
        u = _dot(xn, w_in_ref[:, D_FF + lo:D_FF + hi])
        act_ref[:, lo:hi] = (_silu(g) * u).astype(BF16)
        yield
    return _dot(act_ref[...], w_out_ref[...])


def _interleave(*stage_lists):
    results = [None] * len(stage_lists)
    active = dict(enumerate(stage_lists))
    while active:
        for i, g in list(active.items()):
            try:
                next(g)
            except StopIteration as stop:
                results[i] = stop.value
                del active[i]
    return results


def _swiglu(xn, w_in_ref, w_out_ref, act_ref):
    return _interleave(_swiglu_stages(xn, w_in_ref, w_out_ref, act_ref))[0]


def _ffn_in_tile(h, pads, g1_ref, w1i_ref, w1o_ref, gm_ref, whg_ref, wq_ref,
                 wk_ref, wv_ref, gk_ref, gq_ref, lbl_ref,
                 h1_ref, hf_ref, rec_ref, act_ref):
    n = h.shape[0]
    rows = slice(0, n)
    xn = _rms(h, g1_ref[...]).astype(BF16)
    h1 = h + 0.5 * _swiglu(xn, w1i_ref, w1o_ref, act_ref.at[rows])
    h1_ref[rows] = h1
    xm = _rms(h1, gm_ref[...]).astype(BF16)

    def put(base, piece, val):
        for t in range(n // BLOCK):
            lo = base + piece * BLOCK
            rec_ref[t, lo:lo + BLOCK, :] = val[_blk(t)]

    q = _dot(xm, wq_ref[...])
    low = lax.broadcasted_iota(jnp.int32, (n, BLOCK), 1) < SB_DH
    qscale = gq_ref[...] * (LOG2E / np.sqrt(np.float32(SB_DH)))
    for p in range(SB_PAIRS):
        qp = q[:, _blk(p)]
        for a in range(2):
            own = low if a == 0 else jnp.logical_not(low)
            ms = jnp.sum(jnp.where(own, qp * qp, 0.0), axis=-1,
                         keepdims=True) * (1.0 / SB_DH)
            qn = jnp.where(own, qp * lax.rsqrt(ms + RMS_EPS) * qscale, 0.0)
            put(REC_QN, 2 * p + a, qn.astype(BF16))
    v = _dot(xm, wv_ref[...]).astype(BF16)
    for p in range(SB_PAIRS):
        put(REC_V, p, v[:, _blk(p)])
    kt = lax.dot_general(wk_ref[...], xm, (((0,), (1,)), ((), ())),
                         preferred_element_type=F32)
    k3 = kt.reshape(SB_HEADS, SB_DH, n)
    ms = jnp.mean(k3 * k3, axis=1, keepdims=True)
    kn = (k3 * lax.rsqrt(ms + RMS_EPS)).reshape(SB_W, n) * gk_ref[...]
    kn = kn.astype(BF16)
    for t in range(n // BLOCK):
        rec_ref[t, REC_KT:REC_KT + SB_W, :] = kn[:, _blk(t)]

    part = lambda i: _dot(xm, whg_ref[:, i * HG_W:(i + 1) * HG_W])
    hv = part(2).astype(BF16)
    for hd in range(HG_HEADS):
        put(REC_HV, hd, hv[:, _blk(hd)])
    hf_ref[rows, 0:HG_W] = _silu(part(0))
    hf_ref[rows, 3 * HG_W:4 * HG_W] = _silu(part(3))
    lg = lbl_ref[...]
    e = jnp.exp(lg - jnp.max(lg, axis=0, keepdims=True))
    lb = e[0:1] / jnp.sum(e, axis=0, keepdims=True)
    z = part(1)
    ez = jnp.exp(_neg_abs(z))
    rz = 1.0 / (1.0 + ez)
    erz = ez * rz
    pos = z >= 0.0
    lf = jnp.log2(lb + (1.0 - lb) * jnp.where(pos, rz, erz))
    k = (1.0 - lb) * jnp.where(pos, erz, rz)
    if pads:
        valid = lax.broadcasted_iota(jnp.int32, z.shape, 0) >= PAD
        lf, k = jnp.where(valid, lf, 0.0), jnp.where(valid, k, 0.0)
    hf_ref[rows, HG_W:2 * HG_W] = lf
    hf_ref[rows, 2 * HG_W:3 * HG_W] = k


N_FFN_IN_PARAMS = 11


def _ffn_in_kernel(x_ref, meta_ref, *refs, cast_chunks):
    nc = len(cast_chunks)
    params = refs[:N_FFN_IN_PARAMS]
    cast_in = refs[N_FFN_IN_PARAMS:N_FFN_IN_PARAMS + nc]
    outs = refs[N_FFN_IN_PARAMS + nc:N_FFN_IN_PARAMS + nc + 3]
    cast_out = refs[N_FFN_IN_PARAMS + nc + 3:N_FFN_IN_PARAMS + 2 * nc + 3]
    act_ref = refs[-1]
    i = pl.program_id(0)
    is_meta = i == pl.num_programs(0) - 1

    for src, dst, chunks in zip(cast_in, cast_out, cast_chunks):
        @pl.when(i < chunks)
        def _():
            dst[...] = src[...].astype(BF16)

    @pl.when(jnp.logical_not(is_meta))
    def _():
        _ffn_in_tile(x_ref[...], False, *params, *outs, act_ref)

    @pl.when(is_meta)
    def _():
        _ffn_in_tile(meta_ref[...], True, *params, *outs, act_ref)


def _cast_chunks(n_rows, n_steps):
    return max(k for k in range(1, n_steps + 1)
               if n_rows % k == 0 and (n_rows // k) % 16 == 0)


def _ffn_in(x2d, meta_tile, g1, w1i, w1o, gm, win, gk, gq, lbl, to_cast, tm):
    n_real_tiles = x2d.shape[0] // tm
    lp = x2d.shape[0] + BLOCK
    n_blk = lp // BLOCK
    tb = tm // BLOCK
    const = lambda shape: pl.BlockSpec(shape, lambda i: (0,) * len(shape),
                                       pipeline_mode=pl.Buffered(1))
    rows = lambda w: pl.BlockSpec((tm, w), lambda i: (i, 0))
    wcols = lambda width, start: pl.BlockSpec(
        (D_MODEL, width), lambda i: (0, start // width),
        pipeline_mode=pl.Buffered(1))
    chunks = tuple(_cast_chunks(w.shape[0], n_real_tiles + 1) for w in to_cast)
    cast_specs = [
        pl.BlockSpec((w.shape[0] // k, w.shape[1]),
                     lambda i, k=k: (jnp.minimum(i, k - 1), 0))
        for w, k in zip(to_cast, chunks)]
    return pl.pallas_call(
        functools.partial(_ffn_in_kernel, cast_chunks=chunks),
        grid=(n_real_tiles + 1,),
        in_specs=[pl.BlockSpec((tm, D_MODEL),
                               lambda i: (jnp.minimum(i, n_real_tiles - 1), 0)),
                  const((BLOCK, D_MODEL)), const((1, D_MODEL)),
                  const((D_MODEL, 2 * D_FF)), const((D_FF, D_MODEL)),
                  const((1, D_MODEL)), wcols(4 * HG_W, 0),
                  wcols(SB_W, 4 * HG_W), wcols(SB_W, 4 * HG_W + SB_W),
                  wcols(SB_W, 4 * HG_W + 2 * SB_W), const((SB_W, 1)),
                  const((1, BLOCK)), const((2, HG_W))] + cast_specs,
        out_specs=[rows(D_MODEL), rows(4 * HG_W),
                   pl.BlockSpec((tb, REC_ROWS, BLOCK), lambda i: (i, 0, 0))]
        + cast_specs,
        out_shape=[jax.ShapeDtypeStruct((lp, D_MODEL), F32),
                   jax.ShapeDtypeStruct((lp, 4 * HG_W), F32),
                   jax.ShapeDtypeStruct((n_blk, REC_ROWS, BLOCK), BF16)]
        + [jax.ShapeDtypeStruct(w.shape, BF16) for w in to_cast],
        scratch_shapes=[pltpu.VMEM((tm, D_FF), BF16)],
        compiler_params=pltpu.CompilerParams(
            dimension_semantics=("arbitrary",), vmem_limit_bytes=VMEM_LIMIT),
        name="ffn_in",
    )(x2d, meta_tile, g1, w1i, w1o, gm, win, win, win, win, gk, gq, lbl, *to_cast)


def _hgrn_sum_matrix():
    t = np.arange(BLOCK)[:, None]
    j = np.arange(BLOCK)[None, :]
    mats = [(j <= t)]
    for lvl in range(1, N_FINE_LEVELS):
        c = 1 << lvl
        m = (t // (2 * c)) * (2 * c) + c
        upper = (t >= m) & (j >= m) & (j <= t)
        lower = (t < m) & (j > t) & (j <= m - 1)
        mats.append(upper | lower)
    m = np.concatenate(mats, axis=0).astype(np.float32)
    return np.concatenate([m, m], axis=1)


def _hgrn_stages(hf_ref, rec_ref, gain_ref, m_ref, o_ref, st_ref):
    heads = range(HG_HEADS)
    hs = lambda a, h: a[:, h * HG_DK:(h + 1) * HG_DK]

    qf = lambda r=slice(None): hf_ref[r, 0:HG_W]
    kf = lambda r=slice(None): hf_ref[r, 2 * HG_W:3 * HG_W]
    vh = lambda h: rec_ref[REC_HV + h * BLOCK:REC_HV + (h + 1) * BLOCK, :]
    lf = hf_ref[:, HG_W:2 * HG_W]
    x = _dot(m_ref[...], jnp.concatenate(_split2(lf), axis=0))
    yield
    bcum = x[0:BLOCK]
    b_last = bcum[BLOCK - 1:BLOCK]
    qe = (qf() * jnp.exp2(bcum)).astype(BF16)
    kd = (kf() * jnp.exp2(b_last - bcum)).astype(BF16)
    st_decay = jnp.exp2(b_last)

    row = lax.broadcasted_iota(jnp.int32, (BLOCK, BLOCK), 0)
    col = lax.broadcasted_iota(jnp.int32, (BLOCK, BLOCK), 1)
    qb, kb = qf().astype(BF16), kf().astype(BF16)
    diag = row == col
    attn = [jnp.where(diag, _dot_nt(hs(qb, h), hs(kb, h)), 0.0) for h in heads]
    rowf = lax.broadcasted_iota(jnp.int32, (BLOCK, HG_W), 0)
    rowu = lax.broadcasted_iota(jnp.int32, (BLOCK // 2, BLOCK), 0)
    colu = lax.broadcasted_iota(jnp.int32, (BLOCK // 2, BLOCK), 1)
    for lvl in range(N_LEVELS):
        if lvl % 2 == 1:
            yield
        half = 1 << lvl
        if lvl >= N_FINE_LEVELS:
            n_half = BLOCK // half
            qparts, kparts = [], []
            for b in range(n_half):
                rows = slice(b * half, (b + 1) * half)
                if b % 2 == 1:
                    edge = bcum[b * half - 1:b * half]
                    qparts.append(qf(rows) * jnp.exp2(bcum[rows] - edge))
                    kparts.append(jnp.zeros((half, HG_W), F32))
                else:
                    edge = bcum[(b + 1) * half - 1:(b + 1) * half]
                    kparts.append(kf(rows) * jnp.exp2(edge - bcum[rows]))
            ql = jnp.concatenate(qparts, axis=0).astype(BF16)
            kl = jnp.concatenate(kparts, axis=0).astype(BF16)
            t_up = ((rowu >> lvl) << (lvl + 1)) + half + (rowu & (half - 1))
            same_up = (t_up >> (lvl + 1)) == (colu >> (lvl + 1))
            zero = jnp.zeros((half, BLOCK), F32)
            for h in heads:
                al = _dot_nt(hs(ql, h), hs(kl, h))
                if lvl + 1 < N_LEVELS:
                    al = jnp.where(same_up, al, 0.0)
                attn[h] = attn[h] + jnp.concatenate(
                    [al[(b // 2) * half:(b // 2 + 1) * half] if b % 2 else zero
                     for b in range(n_half)], axis=0)
        else:
            xl = lf if lvl == 0 else x[lvl * BLOCK:(lvl + 1) * BLOCK]
            is_q = ((rowf >> lvl) & 1) == 1
            el = jnp.exp2(jnp.where(is_q, xl, 0.0) if lvl == 0 else xl)
            ql = jnp.where(is_q, qf() * el, 0.0).astype(BF16)
            kl = jnp.where(is_q, 0.0, kf() * el).astype(BF16)
            same = (row >> (lvl + 1)) == (col >> (lvl + 1))
            for h in heads:
                al = jnp.where(same, _dot_nt(hs(ql, h), hs(kl, h)), 0.0)
                attn[h] = attn[h] + al
    yield

    o = []
    for h in heads:
        st = st_ref[h]
        oh = _dot(attn[h].astype(BF16), vh(h))
        oh = oh + _dot_nt(hs(qe, h), st.astype(BF16))
        st_ref[h] = st * hs(st_decay, h) + lax.dot_general(
            vh(h), hs(kd, h), TN_DIMS, preferred_element_type=F32)
        o.append(oh)
    yield
    o = jnp.concatenate([_rms(o[h], hs(gain_ref[...], h)) for h in heads], axis=1)
    o_ref[:, 0:HG_W] = (o * hf_ref[:, 3 * HG_W:4 * HG_W]).astype(BF16)


def _sb_sum_matrix():
    j = np.arange(BLOCK)[:, None]
    s = np.arange(BLOCK)[None, :]
    w = np.concatenate([(j >= s), np.ones((BLOCK, BLOCK), bool)],
                       axis=1).astype(np.float32)
    return np.concatenate([w, w], axis=0)


def _softplus2(z):
    return jnp.maximum(z, 0.0) + jnp.log2(1.0 + jnp.exp2(_neg_abs(z)))


BLOCKS_PER_STEP = 4


def _mixer_kernel(hf_ref, rec_blk_ref, rec_hbm, gain_ref, m_ref, w_ref, o_ref,
                  st_ref, acc_ref, crep_ref, ring, kv_scr, sem, *, n_real_blk):
    n_blk = rec_hbm.shape[0]
    s = pl.program_id(0)
    subs = range(BLOCKS_PER_STEP)
    blk_of = lambda u: jnp.where(s == 0, 0, BLOCKS_PER_STEP * (s - 1) + 1 + u)
    rec_of = lambda u: rec_blk_ref.at[u]
    kv_of = lambda u: rec_blk_ref.at[u, REC_KT:REC_HV]
    rows_of = lambda u: slice(u * BLOCK, (u + 1) * BLOCK)

    def hgrn(u):
        return _hgrn_stages(hf_ref.at[rows_of(u)], rec_of(u), gain_ref, m_ref,
                            o_ref.at[rows_of(u)], st_ref)

    row = lax.broadcasted_iota(jnp.int32, (BLOCK, BLOCK), 0)
    col = lax.broadcasted_iota(jnp.int32, (BLOCK, BLOCK), 1)
    low = col < SB_DH
    pairs = range(SB_PAIRS)
    heads = range(SB_HEADS)

    def fold_stages(u, j_top, tiles, srcs, fresh, keep=True, hsel=(0, SB_HEADS)):
        c = blk_of(u)
        rec_ref, o_u = rec_of(u), o_ref.at[rows_of(u)]
        heads = range(*hsel)
        pairs = range(hsel[0] // 2, hsel[1] // 2)
        n = len(tiles)
        nrows = [t[1] for t in tiles]
        assert all(nr == BLOCK for nr in nrows[:-1])
        masks = []
        for i, (kind, nr) in enumerate(tiles):
            assert kind is None or nr == BLOCK
            if kind == "diag":
                m = col < row
            elif kind == "general":
                kpos = (j_top - i) * BLOCK + col
                m = jnp.logical_and(kpos < c * BLOCK + row, kpos >= PAD)
            else:
                m = None
            masks.append(m)
        z = {}
        for p in pairs:
            kt = jnp.concatenate([srcs[i][_blk(p), :] for i in range(n)],
                                 axis=1)
            zz = _dot(rec_ref[REC_QN + 2 * p * BLOCK:REC_QN + 2 * (p + 1) * BLOCK, :],
                      kt)
            for a in range(2):
                for i in range(n):
                    z[2 * p + a, i] = zz[_blk(a), _blk(i)][:nrows[i]]
        yield
        r = {}
        for i in range(n):
            packed = []
            for hd in heads:
                sp = _softplus2(z[hd, i])
                if masks[i] is not None:
                    sp = jnp.where(masks[i], sp, 0.0)
                packed.append(jnp.concatenate(_split2(sp), axis=1))
            r[i] = _dot(jnp.concatenate(packed, axis=0), w_ref[...])
            yield
        w = {}
        cmax = None
        for hd in heads:
            crep = None if fresh else crep_ref[u, hd]
            for i in range(n):
                nr = nrows[i]
                rh = r[i][(hd - hsel[0]) * nr:(hd - hsel[0] + 1) * nr]
                arg = z[hd, i] - rh[:, :BLOCK]
                wh = jnp.exp2(arg if crep is None else arg + crep[:nr])
                if masks[i] is not None:
                    wh = jnp.where(masks[i], wh, 0.0)
                w[hd, i] = wh.astype(BF16)
                if crep is None:
                    crep = -rh[:, BLOCK:]
                elif nr == BLOCK:
                    crep = crep - rh[:, BLOCK:]
                else:
                    crep = jnp.concatenate(
                        [crep[:nr] - rh[:, BLOCK:], crep[nr:]], axis=0)
            if keep:
                crep_ref[u, hd] = crep
            cmax = crep if cmax is None else jnp.maximum(cmax, crep)
        yield
        full = [i for i in range(n) if nrows[i] == BLOCK]
        for p in pairs:
            wp = jnp.concatenate(
                [jnp.concatenate([w[2 * p + a, i] for i in full], axis=1)
                 for a in range(2)], axis=0)
            vtile = lambda i: srcs[i][SB_W + p * BLOCK:SB_W + (p + 1) * BLOCK, :]
            pv = _dot(wp, jnp.concatenate([vtile(i) for i in full], axis=0))
            pv = jnp.where(low, pv[:BLOCK], pv[BLOCK:])
            if nrows[-1] < BLOCK:
                nr = nrows[-1]
                ps = _dot(jnp.concatenate([w[2 * p, n - 1], w[2 * p + 1, n - 1]],
                                          axis=0), vtile(n - 1))
                low_nr = lax.broadcasted_iota(jnp.int32, (nr, BLOCK), 1) < SB_DH
                ps = jnp.where(low_nr, ps[:nr], ps[nr:])
                pv = jnp.concatenate([pv[:nr] + ps, pv[nr:]], axis=0)
            if fresh:
                o_u[:, HG_W + p * BLOCK:HG_W + (p + 1) * BLOCK] = pv.astype(BF16)
                if keep:
                    acc_ref[u, p] = pv
            else:
                acc_ref[u, p] = acc_ref[u, p] + pv
        return jnp.max(cmax)

    n_fast = 3
    fast = s >= 2
    no_more = (jnp.float32(-jnp.inf),)

    fast_tiles = (("diag", BLOCK), (None, BLOCK), (None, FAR_ROWS))

    def fast_step():
        tiles = fast_tiles
        back = lambda u, i: kv_of(u - i) if u >= i else ring.at[u - i + 2]
        def delayed(gen, turns):
            for _ in range(turns):
                yield
            return (yield from gen)

        halves = [(0, SB_HEADS // 2), (SB_HEADS // 2, SB_HEADS)]
        res = _interleave(
            *[delayed(fold_stages(u, blk_of(u), tiles,
                                  [back(u, i) for i in range(3)], True, keep=False,
                                  hsel=hs_), 4 * u)
              for u in subs for hs_ in halves],
            *[delayed(hgrn(u), 4 * u + 2) for u in subs])
        for i in range(2):
            ring[i] = kv_of(BLOCKS_PER_STEP - 2 + i)[...]
        return tuple(jnp.maximum(res[2 * u], res[2 * u + 1]) for u in subs)

    def first_steps():
        @pl.when(s == 0)
        def _():
            st_ref[...] = jnp.zeros_like(st_ref)

        def start(u):
            if u >= n_fast - 1:
                tiles, srcs = fast_tiles, [kv_of(u - i) for i in range(3)]
            else:
                tiles, srcs = (("general", BLOCK),), [kv_of(u)]
            return lambda: (_interleave(
                fold_stages(u, blk_of(u), tiles, srcs, True,
                            keep=u < n_fast - 1), hgrn(u))[0],)
        res = start(0)()
        for u in subs[1:]:
            res = res + lax.cond(s >= 1, start(u), lambda: no_more)

        @pl.when(s == 0)
        def _():
            ring[1] = kv_of(0)[...]

        @pl.when(s > 0)
        def _():
            for i in range(2):
                ring[i] = kv_of(BLOCKS_PER_STEP - 2 + i)[...]
        return res

    first = lax.cond(fast, fast_step, first_steps)

    def fold(u, j_top, kind, fresh=False):
        pj = lax.rem(j_top + n_real_blk, n_blk)
        cp = pltpu.make_async_copy(rec_hbm.at[pj, REC_KT:REC_HV], kv_scr, sem.at[0])
        cp.start()
        cp.wait()
        return _interleave(fold_stages(u, j_top, ((kind, BLOCK),), [kv_scr],
                                       fresh))[0]

    def walk_back(u):
        cmax0 = first[u]
        c = blk_of(u)

        @pl.when(cmax0 > EXP2_ZERO_BELOW)
        def _():
            cmax1 = lax.cond(c >= n_fast,
                             lambda: fold(u, c, "general", fresh=True),
                             lambda: cmax0)

            def cond(carry):
                j, cmax = carry
                return jnp.logical_and(j >= 1, cmax > EXP2_ZERO_BELOW)

            def body(carry):
                j, _ = carry
                return j - 1, fold(u, j, None)

            j_end, cmax_end = lax.while_loop(cond, body, (c - 1, cmax1))

            @pl.when(jnp.logical_and(j_end == 0, cmax_end > EXP2_ZERO_BELOW))
            def _():
                fold(u, 0, "general")

            for p in pairs:
                o_ref[rows_of(u), HG_W + p * BLOCK:HG_W + (p + 1) * BLOCK] = (
                    acc_ref[u, p].astype(BF16))

    @pl.when(functools.reduce(jnp.maximum, first) > EXP2_ZERO_BELOW)
    def _():
        for u in subs:
            walk_back(u)


def _mixer(hf, rec, out_gain, n_real_blk):
    n_blk = rec.shape[0]
    lp = n_blk * BLOCK
    assert n_real_blk % BLOCKS_PER_STEP == 0 and n_blk == n_real_blk + 1
    n_steps = n_real_blk // BLOCKS_PER_STEP + 1
    tile = lambda s: (s + n_steps - 1) % n_steps
    rows = BLOCKS_PER_STEP * BLOCK
    whole = lambda a: pl.BlockSpec(a.shape, lambda s: (0,) * a.ndim)
    msum = jnp.asarray(_hgrn_sum_matrix(), dtype=BF16)
    wsum = jnp.asarray(_sb_sum_matrix(), dtype=BF16)
    return pl.pallas_call(
        functools.partial(_mixer_kernel, n_real_blk=n_real_blk),
        grid=(n_steps,),
        in_specs=[pl.BlockSpec((rows, 4 * HG_W), lambda s: (tile(s), 0)),
                  pl.BlockSpec((BLOCKS_PER_STEP, REC_ROWS, BLOCK),
                               lambda s: (tile(s), 0, 0)),
                  pl.BlockSpec(memory_space=pl.ANY),
                  whole(out_gain), whole(msum), whole(wsum)],
        out_specs=pl.BlockSpec((rows, HG_W + SB_W), lambda s: (tile(s), 0)),
        out_shape=jax.ShapeDtypeStruct((lp, HG_W + SB_W), BF16),
        scratch_shapes=[pltpu.VMEM((HG_HEADS, HG_DV, HG_DK), F32),
                        pltpu.VMEM((BLOCKS_PER_STEP, SB_PAIRS, BLOCK, BLOCK), F32),
                        pltpu.VMEM((BLOCKS_PER_STEP, SB_HEADS, BLOCK, BLOCK), F32),
                        pltpu.VMEM((2, KV_ROWS, BLOCK), BF16),
                        pltpu.VMEM((KV_ROWS, BLOCK), BF16),
                        pltpu.SemaphoreType.DMA((1,))],
        compiler_params=pltpu.CompilerParams(
            dimension_semantics=("arbitrary",)),
        name="mixer",
    )(hf, rec, rec, out_gain, msum, wsum)


def _ffn_out_kernel(h1_ref, o_ref, wo_ref, g2_ref, w2i_ref, w2o_ref, out_ref,
                    act_ref):
    h2 = h1_ref[...] + _dot(o_ref[...], wo_ref[...])
    xn = _rms(h2, g2_ref[...]).astype(BF16)
    out_ref[...] = h2 + 0.5 * _swiglu(xn, w2i_ref, w2o_ref, act_ref)


def _ffn_out(h1, o, wo, g2, w2i, w2o, n_rows, tm):
    const = lambda shape: pl.BlockSpec(shape, lambda i: (0,) * len(shape),
                                       pipeline_mode=pl.Buffered(1))
    rows = lambda w: pl.BlockSpec((tm, w), lambda i: (i, 0))
    return pl.pallas_call(
        _ffn_out_kernel,
        grid=(n_rows // tm,),
        in_specs=[rows(D_MODEL), rows(HG_W + SB_W),
                  const((HG_W + SB_W, D_MODEL)),
                  const((1, D_MODEL)), const((D_MODEL, 2 * D_FF)),
                  const((D_FF, D_MODEL))],
        out_specs=rows(D_MODEL),
        out_shape=jax.ShapeDtypeStruct((n_rows, D_MODEL), F32),
        scratch_shapes=[pltpu.VMEM((tm, D_FF), BF16)],
        compiler_params=pltpu.CompilerParams(
            dimension_semantics=("arbitrary",), vmem_limit_bytes=VMEM_LIMIT),
        name="ffn_out",
    )(h1, o, wo, g2, w2i, w2o)


def kernel(x, meta_tokens, ffn1_norm, ffn1_w_in, ffn1_w_out, mix_norm, w_in,
           hgrn_lb_logits, hgrn_out_norm, sb_q_norm, sb_k_norm, w_out,
           ffn2_norm, ffn2_w_in, ffn2_w_out):
    b, seq, _ = x.shape
    assert b == 1 and seq % BLOCK == 0
    assert ffn1_norm.shape[0] == 1, "single layer"
    n_real_blk = seq // BLOCK
    tm = 512
    assert seq % tm == 0

    meta_tile = jnp.zeros((BLOCK, D_MODEL), x.dtype).at[PAD:].set(
        meta_tokens.astype(x.dtype))

    gk = jnp.tile(sb_k_norm[0], SB_HEADS).reshape(SB_W, 1)
    gq = jnp.tile(sb_q_norm[0], 2).reshape(1, BLOCK)

    h1, hf, rec, wo, w2i, w2o = _ffn_in(
        x[0], meta_tile, ffn1_norm, ffn1_w_in[0].astype(BF16),
        ffn1_w_out[0].astype(BF16), mix_norm, w_in[0].astype(BF16), gk, gq,
        hgrn_lb_logits, (w_out[0], ffn2_w_in[0], ffn2_w_out[0]), tm)
    o = _mixer(hf, rec, hgrn_out_norm, n_real_blk)
    out = _ffn_out(h1, o, wo, ffn2_norm, w2i, w2o, seq, 2 * tm)
    return out[None]
```
